```python
import math
import jax, jax.numpy as jnp
from jax import lax
import numpy as np

D_MODEL = 1024
BATCH = 4
SEQ = 4096
DEPTH = 2

S5_WIDTH = D_MODEL // 2
S5_GROUP = 16
S5_GROUPS = S5_WIDTH // S5_GROUP
S5_STATE = 64
FOX_HEAD_DIM = 64
FOX_HEADS = (D_MODEL - S5_WIDTH) // FOX_HEAD_DIM
FOX_WIDTH = FOX_HEADS * FOX_HEAD_DIM
Q_BLOCK = 128
EVEN_IN = S5_WIDTH + 3 * FOX_WIDTH + FOX_HEADS
EVEN_MIX = S5_WIDTH + FOX_WIDTH
POOL_WIDTH = D_MODEL // 2
POOL_WINDOWS = (2, 4, 8, 16)
POOL_GROUPS = len(POOL_WINDOWS)
POOL_GROUP_DIM = POOL_WIDTH // POOL_GROUPS
SGU_WIDTH = D_MODEL // 2
SGU_GROUPS = 4
SGU_GROUP_DIM = SGU_WIDTH // SGU_GROUPS
CHUNK = 128
ODD_IN = POOL_WIDTH + 2 * SGU_WIDTH
ODD_MIX = POOL_WIDTH + SGU_WIDTH
D_FF = 4 * D_MODEL
N_EVEN = (DEPTH + 1) // 2
N_ODD = DEPTH // 2
EPS = 1e-6

kernel_name = 'hybrid_s5_fox_pool_sgu'


def rms_norm(x, g):
    xf = x.astype(jnp.float32)
    y = xf * lax.rsqrt(jnp.mean(xf * xf, axis=-1, keepdims=True) + EPS)
    return (y * g.astype(jnp.float32)).astype(x.dtype)


def layer_norm(x, g, b):
    xf = x.astype(jnp.float32)
    mu = jnp.mean(xf, axis=-1, keepdims=True)
    xc = xf - mu
    y = xc * lax.rsqrt(jnp.mean(xc * xc, axis=-1, keepdims=True) + EPS)
    return (y * g.astype(jnp.float32) + b.astype(jnp.float32)).astype(x.dtype)


def _complex_scan_combine(e_i, e_j):
    ar_i, ai_i, br_i, bi_i = e_i
    ar_j, ai_j, br_j, bi_j = e_j
    ar = ar_j * ar_i - ai_j * ai_i
    ai = ar_j * ai_i + ai_j * ar_i
    br = ar_j * br_i - ai_j * bi_i + br_j
    bi = ar_j * bi_i + ai_j * br_i + bi_j
    return (ar, ai, br, bi)


def s5_mixer(u, lam_re, lam_im, log_dt, b_re, b_im, c_re, c_im, d, w_glu):
    f32 = jnp.float32
    bsz, L, _ = u.shape
    uf = u.astype(f32)
    dt = jnp.exp(log_dt.astype(f32))[:, None]
    lr = lam_re.astype(f32)
    li = lam_im.astype(f32)
    mag = jnp.exp(lr * dt)
    ab_re = mag * jnp.cos(li * dt)
    ab_im = mag * jnp.sin(li * dt)
    den = lr * lr + li * li
    nr = ab_re - 1.0
    ni = ab_im
    q_re = (nr * lr + ni * li) / den
    q_im = (ni * lr - nr * li) / den
    br = b_re.astype(f32)
    bi = b_im.astype(f32)
    bb_re = q_re[..., None] * br - q_im[..., None] * bi
    bb_im = q_re[..., None] * bi + q_im[..., None] * br
    ut = jnp.swapaxes(uf.reshape(bsz, L, S5_GROUPS, S5_GROUP), 0, 1)
    bu_re = jnp.einsum('lbgh,gph->lbgp', ut, bb_re)
    bu_im = jnp.einsum('lbgh,gph->lbgp', ut, bb_im)
    a_re = jnp.broadcast_to(ab_re[None, None], (L, 1, S5_GROUPS, S5_STATE))
    a_im = jnp.broadcast_to(ab_im[None, None], (L, 1, S5_GROUPS, S5_STATE))
    _, _, x_re, x_im = lax.associative_scan(_complex_scan_combine, (a_re, a_im, bu_re, bu_im), axis=0)
    y = (jnp.einsum('lbgp,ghp->lbgh', x_re, c_re.astype(f32))
         - jnp.einsum('lbgp,ghp->lbgh', x_im, c_im.astype(f32)))
    y = jnp.swapaxes(y, 0, 1).reshape(bsz, L, S5_WIDTH) + d.astype(f32) * uf
    y = jax.nn.gelu(y)
    y = y * jax.nn.sigmoid(y @ w_glu.astype(f32))
    return y.astype(u.dtype)


def fox_attention(q, k, v, f_logit, b_f):
    f32 = jnp.float32
    bsz, L, H, Dh = q.shape
    log_f = jax.nn.log_sigmoid(f_logit.astype(f32) + b_f.astype(f32))
    F = jnp.cumsum(log_f, axis=1)
    F_k = jnp.transpose(F, (0, 2, 1))
    n_blk = L // Q_BLOCK
    q_blocks = jnp.swapaxes(q.reshape(bsz, n_blk, Q_BLOCK, H, Dh), 0, 1)
    F_blocks = jnp.swapaxes(F_k.reshape(bsz, H, n_blk, Q_BLOCK), 0, 2).swapaxes(1, 2)
    k_pos = jnp.arange(L)
    scale = Dh ** -0.5

    def block(args):
        i, qi, Fi = args
        s = jnp.einsum('bqhd,bkhd->bhqk', qi, k).astype(f32) * scale
        s = s + (Fi[..., None] - F_k[:, :, None, :])
        q_pos = i * Q_BLOCK + jnp.arange(Q_BLOCK)
        mask = k_pos[None, :] <= q_pos[:, None]
        s = jnp.where(mask[None, None], s, -jnp.inf)
        p = jax.nn.softmax(s, axis=-1).astype(v.dtype)
        return jnp.einsum('bhqk,bkhd->bqhd', p, v)

    out = lax.map(block, (jnp.arange(n_blk), q_blocks, F_blocks))
    return jnp.swapaxes(out, 0, 1).reshape(bsz, L, H * Dh)


def pool_mixer(xc, pool_w, pool_scale):
    f32 = jnp.float32
    bsz, L, _ = xc.shape
    xg = xc.astype(f32).reshape(bsz, L, POOL_GROUPS, POOL_GROUP_DIM)
    csum = jnp.cumsum(xg, axis=1)
    t = jnp.arange(L, dtype=f32)
    outs = []
    for g, w in enumerate(POOL_WINDOWS):
        cg = csum[:, :, g]
        lagged = jnp.pad(cg, ((0, 0), (w, 0), (0, 0)))[:, :L]
        cnt = jnp.minimum(t + 1.0, float(w))[None, :, None]
        outs.append((cg - lagged) / cnt - xg[:, :, g])
    pooled = jnp.stack(outs, axis=2)
    y = jnp.einsum('blgc,gcd->blgd', pooled, pool_w.astype(f32)).reshape(bsz, L, POOL_WIDTH)
    return (y * pool_scale.astype(f32)).astype(xc.dtype)


def sgu_mixer(u, v, ln_g, ln_b, w_s, b_s):
    bsz, L, _ = u.shape
    u = jax.nn.gelu(u)
    v = layer_norm(jax.nn.gelu(v), ln_g, ln_b)
    n_chunk = L // CHUNK
    vg = v.reshape(bsz, n_chunk, CHUNK, SGU_GROUPS, SGU_GROUP_DIM)
    causal = jnp.tril(jnp.ones((CHUNK, CHUNK), dtype=bool))
    ws = jnp.where(causal[None], w_s, jnp.zeros_like(w_s))
    mixed = jnp.einsum('gts,bnsgc->bntgc', ws, vg) + jnp.transpose(b_s)[None, None, :, :, None]
    return u * mixed.reshape(bsz, L, SGU_WIDTH)


def even_mixer(h, w_in, lam_re, lam_im, log_dt, b_re, b_im, c_re, c_im, d, w_glu, b_f, w_out):
    bsz, L, _ = h.shape
    z = h @ w_in
    s1 = S5_WIDTH
    s2 = s1 + FOX_WIDTH
    s3 = s2 + FOX_WIDTH
    s4 = s3 + FOX_WIDTH
    u, q, k, v, fl = z[..., :s1], z[..., s1:s2], z[..., s2:s3], z[..., s3:s4], z[..., s4:]
    y_a = s5_mixer(u, lam_re, lam_im, log_dt, b_re, b_im, c_re, c_im, d, w_glu)
    shp = (bsz, L, FOX_HEADS, FOX_HEAD_DIM)
    y_b = fox_attention(q.reshape(shp), k.reshape(shp), v.reshape(shp), fl, b_f)
    return jnp.concatenate([y_a, y_b], axis=-1) @ w_out


def odd_mixer(h, w_in, pool_w, pool_scale, ln_g, ln_b, w_s, b_s, w_out):
    z = h @ w_in
    s1 = POOL_WIDTH
    s2 = s1 + SGU_WIDTH
    xc, u, v = z[..., :s1], z[..., s1:s2], z[..., s2:]
    y_c = pool_mixer(xc, pool_w, pool_scale)
    y_d = sgu_mixer(u, v, ln_g, ln_b, w_s, b_s)
    return jnp.concatenate([y_c, y_d], axis=-1) @ w_out


def sq_relu_mlp(h, w1, w2):
    return jnp.square(jax.nn.relu(h @ w1)) @ w2


def setup_inputs(seed: int = 0) -> dict:
    key = jax.random.key(seed)
    ks = jax.random.split(key, 32)
    f32 = jnp.float32

    def nrm(k, shape, scale):
        return scale * jax.random.normal(k, shape, f32)

    G, P, H = S5_GROUPS, S5_STATE, S5_GROUP
    return {
        'x': nrm(ks[0], (BATCH, SEQ, D_MODEL), 1.0),
        'mix_pre_g': 1.0 + nrm(ks[1], (DEPTH, D_MODEL), 0.02),
        'mix_post_g': 1.0 + nrm(ks[2], (DEPTH, D_MODEL), 0.02),
        'mlp_pre_g': 1.0 + nrm(ks[3], (DEPTH, D_MODEL), 0.02),
        'mlp_post_g': 1.0 + nrm(ks[4], (DEPTH, D_MODEL), 0.02),
        'w_in_even': nrm(ks[5], (N_EVEN, D_MODEL, EVEN_IN), D_MODEL ** -0.5),
        's5_lam_re': -0.5 + nrm(ks[6], (N_EVEN, G, P), 0.01),
        's5_lam_im': jnp.pi * jnp.arange(P, dtype=f32) + nrm(ks[7], (N_EVEN, G, P), 0.01),
        's5_log_dt': jax.random.uniform(ks[8], (N_EVEN, G), f32, math.log(1e-3), math.log(1e-1)),
        's5_b_re': nrm(ks[9], (N_EVEN, G, P, H), (2 * H) ** -0.5),
        's5_b_im': nrm(ks[10], (N_EVEN, G, P, H), (2 * H) ** -0.5),
        's5_c_re': nrm(ks[11], (N_EVEN, G, H, P), P ** -0.5),
        's5_c_im': nrm(ks[12], (N_EVEN, G, H, P), P ** -0.5),
        's5_d': nrm(ks[13], (N_EVEN, S5_WIDTH), 1.0),
        's5_w_glu': nrm(ks[14], (N_EVEN, S5_WIDTH, S5_WIDTH), S5_WIDTH ** -0.5),
        'fox_b_f': jax.random.uniform(ks[15], (N_EVEN, FOX_HEADS), f32, 0.0, 3.0),
        'w_out_even': nrm(ks[16], (N_EVEN, EVEN_MIX, D_MODEL), EVEN_MIX ** -0.5),
        'w_in_odd': nrm(ks[17], (N_ODD, D_MODEL, ODD_IN), D_MODEL ** -0.5),
        'pool_w': nrm(ks[18], (N_ODD, POOL_GROUPS, POOL_GROUP_DIM, POOL_GROUP_DIM), POOL_GROUP_DIM ** -0.5),
        'pool_scale': 1.0 + nrm(ks[19], (N_ODD, POOL_WIDTH), 0.02),
        'sgu_ln_g': 1.0 + nrm(ks[20], (N_ODD, SGU_WIDTH), 0.02),
        'sgu_ln_b': nrm(ks[21], (N_ODD, SGU_WIDTH), 0.02),
        'sgu_w_s': nrm(ks[22], (N_ODD, SGU_GROUPS, CHUNK, CHUNK), CHUNK ** -0.5),
        'sgu_b_s': 1.0 + nrm(ks[23], (N_ODD, SGU_GROUPS, CHUNK), 0.1),
        'w_out_odd': nrm(ks[24], (N_ODD, ODD_MIX, D_MODEL), ODD_MIX ** -0.5),
        'mlp_w1': nrm(ks[25], (DEPTH, D_MODEL, D_FF), D_MODEL ** -0.5),
        'mlp_w2': nrm(ks[26], (DEPTH, D_FF, D_MODEL), D_FF ** -0.5),
    }


def reference(x, mix_pre_g, mix_post_g, mlp_pre_g, mlp_post_g,
              w_in_even, s5_lam_re, s5_lam_im, s5_log_dt, s5_b_re, s5_b_im, s5_c_re, s5_c_im,
              s5_d, s5_w_glu, fox_b_f, w_out_even,
              w_in_odd, pool_w, pool_scale, sgu_ln_g, sgu_ln_b, sgu_w_s, sgu_b_s, w_out_odd,
              mlp_w1, mlp_w2):
    for l in range(DEPTH):
        h = rms_norm(x, mix_pre_g[l])
        if l % 2 == 0:
            e = l // 2
            y = even_mixer(h, w_in_even[e], s5_lam_re[e], s5_lam_im[e], s5_log_dt[e],
                           s5_b_re[e], s5_b_im[e], s5_c_re[e], s5_c_im[e], s5_d[e], s5_w_glu[e],
                           fox_b_f[e], w_out_even[e])
        else:
            o = l // 2
            y = odd_mixer(h, w_in_odd[o], pool_w[o], pool_scale[o], sgu_ln_g[o], sgu_ln_b[o],
                          sgu_w_s[o], sgu_b_s[o], w_out_odd[o])
        x = x + rms_norm(y, mix_post_g[l])
        h = rms_norm(x, mlp_pre_g[l])
        x = x + rms_norm(sq_relu_mlp(h, mlp_w1[l], mlp_w2[l]), mlp_post_g[l])
    return x
```

```python
import functools

import numpy as np
import jax
import jax.numpy as jnp
from jax import lax
from jax.experimental import pallas as pl
from jax.experimental.pallas import tpu as pltpu

F32 = jnp.float32
BF16 = jnp.bfloat16

EPS = 1e-6
LOG2E = 1.4426950408889634
NEG_BIG = -1e30

LANES = 128
SUBLANES = 8
VMEM_LIMIT = 56 * 1024 * 1024

S5_GROUP = 16
S5_STATE = 64
HEAD_DIM = 64
POOL_WINDOWS = (2, 4, 8, 16)
CHUNK = 128

ROW_TILE = 512
S5_TILE = 128
ATT_TILE = 256


def _params(*sem):
    return pltpu.CompilerParams(dimension_semantics=sem, vmem_limit_bytes=VMEM_LIMIT)


def _const_spec(shape):
    nd = len(shape)
    return pl.BlockSpec(shape, lambda *_: (0,) * nd, pipeline_mode=pl.Buffered(1))


def _rms(x, g):
    return x * lax.rsqrt(jnp.mean(x * x, axis=-1, keepdims=True) + EPS) * g


def _split3(x):
    hi = x.astype(BF16)
    r1 = x - hi.astype(F32)
    mid = r1.astype(BF16)
    lo = (r1 - mid.astype(F32)).astype(BF16)
    return hi, mid, lo


def _dot(a, b):
    return jnp.dot(a, b, preferred_element_type=F32)


def _even_inproj_kernel(x_ref, g_ref, w_ref, u_ref, q_ref, k_ref, v_ref, f_ref):
    h = _rms(x_ref[0], g_ref[...]).astype(BF16)
    u_ref[...] = _dot(h, w_ref[:, 0:512])
    q_ref[0] = _dot(h, w_ref[:, 512:1024]).astype(BF16)
    k_ref[0] = _dot(h, w_ref[:, 1024:1536]).astype(BF16)
    v_ref[0] = _dot(h, w_ref[:, 1536:2048]).astype(BF16)
    f_ref[0] = _dot(h, w_ref[:, 2048:2176])


def _even_inproj(x, g, w):
    B, L, D = x.shape
    tm = min(ROW_TILE, L)
    tok = lambda b, r: (b, r, 0)
    return pl.pallas_call(
        _even_inproj_kernel,
        grid=(B, L // tm),
        in_specs=[pl.BlockSpec((1, tm, D), tok), _const_spec((1, D)), _const_spec(w.shape)],
        out_specs=[
            pl.BlockSpec((tm, 512), lambda b, r: (r, b)),
            pl.BlockSpec((1, tm, 512), tok),
            pl.BlockSpec((1, tm, 512), tok),
            pl.BlockSpec((1, tm, 512), tok),
            pl.BlockSpec((1, tm, LANES), tok),
        ],
        out_shape=[
            jax.ShapeDtypeStruct((L, B * 512), F32),
            jax.ShapeDtypeStruct((B, L, 512), BF16),
            jax.ShapeDtypeStruct((B, L, 512), BF16),
            jax.ShapeDtypeStruct((B, L, 512), BF16),
            jax.ShapeDtypeStruct((B, L, LANES), F32),
        ],
        compiler_params=_params("parallel", "parallel"),
        name="even_inproj",
    )(x, g, w)


def _fox_prep_kernel(f_ref, bf_ref, tri_ref, pq_ref, pk_ref, cq_ref, ck_ref,
                     fq_ref, fk_ref, carry_ref):
    @pl.when(pl.program_id(1) == 0)
    def _():
        carry_ref[...] = jnp.zeros_like(carry_ref)

    z = f_ref[0] + bf_ref[...]
    logf = jnp.minimum(z, 0.0) - jnp.log1p(jnp.exp(-jnp.abs(z)))
    tri = tri_ref[...]
    hi, mid, lo = _split3(logf)
    csum = _dot(tri, hi) + _dot(tri, mid) + _dot(tri, lo) + carry_ref[...]
    tm = csum.shape[0]
    carry_ref[...] = csum[tm - 1:tm, :]
    hi, mid, lo = _split3(csum * LOG2E)
    fq = _dot(hi, pq_ref[0]) + _dot(mid, pq_ref[1]) + _dot(lo, pq_ref[2]) + cq_ref[...]
    fk = _dot(hi, pk_ref[0]) + _dot(mid, pk_ref[1]) + _dot(lo, pk_ref[2]) + ck_ref[...]
    fq_ref[0] = fq.astype(BF16)
    fk_ref[0] = fk.astype(BF16)


def _fox_placement(n_heads):
    width = (n_heads // 2) * LANES
    pq = np.zeros((3, LANES, width), np.float32)
    pk = np.zeros((3, LANES, width), np.float32)
    cq = np.zeros((1, width), np.float32)
    ck = np.zeros((1, width), np.float32)
    for h in range(n_heads):
        base = (h // 2) * LANES + (HEAD_DIM if h % 2 == 0 else 0)
        for j in range(3):
            pq[j, h, base + j] = 1.0
            ck[0, base + j] = 1.0
            cq[0, base + 3 + j] = 1.0
            pk[j, h, base + 3 + j] = -1.0
    return pq, pk, cq, ck


def _fox_prep(fl, b_f):
    B, L, _ = fl.shape
    n_heads = b_f.shape[0]
    width = (n_heads // 2) * LANES
    tm = min(ROW_TILE, L)
    pq, pk, cq, ck = _fox_placement(n_heads)
    tri = jnp.asarray(np.tril(np.ones((tm, tm), np.float32)), BF16)
    bf = jnp.zeros((1, LANES), F32).at[0, :n_heads].set(b_f)
    tok = lambda b, r: (b, r, 0)
    return pl.pallas_call(
        _fox_prep_kernel,
        grid=(B, L // tm),
        in_specs=[pl.BlockSpec((1, tm, LANES), tok), _const_spec((1, LANES)),
                  _const_spec((tm, tm)), _const_spec(pq.shape), _const_spec(pk.shape),
                  _const_spec(cq.shape), _const_spec(ck.shape)],
        out_specs=[pl.BlockSpec((1, tm, width), tok), pl.BlockSpec((1, tm, width), tok)],
        out_shape=[jax.ShapeDtypeStruct((B, L, width), BF16)] * 2,
        scratch_shapes=[pltpu.VMEM((1, LANES), F32)],
        compiler_params=_params("parallel", "arbitrary"),
        name="fox_prep",
    )(fl, bf, tri, jnp.asarray(pq, BF16), jnp.asarray(pk, BF16), jnp.asarray(cq), jnp.asarray(ck))


def _fox_attn_kernel(q_ref, k_ref, v_ref, fq_ref, fk_ref, o_ref, kaug_ref, vaug_ref):
    qi = pl.program_id(2)
    tq = q_ref.shape[1]
    tk = tq
    L = k_ref.shape[1]

    @pl.when(qi == 0)
    def _():
        lanes = lax.broadcasted_iota(jnp.int32, (L, LANES), 1)
        k2, fk, v2 = k_ref[0].astype(F32), fk_ref[0].astype(F32), v_ref[0].astype(F32)
        kaug_ref[0] = jnp.where(lanes < HEAD_DIM, k2, fk).astype(BF16)
        kaug_ref[1] = jnp.where(lanes >= HEAD_DIM, k2, fk).astype(BF16)
        vaug_ref[0] = jnp.where(lanes < HEAD_DIM, v2, (lanes == HEAD_DIM).astype(F32)).astype(BF16)
        vaug_ref[1] = jnp.where(lanes >= HEAD_DIM, v2, (lanes == 0).astype(F32)).astype(BF16)

    lanes_q = lax.broadcasted_iota(jnp.int32, (tq, LANES), 1)
    q2, fq = q_ref[0].astype(F32), fq_ref[0].astype(F32)
    qa = (jnp.where(lanes_q < HEAD_DIM, q2, fq).astype(BF16),
          jnp.where(lanes_q >= HEAD_DIM, q2, fq).astype(BF16))

    def tile(j, carry, masked):
        start = pl.multiple_of(j * tk, tk)
        out = []
        for e in range(2):
            m, acc = carry[2 * e], carry[2 * e + 1]
            kt = kaug_ref[e, pl.ds(start, tk), :]
            s = lax.dot_general(qa[e], kt, (((1,), (1,)), ((), ())), preferred_element_type=F32)
            if masked:
                row = lax.broadcasted_iota(jnp.int32, (tq, tk), 0)
                col = lax.broadcasted_iota(jnp.int32, (tq, tk), 1)
                s = jnp.where(col <= row, s, NEG_BIG)
            m_new = jnp.maximum(m, jnp.max(s, axis=1, keepdims=True))
            alpha = jnp.exp2(m - m_new)
            p = jnp.exp2(s - m_new).astype(BF16)
            acc = alpha * acc + _dot(p, vaug_ref[e, pl.ds(start, tk), :])
            out += [m_new, acc]
        return tuple(out)

    init = (jnp.full((tq, 1), NEG_BIG, F32), jnp.zeros((tq, LANES), F32)) * 2
    carry = lax.fori_loop(0, qi, lambda j, c: tile(j, c, False), init)
    _, acc0, _, acc1 = tile(qi, carry, True)
    out0 = acc0 / acc0[:, HEAD_DIM:HEAD_DIM + 1]
    out1 = acc1 / acc1[:, 0:1]
    o_ref[0] = jnp.where(lanes_q < HEAD_DIM, out0, out1).astype(BF16)


def _fox_attn(q, k, v, fq, fk):
    B, L, W = q.shape
    n_pairs = W // LANES
    tq = min(ATT_TILE, L)
    qspec = pl.BlockSpec((1, tq, LANES), lambda b, p, i: (b, i, p))
    kspec = pl.BlockSpec((1, L, LANES), lambda b, p, i: (b, 0, p))
    return pl.pallas_call(
        _fox_attn_kernel,
        grid=(B, n_pairs, L // tq),
        in_specs=[qspec, kspec, kspec, qspec, kspec],
        out_specs=qspec,
        out_shape=jax.ShapeDtypeStruct((B, L, W), BF16),
        scratch_shapes=[pltpu.VMEM((2, L, LANES), BF16), pltpu.VMEM((2, L, LANES), BF16)],
        compiler_params=_params("parallel", "parallel", "arbitrary"),
        name="fox_attn",
    )(q, k, v, fq, fk)


def _s5_kernel(u_ref, bst_ref, cst_ref, are_ref, aim_ref, d_ref, o_ref, x_ref, st_ref):
    ti = u_ref.shape[0]
    ns = are_ref.shape[1]
    hw = u_ref.shape[2]

    @pl.when(pl.program_id(0) == 0)
    def _():
        st_ref[...] = jnp.zeros_like(st_ref)

    u = u_ref[...]
    half = lax.broadcasted_iota(jnp.int32, u.shape, 1) % 2
    zero = jnp.zeros_like(u)
    lhs = jnp.concatenate([jnp.where(half == 0, u, zero), jnp.where(half == 1, u, zero)], axis=-1)
    lhs = lhs.reshape(ti * SUBLANES, 2 * hw).astype(BF16)
    x_ref[...] = _dot(lhs, bst_ref[...]).reshape(ti, SUBLANES, 2 * ns)

    a_re, a_im = are_ref[...], aim_ref[...]

    def step(i, carry):
        s_re, s_im = carry
        n_re = a_re * s_re - a_im * s_im + x_ref[i, :, 0:ns]
        n_im = a_re * s_im + a_im * s_re + x_ref[i, :, ns:2 * ns]
        x_ref[i, :, 0:ns] = n_re
        x_ref[i, :, ns:2 * ns] = n_im
        return n_re, n_im

    s_re, s_im = lax.fori_loop(0, ti, step, (st_ref[:, 0:ns], st_ref[:, ns:2 * ns]), unroll=4)
    st_ref[:, 0:ns] = s_re
    st_ref[:, ns:2 * ns] = s_im

    xs = x_ref[...].reshape(ti * SUBLANES, 2 * ns).astype(BF16)
    y = _dot(xs, cst_ref[...]).reshape(ti, SUBLANES, 2 * hw)
    y = jnp.where(half == 0, y[:, :, 0:hw], y[:, :, hw:2 * hw])
    o_ref[...] = jax.nn.gelu(y + d_ref[...][None] * u)


def _s5_discretize(lam_re, lam_im, log_dt, b_re, b_im):
    dt = jnp.exp(log_dt)[:, None]
    mag = jnp.exp(lam_re * dt)
    ab_re = mag * jnp.cos(lam_im * dt)
    ab_im = mag * jnp.sin(lam_im * dt)
    den = lam_re * lam_re + lam_im * lam_im
    nr = ab_re - 1.0
    ni = ab_im
    q_re = (nr * lam_re + ni * lam_im) / den
    q_im = (ni * lam_re - nr * lam_im) / den
    bb_re = q_re[..., None] * b_re - q_im[..., None] * b_im
    bb_im = q_re[..., None] * b_im + q_im[..., None] * b_re
    return ab_re, ab_im, bb_re, bb_im


def _s5_matrices(ab_re, ab_im, bb_re, bb_im, c_re, c_im, d, batch):
    G, P, H = bb_re.shape
    gh = G // 2
    eye = jnp.eye(gh, dtype=F32)

    def bmat(bb):
        t = jnp.einsum('zgph,gk->zghkp', bb.reshape(2, gh, P, H), eye)
        return t.reshape(2 * gh * H, gh * P)

    def cmat(cc):
        t = jnp.einsum('zghp,gk->kpzgh', cc.reshape(2, gh, H, P), eye)
        return t.reshape(gh * P, 2 * gh * H)

    bst = jnp.concatenate([bmat(bb_re), bmat(bb_im)], axis=1).astype(BF16)
    cst = jnp.concatenate([cmat(c_re), -cmat(c_im)], axis=0).astype(BF16)
    a_re = jnp.tile(ab_re.reshape(2, gh * P), (batch, 1))
    a_im = jnp.tile(ab_im.reshape(2, gh * P), (batch, 1))
    d_t = jnp.tile(d.reshape(2, gh * H), (batch, 1))
    return bst, cst, a_re, a_im, d_t


def _s5_scan(u3, bst, cst, a_re, a_im, d_t):
    L, S, hw = u3.shape
    ns = a_re.shape[1]
    ti = min(S5_TILE, L)
    blk = pl.BlockSpec((ti, S, hw), lambda t: (t, 0, 0))
    return pl.pallas_call(
        _s5_kernel,
        grid=(L // ti,),
        in_specs=[blk, _const_spec(bst.shape), _const_spec(cst.shape),
                  _const_spec(a_re.shape), _const_spec(a_im.shape), _const_spec(d_t.shape)],
        out_specs=blk,
        out_shape=jax.ShapeDtypeStruct((L, S, hw), F32),
        scratch_shapes=[pltpu.VMEM((ti, S, 2 * ns), F32), pltpu.VMEM((S, 2 * ns), F32)],
        compiler_params=_params("arbitrary"),
        name="s5_scan",
    )(u3, bst, cst, a_re, a_im, d_t)


def _even_post_kernel(x_ref, ga_ref, yb_ref, wglu_ref, wout_ref, g_ref, o_ref):
    ga = ga_ref[...]
    ya = ga * jax.nn.sigmoid(_dot(ga.astype(BF16), wglu_ref[...]))
    half = wout_ref.shape[0] // 2
    mix = _dot(ya.astype(BF16), wout_ref[0:half, :]) + _dot(yb_ref[0], wout_ref[half:2 * half, :])
    o_ref[0] = x_ref[0] + _rms(mix, g_ref[...])


def _even_post(x, ga, yb, w_glu, w_out, g):
    B, L, D = x.shape
    tm = min(ROW_TILE, L)
    tok = lambda b, r: (b, r, 0)
    return pl.pallas_call(
        _even_post_kernel,
        grid=(B, L // tm),
        in_specs=[pl.BlockSpec((1, tm, D), tok),
                  pl.BlockSpec((tm, 512), lambda b, r: (r, b)),
                  pl.BlockSpec((1, tm, 512), tok),
                  _const_spec(w_glu.shape), _const_spec(w_out.shape), _const_spec((1, D))],
        out_specs=pl.BlockSpec((1, tm, D), tok),
        out_shape=jax.ShapeDtypeStruct((B, L, D), F32),
        compiler_params=_params("parallel", "parallel"),
        name="even_post",
    )(x, ga, yb, w_glu, w_out, g)


def _mlp_kernel(x_ref, gpre_ref, gpost_ref, w1_ref, w2_ref, o_ref):
    x = x_ref[...]
    h = _rms(x, gpre_ref[...]).astype(BF16)
    d_ff = w1_ref.shape[1]
    fc = 1024
    acc = jnp.zeros(x.shape, F32)
    for f in range(0, d_ff, fc):
        a = jnp.maximum(_dot(h, w1_ref[:, f:f + fc]), 0.0)
        acc = acc + _dot((a * a).astype(BF16), w2_ref[f:f + fc, :])
    o_ref[...] = x + _rms(acc, gpost_ref[...])


def _mlp(x2, g_pre, g_post, w1, w2):
    N, D = x2.shape
    tm = min(ROW_TILE, N)
    row = pl.BlockSpec((tm, D), lambda r: (r, 0))
    return pl.pallas_call(
        _mlp_kernel,
        grid=(N // tm,),
        in_specs=[row, _const_spec((1, D)), _const_spec((1, D)),
                  _const_spec(w1.shape), _const_spec(w2.shape)],
        out_specs=row,
        out_shape=jax.ShapeDtypeStruct((N, D), F32),
        compiler_params=_params("parallel"),
        name="mlp",
    )(x2, g_pre, g_post, w1, w2)


def _odd_mixer_kernel(x_ref, gpre_ref, win_ref, pw_ref, ps_ref, lng_ref, lnb_ref, ws_ref,
                      bs_ref, wout_ref, gpost_ref, o_ref, pool_ref):
    r = pl.program_id(1)
    x = x_ref[0]
    tm = x.shape[0]
    pad = max(POOL_WINDOWS)
    gd = LANES
    n_grp = len(POOL_WINDOWS)
    width = n_grp * gd

    h = _rms(x, gpre_ref[...]).astype(BF16)
    xc = _dot(h, win_ref[:, 0:width])
    u = _dot(h, win_ref[:, width:2 * width])
    v = _dot(h, win_ref[:, 2 * width:3 * width])

    @pl.when(r == 0)
    def _():
        pool_ref[0:pad, :] = jnp.zeros((pad, width), F32)

    pool_ref[pad:pad + tm, :] = xc
    t = r * tm + lax.broadcasted_iota(jnp.int32, (tm, 1), 0)
    yc = []
    for g, w in enumerate(POOL_WINDOWS):
        cols = slice(g * gd, (g + 1) * gd)
        tot = xc[:, cols]
        for lag in range(1, w):
            tot = tot + pool_ref[pl.ds(pad - lag, tm), cols]
        cnt = jnp.minimum(t + 1, w).astype(F32)
        pooled = tot / cnt - xc[:, cols]
        yc.append(_dot(pooled.astype(BF16), pw_ref[g]))
    pool_ref[0:pad, :] = pool_ref[tm:tm + pad, :]
    y_c = jnp.concatenate(yc, axis=-1) * ps_ref[...]

    gv = jax.nn.gelu(v)
    mu = jnp.mean(gv, axis=-1, keepdims=True)
    vc = gv - mu
    vn = vc * lax.rsqrt(jnp.mean(vc * vc, axis=-1, keepdims=True) + EPS)
    vn = (vn * lng_ref[...] + lnb_ref[...]).astype(BF16)
    rows = lax.broadcasted_iota(jnp.int32, (CHUNK, CHUNK), 0)
    cols_i = lax.broadcasted_iota(jnp.int32, (CHUNK, CHUNK), 1)
    ws = [jnp.where(cols_i <= rows, ws_ref[g], 0.0).astype(BF16) for g in range(n_grp)]
    chunks = []
    for n in range(tm // CHUNK):
        parts = [_dot(ws[g], vn[n * CHUNK:(n + 1) * CHUNK, g * gd:(g + 1) * gd]) for g in range(n_grp)]
        chunks.append(jnp.concatenate(parts, axis=-1) + bs_ref[...])
    y_d = jax.nn.gelu(u) * jnp.concatenate(chunks, axis=0)

    mix = _dot(y_c.astype(BF16), wout_ref[0:width, :]) + _dot(y_d.astype(BF16), wout_ref[width:2 * width, :])
    o_ref[0] = x + _rms(mix, gpost_ref[...])


def _odd_mixer(x, g_pre, w_in, pool_w, pool_scale, ln_g, ln_b, w_s, bs_full, w_out, g_post):
    B, L, D = x.shape
    tm = min(ROW_TILE, L)
    width = pool_scale.shape[1]
    tok = lambda b, r: (b, r, 0)
    consts = (g_pre, w_in, pool_w, pool_scale, ln_g, ln_b, w_s, bs_full, w_out, g_post)
    return pl.pallas_call(
        _odd_mixer_kernel,
        grid=(B, L // tm),
        in_specs=[pl.BlockSpec((1, tm, D), tok)] + [_const_spec(c.shape) for c in consts],
        out_specs=pl.BlockSpec((1, tm, D), tok),
        out_shape=jax.ShapeDtypeStruct((B, L, D), F32),
        scratch_shapes=[pltpu.VMEM((tm + max(POOL_WINDOWS), width), F32)],
        compiler_params=_params("parallel", "arbitrary"),
        name="odd_mixer",
    )(x, *consts)


def _even_layer(x, g_pre, g_post, w_in, lam_re, lam_im, log_dt, b_re, b_im, c_re, c_im, d, w_glu,
                b_f, w_out):
    B, L, D = x.shape
    n_heads = b_f.shape[0]
    s5w = d.shape[0]
    fw = n_heads * HEAD_DIM
    qs = HEAD_DIM ** -0.5 * LOG2E
    w = jnp.concatenate([
        w_in[:, :s5w], w_in[:, s5w:s5w + fw] * qs, w_in[:, s5w + fw:s5w + 3 * fw],
        jnp.pad(w_in[:, s5w + 3 * fw:], ((0, 0), (0, LANES - n_heads)))], axis=1).astype(BF16)
    u, q, k, v, fl = _even_inproj(x, g_pre[None], w)

    fq, fk = _fox_prep(fl, b_f)
    y_b = _fox_attn(q, k, v, fq, fk)

    ab_re, ab_im, bb_re, bb_im = _s5_discretize(lam_re, lam_im, log_dt, b_re, b_im)
    bst, cst, a_re, a_im, d_t = _s5_matrices(ab_re, ab_im, bb_re, bb_im, c_re, c_im, d, B)
    ga = _s5_scan(u.reshape(L, 2 * B, s5w // 2), bst, cst, a_re, a_im, d_t).reshape(L, B * s5w)

    return _even_post(x, ga, y_b, w_glu.astype(BF16), w_out.astype(BF16), g_post[None])


def _odd_layer(x, g_pre, g_post, w_in, pool_w, pool_scale, ln_g, ln_b, w_s, b_s, w_out):
    bs_full = jnp.repeat(jnp.transpose(b_s), LANES, axis=1)
    return _odd_mixer(x, g_pre[None], w_in.astype(BF16), pool_w.astype(BF16), pool_scale[None],
                      ln_g[None], ln_b[None], w_s, bs_full, w_out.astype(BF16), g_post[None])


def kernel(x, mix_pre_g, mix_post_g, mlp_pre_g, mlp_post_g, w_in_even, s5_lam_re, s5_lam_im, s5_log_dt, s5_b_re, s5_b_im, s5_c_re, s5_c_im, s5_d, s5_w_glu, fox_b_f, w_out_even, w_in_odd, pool_w, pool_scale, sgu_ln_g, sgu_ln_b, sgu_w_s, sgu_b_s, w_out_odd, mlp_w1, mlp_w2):
    B, L, D = x.shape
    depth = mix_pre_g.shape[0]
    for l in range(depth):
        i = l // 2
        if l % 2 == 0:
            x = _even_layer(x, mix_pre_g[l], mix_post_g[l], w_in_even[i], s5_lam_re[i], s5_lam_im[i],
                            s5_log_dt[i], s5_b_re[i], s5_b_im[i], s5_c_re[i], s5_c_im[i], s5_d[i],
                            s5_w_glu[i], fox_b_f[i], w_out_even[i])
        else:
            x = _odd_layer(x, mix_pre_g[l], mix_post_g[l], w_in_odd[i], pool_w[i], pool_scale[i],
                           sgu_ln_g[i], sgu_ln_b[i], sgu_w_s[i], sgu_b_s[i], w_out_odd[i])
        x = _mlp(x.reshape(B * L, D), mlp_pre_g[l][None], mlp_post_g[l][None],
                 mlp_w1[l].astype(BF16), mlp_w2[l].astype(BF16)).reshape(B, L, D)
    return x
```

```python
import functools

import numpy as np
import jax
import jax.numpy as jnp
from jax import lax
from jax.experimental import pallas as pl
from jax.experimental.pallas import tpu as pltpu

F32 = jnp.float32
BF16 = jnp.bfloat16

EPS = 1e-6
LOG2E = 1.4426950408889634
NEG_BIG = -1e30

LANES = 128
SUBLANES = 8
VMEM_LIMIT = 56 * 1024 * 1024

S5_GROUP = 16
S5_STATE = 64
HEAD_DIM = 64
POOL_WINDOWS = (2, 4, 8, 16)
CHUNK = 128

ROW_TILE = 512
S5_TILE = 128
ATT_TQ = 1024
ATT_TK = 512


def _params(*sem):
    return pltpu.CompilerParams(dimension_semantics=sem, vmem_limit_bytes=VMEM_LIMIT)


def _const_spec(shape):
    nd = len(shape)
    return pl.BlockSpec(shape, lambda *_: (0,) * nd, pipeline_mode=pl.Buffered(1))


def _rms(x, g):
    return x * lax.rsqrt(jnp.mean(x * x, axis=-1, keepdims=True) + EPS) * g


def _split3(x):
    hi = x.astype(BF16)
    r1 = x - hi.astype(F32)
    mid = r1.astype(BF16)
    lo = (r1 - mid.astype(F32)).astype(BF16)
    return hi, mid, lo


def _dot(a, b):
    return jnp.dot(a, b, preferred_element_type=F32)


def _even_inproj_kernel(x_ref, g_ref, w_ref, u_ref, q_ref, k_ref, v_ref, f_ref):
    h = _rms(x_ref[0], g_ref[...]).astype(BF16)
    u_ref[...] = _dot(h, w_ref[:, 0:512])
    q_ref[0] = _dot(h, w_ref[:, 512:1024]).astype(BF16)
    k_ref[0] = _dot(h, w_ref[:, 1024:1536]).astype(BF16)
    v_ref[0] = _dot(h, w_ref[:, 1536:2048]).astype(BF16)
    f_ref[0] = _dot(h, w_ref[:, 2048:2176])


def _even_inproj(x, g, w):
    B, L, D = x.shape
    tm = min(ROW_TILE, L)
    tok = lambda b, r: (b, r, 0)
    return pl.pallas_call(
        _even_inproj_kernel,
        grid=(B, L // tm),
        in_specs=[pl.BlockSpec((1, tm, D), tok), _const_spec((1, D)), _const_spec(w.shape)],
        out_specs=[
            pl.BlockSpec((tm, 512), lambda b, r: (r, b)),
            pl.BlockSpec((1, tm, 512), tok),
            pl.BlockSpec((1, tm, 512), tok),
            pl.BlockSpec((1, tm, 512), tok),
            pl.BlockSpec((1, tm, LANES), tok),
        ],
        out_shape=[
            jax.ShapeDtypeStruct((L, B * 512), F32),
            jax.ShapeDtypeStruct((B, L, 512), BF16),
            jax.ShapeDtypeStruct((B, L, 512), BF16),
            jax.ShapeDtypeStruct((B, L, 512), BF16),
            jax.ShapeDtypeStruct((B, L, LANES), F32),
        ],
        compiler_params=_params("parallel", "parallel"),
        name="even_inproj",
    )(x, g, w)


def _fox_prep_kernel(f_ref, bf_ref, tri_ref, pq_ref, pk_ref, cq_ref, ck_ref,
                     fq_ref, fk_ref, carry_ref):
    @pl.when(pl.program_id(1) == 0)
    def _():
        carry_ref[...] = jnp.zeros_like(carry_ref)

    z = f_ref[0] + bf_ref[...]
    logf = jnp.minimum(z, 0.0) - jnp.log1p(jnp.exp(-jnp.abs(z)))
    tri = tri_ref[...]
    hi, mid, lo = _split3(logf)
    csum = _dot(tri, hi) + _dot(tri, mid) + _dot(tri, lo) + carry_ref[...]
    tm = csum.shape[0]
    carry_ref[...] = csum[tm - 1:tm, :]
    hi, mid, lo = _split3(csum * LOG2E)
    fq = _dot(hi, pq_ref[0]) + _dot(mid, pq_ref[1]) + _dot(lo, pq_ref[2]) + cq_ref[...]
    fk = _dot(hi, pk_ref[0]) + _dot(mid, pk_ref[1]) + _dot(lo, pk_ref[2]) + ck_ref[...]
    fq_ref[0] = fq.astype(BF16)
    fk_ref[0] = fk.astype(BF16)


def _fox_placement(n_heads):
    width = (n_heads // 2) * LANES
    pq = np.zeros((3, LANES, width), np.float32)
    pk = np.zeros((3, LANES, width), np.float32)
    cq = np.zeros((1, width), np.float32)
    ck = np.zeros((1, width), np.float32)
    for h in range(n_heads):
        base = (h // 2) * LANES + (HEAD_DIM if h % 2 == 0 else 0)
        for j in range(3):
            pq[j, h, base + j] = 1.0
            ck[0, base + j] = 1.0
            cq[0, base + 3 + j] = 1.0
            pk[j, h, base + 3 + j] = -1.0
    return pq, pk, cq, ck


def _fox_prep(fl, b_f):
    B, L, _ = fl.shape
    n_heads = b_f.shape[0]
    width = (n_heads // 2) * LANES
    tm = min(ROW_TILE, L)
    pq, pk, cq, ck = _fox_placement(n_heads)
    tri = jnp.asarray(np.tril(np.ones((tm, tm), np.float32)), BF16)
    bf = jnp.zeros((1, LANES), F32).at[0, :n_heads].set(b_f)
    tok = lambda b, r: (b, r, 0)
    return pl.pallas_call(
        _fox_prep_kernel,
        grid=(B, L // tm),
        in_specs=[pl.BlockSpec((1, tm, LANES), tok), _const_spec((1, LANES)),
                  _const_spec((tm, tm)), _const_spec(pq.shape), _const_spec(pk.shape),
                  _const_spec(cq.shape), _const_spec(ck.shape)],
        out_specs=[pl.BlockSpec((1, tm, width), tok), pl.BlockSpec((1, tm, width), tok)],
        out_shape=[jax.ShapeDtypeStruct((B, L, width), BF16)] * 2,
        scratch_shapes=[pltpu.VMEM((1, LANES), F32)],
        compiler_params=_params("parallel", "arbitrary"),
        name="fox_prep",
    )(fl, bf, tri, jnp.asarray(pq, BF16), jnp.asarray(pk, BF16), jnp.asarray(cq), jnp.asarray(ck))


def _fox_attn_kernel(q_ref, k_ref, v_ref, fq_ref, fk_ref, o_ref, kaug_ref, vaug_ref):
    qi = pl.program_id(2)
    tq = q_ref.shape[1]
    L = k_ref.shape[1]
    tk = min(ATT_TK, L)
    assert tq % tk == 0

    @pl.when(qi == 0)
    def _():
        lanes = lax.broadcasted_iota(jnp.int32, (L, LANES), 1)
        k2, fk, v2 = k_ref[0].astype(F32), fk_ref[0].astype(F32), v_ref[0].astype(F32)
        kaug_ref[0] = jnp.where(lanes < HEAD_DIM, k2, fk).astype(BF16)
        kaug_ref[1] = jnp.where(lanes >= HEAD_DIM, k2, fk).astype(BF16)
        vaug_ref[0] = jnp.where(lanes < HEAD_DIM, v2, (lanes == HEAD_DIM).astype(F32)).astype(BF16)
        vaug_ref[1] = jnp.where(lanes >= HEAD_DIM, v2, (lanes == 0).astype(F32)).astype(BF16)

    lanes_q = lax.broadcasted_iota(jnp.int32, (tq, LANES), 1)
    q2, fq = q_ref[0].astype(F32), fq_ref[0].astype(F32)
    qa = (jnp.where(lanes_q < HEAD_DIM, q2, fq).astype(BF16),
          jnp.where(lanes_q >= HEAD_DIM, q2, fq).astype(BF16))

    def tile(j, carry, masked):
        start = pl.multiple_of(j * tk, tk)
        out = []
        for e in range(2):
            m, acc = carry[2 * e], carry[2 * e + 1]
            kt = kaug_ref[e, pl.ds(start, tk), :]
            s = lax.dot_general(qa[e], kt, (((1,), (1,)), ((), ())), preferred_element_type=F32)
            if masked:
                row = qi * tq + lax.broadcasted_iota(jnp.int32, (tq, tk), 0)
                col = start + lax.broadcasted_iota(jnp.int32, (tq, tk), 1)
                s = jnp.where(col <= row, s, NEG_BIG)
            m_new = jnp.maximum(m, jnp.max(s, axis=1, keepdims=True))
            alpha = jnp.exp2(m - m_new)
            p = jnp.exp2(s - m_new).astype(BF16)
            acc = alpha * acc + _dot(p, vaug_ref[e, pl.ds(start, tk), :])
            out += [m_new, acc]
        return tuple(out)

    init = (jnp.full((tq, 1), NEG_BIG, F32), jnp.zeros((tq, LANES), F32)) * 2
    n_diag = tq // tk
    carry = lax.fori_loop(0, qi * n_diag, lambda j, c: tile(j, c, False), init)
    for d in range(n_diag):
        carry = tile(qi * n_diag + d, carry, True)
    _, acc0, _, acc1 = carry
    out0 = acc0 / acc0[:, HEAD_DIM:HEAD_DIM + 1]
    out1 = acc1 / acc1[:, 0:1]
    o_ref[0] = jnp.where(lanes_q < HEAD_DIM, out0, out1).astype(BF16)


def _fox_attn(q, k, v, fq, fk):
    B, L, W = q.shape
    n_pairs = W // LANES
    tq = min(ATT_TQ, L)
    qspec = pl.BlockSpec((1, tq, LANES), lambda b, p, i: (b, i, p))
    kspec = pl.BlockSpec((1, L, LANES), lambda b, p, i: (b, 0, p))
    return pl.pallas_call(
        _fox_attn_kernel,
        grid=(B, n_pairs, L // tq),
        in_specs=[qspec, kspec, kspec, qspec, kspec],
        out_specs=qspec,
        out_shape=jax.ShapeDtypeStruct((B, L, W), BF16),
        scratch_shapes=[pltpu.VMEM((2, L, LANES), BF16), pltpu.VMEM((2, L, LANES), BF16)],
        compiler_params=_params("parallel", "parallel", "arbitrary"),
        name="fox_attn",
    )(q, k, v, fq, fk)


def _s5_kernel(u_ref, bst_ref, cst_ref, are_ref, aim_ref, d_ref, o_ref, x_ref, st_ref):
    ti = u_ref.shape[0]
    ns = are_ref.shape[1]
    hw = u_ref.shape[2]

    @pl.when(pl.program_id(0) == 0)
    def _():
        st_ref[...] = jnp.zeros_like(st_ref)

    u = u_ref[...]
    half = lax.broadcasted_iota(jnp.int32, u.shape, 1) % 2
    zero = jnp.zeros_like(u)
    lhs = jnp.concatenate([jnp.where(half == 0, u, zero), jnp.where(half == 1, u, zero)], axis=-1)
    lhs = lhs.reshape(ti * SUBLANES, 2 * hw).astype(BF16)
    x_ref[...] = _dot(lhs, bst_ref[...]).reshape(ti, SUBLANES, 2 * ns)

    a_re, a_im = are_ref[...], aim_ref[...]

    def step(i, carry):
        s_re, s_im = carry
        n_re = a_re * s_re - a_im * s_im + x_ref[i, :, 0:ns]
        n_im = a_re * s_im + a_im * s_re + x_ref[i, :, ns:2 * ns]
        x_ref[i, :, 0:ns] = n_re
        x_ref[i, :, ns:2 * ns] = n_im
        return n_re, n_im

    s_re, s_im = lax.fori_loop(0, ti, step, (st_ref[:, 0:ns], st_ref[:, ns:2 * ns]), unroll=4)
    st_ref[:, 0:ns] = s_re
    st_ref[:, ns:2 * ns] = s_im

    xs = x_ref[...].reshape(ti * SUBLANES, 2 * ns).astype(BF16)
    y = _dot(xs, cst_ref[...]).reshape(ti, SUBLANES, 2 * hw)
    y = jnp.where(half == 0, y[:, :, 0:hw], y[:, :, hw:2 * hw])
    o_ref[...] = jax.nn.gelu(y + d_ref[...][None] * u)


def _s5_discretize(lam_re, lam_im, log_dt, b_re, b_im):
    dt = jnp.exp(log_dt)[:, None]
    mag = jnp.exp(lam_re * dt)
    ab_re = mag * jnp.cos(lam_im * dt)
    ab_im = mag * jnp.sin(lam_im * dt)
    den = lam_re * lam_re + lam_im * lam_im
    nr = ab_re - 1.0
    ni = ab_im
    q_re = (nr * lam_re + ni * lam_im) / den
    q_im = (ni * lam_re - nr * lam_im) / den
    bb_re = q_re[..., None] * b_re - q_im[..., None] * b_im
    bb_im = q_re[..., None] * b_im + q_im[..., None] * b_re
    return ab_re, ab_im, bb_re, bb_im


def _s5_matrices(ab_re, ab_im, bb_re, bb_im, c_re, c_im, d, batch):
    G, P, H = bb_re.shape
    gh = G // 2
    eye = jnp.eye(gh, dtype=F32)

    def bmat(bb):
        t = jnp.einsum('zgph,gk->zghkp', bb.reshape(2, gh, P, H), eye)
        return t.reshape(2 * gh * H, gh * P)

    def cmat(cc):
        t = jnp.einsum('zghp,gk->kpzgh', cc.reshape(2, gh, H, P), eye)
        return t.reshape(gh * P, 2 * gh * H)

    bst = jnp.concatenate([bmat(bb_re), bmat(bb_im)], axis=1).astype(BF16)
    cst = jnp.concatenate([cmat(c_re), -cmat(c_im)], axis=0).astype(BF16)
    a_re = jnp.tile(ab_re.reshape(2, gh * P), (batch, 1))
    a_im = jnp.tile(ab_im.reshape(2, gh * P), (batch, 1))
    d_t = jnp.tile(d.reshape(2, gh * H), (batch, 1))
    return bst, cst, a_re, a_im, d_t


def _s5_scan(u3, bst, cst, a_re, a_im, d_t):
    L, S, hw = u3.shape
    ns = a_re.shape[1]
    ti = min(S5_TILE, L)
    blk = pl.BlockSpec((ti, S, hw), lambda t: (t, 0, 0))
    return pl.pallas_call(
        _s5_kernel,
        grid=(L // ti,),
        in_specs=[blk, _const_spec(bst.shape), _const_spec(cst.shape),
                  _const_spec(a_re.shape), _const_spec(a_im.shape), _const_spec(d_t.shape)],
        out_specs=blk,
        out_shape=jax.ShapeDtypeStruct((L, S, hw), F32),
        scratch_shapes=[pltpu.VMEM((ti, S, 2 * ns), F32), pltpu.VMEM((S, 2 * ns), F32)],
        compiler_params=_params("arbitrary"),
        name="s5_scan",
    )(u3, bst, cst, a_re, a_im, d_t)


def _even_post_kernel(x_ref, ga_ref, yb_ref, wglu_ref, wout_ref, g_ref, o_ref):
    ga = ga_ref[...]
    ya = ga * jax.nn.sigmoid(_dot(ga.astype(BF16), wglu_ref[...]))
    half = wout_ref.shape[0] // 2
    mix = _dot(ya.astype(BF16), wout_ref[0:half, :]) + _dot(yb_ref[0], wout_ref[half:2 * half, :])
    o_ref[0] = x_ref[0] + _rms(mix, g_ref[...])


def _even_post(x, ga, yb, w_glu, w_out, g):
    B, L, D = x.shape
    tm = min(ROW_TILE, L)
    tok = lambda b, r: (b, r, 0)
    return pl.pallas_call(
        _even_post_kernel,
        grid=(B, L // tm),
        in_specs=[pl.BlockSpec((1, tm, D), tok),
                  pl.BlockSpec((tm, 512), lambda b, r: (r, b)),
                  pl.BlockSpec((1, tm, 512), tok),
                  _const_spec(w_glu.shape), _const_spec(w_out.shape), _const_spec((1, D))],
        out_specs=pl.BlockSpec((1, tm, D), tok),
        out_shape=jax.ShapeDtypeStruct((B, L, D), F32),
        compiler_params=_params("parallel", "parallel"),
        name="even_post",
    )(x, ga, yb, w_glu, w_out, g)


def _mlp_kernel(x_ref, gpre_ref, gpost_ref, w1_ref, w2_ref, o_ref):
    x = x_ref[...]
    h = _rms(x, gpre_ref[...]).astype(BF16)
    d_ff = w1_ref.shape[1]
    fc = 1024
    acc = jnp.zeros(x.shape, F32)
    for f in range(0, d_ff, fc):
        a = jnp.maximum(_dot(h, w1_ref[:, f:f + fc]), 0.0)
        acc = acc + _dot((a * a).astype(BF16), w2_ref[f:f + fc, :])
    o_ref[...] = x + _rms(acc, gpost_ref[...])


def _mlp(x2, g_pre, g_post, w1, w2):
    N, D = x2.shape
    tm = min(ROW_TILE, N)
    row = pl.BlockSpec((tm, D), lambda r: (r, 0))
    return pl.pallas_call(
        _mlp_kernel,
        grid=(N // tm,),
        in_specs=[row, _const_spec((1, D)), _const_spec((1, D)),
                  _const_spec(w1.shape), _const_spec(w2.shape)],
        out_specs=row,
        out_shape=jax.ShapeDtypeStruct((N, D), F32),
        compiler_params=_params("parallel"),
        name="mlp",
    )(x2, g_pre, g_post, w1, w2)


def _odd_mixer_kernel(x_ref, gpre_ref, win_ref, pw_ref, ps_ref, lng_ref, lnb_ref, ws_ref,
                      bs_ref, wout_ref, gpost_ref, o_ref, pool_ref):
    r = pl.program_id(1)
    x = x_ref[0]
    tm = x.shape[0]
    pad = max(POOL_WINDOWS)
    gd = LANES
    n_grp = len(POOL_WINDOWS)
    width = n_grp * gd

    h = _rms(x, gpre_ref[...]).astype(BF16)
    xc = _dot(h, win_ref[:, 0:width])
    u = _dot(h, win_ref[:, width:2 * width])
    v = _dot(h, win_ref[:, 2 * width:3 * width])

    @pl.when(r == 0)
    def _():
        pool_ref[0:pad, :] = jnp.zeros((pad, width), F32)

    pool_ref[pad:pad + tm, :] = xc
    t = r * tm + lax.broadcasted_iota(jnp.int32, (tm, 1), 0)
    yc = []
    for g, w in enumerate(POOL_WINDOWS):
        cols = slice(g * gd, (g + 1) * gd)
        tot = xc[:, cols]
        for lag in range(1, w):
            tot = tot + pool_ref[pl.ds(pad - lag, tm), cols]
        cnt = jnp.minimum(t + 1, w).astype(F32)
        pooled = tot / cnt - xc[:, cols]
        yc.append(_dot(pooled.astype(BF16), pw_ref[g]))
    pool_ref[0:pad, :] = pool_ref[tm:tm + pad, :]
    y_c = jnp.concatenate(yc, axis=-1) * ps_ref[...]

    gv = jax.nn.gelu(v)
    mu = jnp.mean(gv, axis=-1, keepdims=True)
    vc = gv - mu
    vn = vc * lax.rsqrt(jnp.mean(vc * vc, axis=-1, keepdims=True) + EPS)
    vn = (vn * lng_ref[...] + lnb_ref[...]).astype(BF16)
    rows = lax.broadcasted_iota(jnp.int32, (CHUNK, CHUNK), 0)
    cols_i = lax.broadcasted_iota(jnp.int32, (CHUNK, CHUNK), 1)
    ws = [jnp.where(cols_i <= rows, ws_ref[g], 0.0).astype(BF16) for g in range(n_grp)]
    chunks = []
    for n in range(tm // CHUNK):
        parts = [_dot(ws[g], vn[n * CHUNK:(n + 1) * CHUNK, g * gd:(g + 1) * gd]) for g in range(n_grp)]
        chunks.append(jnp.concatenate(parts, axis=-1) + bs_ref[...])
    y_d = jax.nn.gelu(u) * jnp.concatenate(chunks, axis=0)

    mix = _dot(y_c.astype(BF16), wout_ref[0:width, :]) + _dot(y_d.astype(BF16), wout_ref[width:2 * width, :])
    o_ref[0] = x + _rms(mix, gpost_ref[...])


def _odd_mixer(x, g_pre, w_in, pool_w, pool_scale, ln_g, ln_b, w_s, bs_full, w_out, g_post):
    B, L, D = x.shape
    tm = min(ROW_TILE, L)
    width = pool_scale.shape[1]
    tok = lambda b, r: (b, r, 0)
    consts = (g_pre, w_in, pool_w, pool_scale, ln_g, ln_b, w_s, bs_full, w_out, g_post)
    return pl.pallas_call(
        _odd_mixer_kernel,
        grid=(B, L // tm),
        in_specs=[pl.BlockSpec((1, tm, D), tok)] + [_const_spec(c.shape) for c in consts],
        out_specs=pl.BlockSpec((1, tm, D), tok),
        out_shape=jax.ShapeDtypeStruct((B, L, D), F32),
        scratch_shapes=[pltpu.VMEM((tm + max(POOL_WINDOWS), width), F32)],
        compiler_params=_params("parallel", "arbitrary"),
        name="odd_mixer",
    )(x, *consts)


def _even_layer(x, g_pre, g_post, w_in, lam_re, lam_im, log_dt, b_re, b_im, c_re, c_im, d, w_glu,
                b_f, w_out):
    B, L, D = x.shape
    n_heads = b_f.shape[0]
    s5w = d.shape[0]
    fw = n_heads * HEAD_DIM
    qs = HEAD_DIM ** -0.5 * LOG2E
    w = jnp.concatenate([
        w_in[:, :s5w], w_in[:, s5w:s5w + fw] * qs, w_in[:, s5w + fw:s5w + 3 * fw],
        jnp.pad(w_in[:, s5w + 3 * fw:], ((0, 0), (0, LANES - n_heads)))], axis=1).astype(BF16)
    u, q, k, v, fl = _even_inproj(x, g_pre[None], w)

    fq, fk = _fox_prep(fl, b_f)
    y_b = _fox_attn(q, k, v, fq, fk)

    ab_re, ab_im, bb_re, bb_im = _s5_discretize(lam_re, lam_im, log_dt, b_re, b_im)
    bst, cst, a_re, a_im, d_t = _s5_matrices(ab_re, ab_im, bb_re, bb_im, c_re, c_im, d, B)
    ga = _s5_scan(u.reshape(L, 2 * B, s5w // 2), bst, cst, a_re, a_im, d_t).reshape(L, B * s5w)

    return _even_post(x, ga, y_b, w_glu.astype(BF16), w_out.astype(BF16), g_post[None])


def _odd_layer(x, g_pre, g_post, w_in, pool_w, pool_scale, ln_g, ln_b, w_s, b_s, w_out):
    bs_full = jnp.repeat(jnp.transpose(b_s), LANES, axis=1)
    return _odd_mixer(x, g_pre[None], w_in.astype(BF16), pool_w.astype(BF16), pool_scale[None],
                      ln_g[None], ln_b[None], w_s, bs_full, w_out.astype(BF16), g_post[None])


def kernel(x, mix_pre_g, mix_post_g, mlp_pre_g, mlp_post_g, w_in_even, s5_lam_re, s5_lam_im, s5_log_dt, s5_b_re, s5_b_im, s5_c_re, s5_c_im, s5_d, s5_w_glu, fox_b_f, w_out_even, w_in_odd, pool_w, pool_scale, sgu_ln_g, sgu_ln_b, sgu_w_s, sgu_b_s, w_out_odd, mlp_w1, mlp_w2):
    B, L, D = x.shape
    depth = mix_pre_g.shape[0]
    for l in range(depth):
        i = l // 2
        if l % 2 == 0:
            x = _even_layer(x, mix_pre_g[l], mix_post_g[l], w_in_even[i], s5_lam_re[i], s5_lam_im[i],
                            s5_log_dt[i], s5_b_re[i], s5_b_im[i], s5_c_re[i], s5_c_im[i], s5_d[i],
                            s5_w_glu[i], fox_b_f[i], w_out_even[i])
        else:
            x = _odd_layer(x, mix_pre_g[l], mix_post_g[l], w_in_odd[i], pool_w[i], pool_scale[i],
                           sgu_ln_g[i], sgu_ln_b[i], sgu_w_s[i], sgu_b_s[i], w_out_odd[i])
        x = _mlp(x.reshape(B * L, D), mlp_pre_g[l][None], mlp_post_g[l][None],
                 mlp_w1[l].astype(BF16), mlp_w2[l].astype(BF16)).reshape(B, L, D)
    return x
```

```python
import functools

import numpy as np
import jax
import jax.numpy as jnp
from jax import lax
from jax.experimental import pallas as pl
from jax.experimental.pallas import tpu as pltpu

F32 = jnp.float32
BF16 = jnp.bfloat16

EPS = 1e-6
LOG2E = 1.4426950408889634
NEG_BIG = -1e30

LANES = 128
SUBLANES = 8
VMEM_LIMIT = 56 * 1024 * 1024

S5_GROUP = 16
S5_STATE = 64
HEAD_DIM = 64
POOL_WINDOWS = (2, 4, 8, 16)
CHUNK = 128

ROW_TILE = 512
S5_TILE = 128
ATT_TQ = 1024
ATT_TK = 512
ATT_PAIRS = 2


def _params(*sem):
    return pltpu.CompilerParams(dimension_semantics=sem, vmem_limit_bytes=VMEM_LIMIT)


def _const_spec(shape):
    nd = len(shape)
    return pl.BlockSpec(shape, lambda *_: (0,) * nd, pipeline_mode=pl.Buffered(1))


def _rms(x, g):
    return x * lax.rsqrt(jnp.mean(x * x, axis=-1, keepdims=True) + EPS) * g


def _split3(x):
    hi = x.astype(BF16)
    r1 = x - hi.astype(F32)
    mid = r1.astype(BF16)
    lo = (r1 - mid.astype(F32)).astype(BF16)
    return hi, mid, lo


def _dot(a, b):
    return jnp.dot(a, b, preferred_element_type=F32)


def _even_inproj_kernel(x_ref, g_ref, w_ref, u_ref, q_ref, k_ref, v_ref, f_ref):
    h = _rms(x_ref[0], g_ref[...]).astype(BF16)
    u_ref[...] = _dot(h, w_ref[:, 0:512])
    q_ref[0] = _dot(h, w_ref[:, 512:1024]).astype(BF16)
    k_ref[0] = _dot(h, w_ref[:, 1024:1536]).astype(BF16)
    v_ref[0] = _dot(h, w_ref[:, 1536:2048]).astype(BF16)
    f_ref[0] = _dot(h, w_ref[:, 2048:2176])


def _even_inproj(x, g, w):
    B, L, D = x.shape
    tm = min(ROW_TILE, L)
    tok = lambda b, r: (b, r, 0)
    return pl.pallas_call(
        _even_inproj_kernel,
        grid=(B, L // tm),
        in_specs=[pl.BlockSpec((1, tm, D), tok), _const_spec((1, D)), _const_spec(w.shape)],
        out_specs=[
            pl.BlockSpec((tm, 512), lambda b, r: (r, b)),
            pl.BlockSpec((1, tm, 512), tok),
            pl.BlockSpec((1, tm, 512), tok),
            pl.BlockSpec((1, tm, 512), tok),
            pl.BlockSpec((1, tm, LANES), tok),
        ],
        out_shape=[
            jax.ShapeDtypeStruct((L, B * 512), F32),
            jax.ShapeDtypeStruct((B, L, 512), BF16),
            jax.ShapeDtypeStruct((B, L, 512), BF16),
            jax.ShapeDtypeStruct((B, L, 512), BF16),
            jax.ShapeDtypeStruct((B, L, LANES), F32),
        ],
        compiler_params=_params("parallel", "parallel"),
        name="even_inproj",
    )(x, g, w)


def _fox_prep_kernel(f_ref, bf_ref, tri_ref, pq_ref, pk_ref, cq_ref, ck_ref,
                     fq_ref, fk_ref, carry_ref):
    @pl.when(pl.program_id(1) == 0)
    def _():
        carry_ref[...] = jnp.zeros_like(carry_ref)

    z = f_ref[0] + bf_ref[...]
    logf = jnp.minimum(z, 0.0) - jnp.log1p(jnp.exp(-jnp.abs(z)))
    tri = tri_ref[...]
    hi, mid, lo = _split3(logf)
    csum = _dot(tri, hi) + _dot(tri, mid) + _dot(tri, lo) + carry_ref[...]
    tm = csum.shape[0]
    carry_ref[...] = csum[tm - 1:tm, :]
    hi, mid, lo = _split3(csum * LOG2E)
    fq = _dot(hi, pq_ref[0]) + _dot(mid, pq_ref[1]) + _dot(lo, pq_ref[2]) + cq_ref[...]
    fk = _dot(hi, pk_ref[0]) + _dot(mid, pk_ref[1]) + _dot(lo, pk_ref[2]) + ck_ref[...]
    fq_ref[0] = fq.astype(BF16)
    fk_ref[0] = fk.astype(BF16)


def _fox_placement(n_heads):
    width = (n_heads // 2) * LANES
    pq = np.zeros((3, LANES, width), np.float32)
    pk = np.zeros((3, LANES, width), np.float32)
    cq = np.zeros((1, width), np.float32)
    ck = np.zeros((1, width), np.float32)
    for h in range(n_heads):
        base = (h // 2) * LANES + (HEAD_DIM if h % 2 == 0 else 0)
        for j in range(3):
            pq[j, h, base + j] = 1.0
            ck[0, base + j] = 1.0
            cq[0, base + 3 + j] = 1.0
            pk[j, h, base + 3 + j] = -1.0
    return pq, pk, cq, ck


def _fox_prep(fl, b_f):
    B, L, _ = fl.shape
    n_heads = b_f.shape[0]
    width = (n_heads // 2) * LANES
    tm = min(ROW_TILE, L)
    pq, pk, cq, ck = _fox_placement(n_heads)
    tri = jnp.asarray(np.tril(np.ones((tm, tm), np.float32)), BF16)
    bf = jnp.zeros((1, LANES), F32).at[0, :n_heads].set(b_f)
    tok = lambda b, r: (b, r, 0)
    return pl.pallas_call(
        _fox_prep_kernel,
        grid=(B, L // tm),
        in_specs=[pl.BlockSpec((1, tm, LANES), tok), _const_spec((1, LANES)),
                  _const_spec((tm, tm)), _const_spec(pq.shape), _const_spec(pk.shape),
                  _const_spec(cq.shape), _const_spec(ck.shape)],
        out_specs=[pl.BlockSpec((1, tm, width), tok), pl.BlockSpec((1, tm, width), tok)],
        out_shape=[jax.ShapeDtypeStruct((B, L, width), BF16)] * 2,
        scratch_shapes=[pltpu.VMEM((1, LANES), F32)],
        compiler_params=_params("parallel", "arbitrary"),
        name="fox_prep",
    )(fl, bf, tri, jnp.asarray(pq, BF16), jnp.asarray(pk, BF16), jnp.asarray(cq), jnp.asarray(ck))


def _fox_attn_kernel(q_ref, k_ref, v_ref, fq_ref, fk_ref, o_ref,
                     kaug_ref, vaug_ref, qaug_ref, m_ref, acc_ref):
    qi = pl.program_id(2)
    tq = q_ref.shape[1]
    L = k_ref.shape[1]
    tk = min(ATT_TK, L)
    n_heads = kaug_ref.shape[0]
    assert tq % tk == 0

    def merge(lanes, first, x_ref, f):
        x = x_ref[0, :, first * LANES:(first + 1) * LANES].astype(F32)
        return (jnp.where(lanes < HEAD_DIM, x, f(0)).astype(BF16),
                jnp.where(lanes >= HEAD_DIM, x, f(1)).astype(BF16))

    @pl.when(qi == 0)
    def _():
        lanes = lax.broadcasted_iota(jnp.int32, (L, LANES), 1)
        for pr in range(n_heads // 2):
            fk = fk_ref[0, :, pr * LANES:(pr + 1) * LANES].astype(F32)
            kaug_ref[2 * pr], kaug_ref[2 * pr + 1] = merge(lanes, pr, k_ref, lambda e: fk)
            ones = ((lanes == HEAD_DIM).astype(F32), (lanes == 0).astype(F32))
            vaug_ref[2 * pr], vaug_ref[2 * pr + 1] = merge(lanes, pr, v_ref, lambda e: ones[e])

    lanes_q = lax.broadcasted_iota(jnp.int32, (tq, LANES), 1)
    for pr in range(n_heads // 2):
        fq = fq_ref[0, :, pr * LANES:(pr + 1) * LANES].astype(F32)
        qaug_ref[2 * pr], qaug_ref[2 * pr + 1] = merge(lanes_q, pr, q_ref, lambda e: fq)

    m_ref[...] = jnp.full(m_ref.shape, NEG_BIG, F32)
    acc_ref[...] = jnp.zeros(acc_ref.shape, F32)

    def tile(j, r0, r1, masked):
        start = pl.multiple_of(j * tk, tk)
        for e in range(n_heads):
            kt = kaug_ref[e, pl.ds(start, tk), :]
            s = lax.dot_general(qaug_ref[e, r0:r1, :], kt, (((1,), (1,)), ((), ())),
                                preferred_element_type=F32)
            if masked:
                row = lax.broadcasted_iota(jnp.int32, s.shape, 0)
                col = lax.broadcasted_iota(jnp.int32, s.shape, 1)
                s = jnp.where(col <= row, s, NEG_BIG)
            m = m_ref[e, r0:r1, :]
            m_new = jnp.maximum(m, jnp.max(s, axis=1, keepdims=True))
            alpha = jnp.exp2(m - m_new)
            p = jnp.exp2(s - jnp.concatenate([m_new] * (tk // LANES), axis=1)).astype(BF16)
            acc_ref[e, r0:r1, :] = alpha * acc_ref[e, r0:r1, :] + _dot(p, vaug_ref[e, pl.ds(start, tk), :])
            m_ref[e, r0:r1, :] = m_new

    n_diag = tq // tk

    def body(j, c):
        tile(j, 0, tq, False)
        return c

    lax.fori_loop(0, qi * n_diag, body, 0)
    for d in range(n_diag):
        tile(qi * n_diag + d, d * tk, (d + 1) * tk, True)
        if (d + 1) * tk < tq:
            tile(qi * n_diag + d, (d + 1) * tk, tq, False)

    for pr in range(n_heads // 2):
        acc0, acc1 = acc_ref[2 * pr], acc_ref[2 * pr + 1]
        out0 = acc0 / acc0[:, HEAD_DIM:HEAD_DIM + 1]
        out1 = acc1 / acc1[:, 0:1]
        o_ref[0, :, pr * LANES:(pr + 1) * LANES] = jnp.where(lanes_q < HEAD_DIM, out0, out1).astype(BF16)


def _fox_attn(q, k, v, fq, fk):
    B, L, W = q.shape
    bw = ATT_PAIRS * LANES
    n_heads = 2 * ATT_PAIRS
    tq = min(ATT_TQ, L)
    qspec = pl.BlockSpec((1, tq, bw), lambda b, p, i: (b, i, p))
    kspec = pl.BlockSpec((1, L, bw), lambda b, p, i: (b, 0, p))
    return pl.pallas_call(
        _fox_attn_kernel,
        grid=(B, W // bw, L // tq),
        in_specs=[qspec, kspec, kspec, qspec, kspec],
        out_specs=qspec,
        out_shape=jax.ShapeDtypeStruct((B, L, W), BF16),
        scratch_shapes=[pltpu.VMEM((n_heads, L, LANES), BF16), pltpu.VMEM((n_heads, L, LANES), BF16),
                        pltpu.VMEM((n_heads, tq, LANES), BF16), pltpu.VMEM((n_heads, tq, LANES), F32),
                        pltpu.VMEM((n_heads, tq, LANES), F32)],
        compiler_params=_params("parallel", "parallel", "arbitrary"),
        name="fox_attn",
    )(q, k, v, fq, fk)


def _s5_kernel(u_ref, bst_ref, cst_ref, are_ref, aim_ref, d_ref, o_ref, x_ref, st_ref):
    ti = u_ref.shape[0]
    ns = are_ref.shape[1]
    hw = u_ref.shape[2]

    @pl.when(pl.program_id(0) == 0)
    def _():
        st_ref[...] = jnp.zeros_like(st_ref)

    u = u_ref[...]
    half = lax.broadcasted_iota(jnp.int32, u.shape, 1) % 2
    zero = jnp.zeros_like(u)
    lhs = jnp.concatenate([jnp.where(half == 0, u, zero), jnp.where(half == 1, u, zero)], axis=-1)
    lhs = lhs.reshape(ti * SUBLANES, 2 * hw).astype(BF16)
    x_ref[...] = _dot(lhs, bst_ref[...]).reshape(ti, SUBLANES, 2 * ns)

    a_re, a_im = are_ref[...], aim_ref[...]

    def step(i, carry):
        s_re, s_im = carry
        n_re = a_re * s_re - a_im * s_im + x_ref[i, :, 0:ns]
        n_im = a_re * s_im + a_im * s_re + x_ref[i, :, ns:2 * ns]
        x_ref[i, :, 0:ns] = n_re
        x_ref[i, :, ns:2 * ns] = n_im
        return n_re, n_im

    s_re, s_im = lax.fori_loop(0, ti, step, (st_ref[:, 0:ns], st_ref[:, ns:2 * ns]), unroll=4)
    st_ref[:, 0:ns] = s_re
    st_ref[:, ns:2 * ns] = s_im

    xs = x_ref[...].reshape(ti * SUBLANES, 2 * ns).astype(BF16)
    y = _dot(xs, cst_ref[...]).reshape(ti, SUBLANES, 2 * hw)
    y = jnp.where(half == 0, y[:, :, 0:hw], y[:, :, hw:2 * hw])
    o_ref[...] = jax.nn.gelu(y + d_ref[...][None] * u)


def _s5_discretize(lam_re, lam_im, log_dt, b_re, b_im):
    dt = jnp.exp(log_dt)[:, None]
    mag = jnp.exp(lam_re * dt)
    ab_re = mag * jnp.cos(lam_im * dt)
    ab_im = mag * jnp.sin(lam_im * dt)
    den = lam_re * lam_re + lam_im * lam_im
    nr = ab_re - 1.0
    ni = ab_im
    q_re = (nr * lam_re + ni * lam_im) / den
    q_im = (ni * lam_re - nr * lam_im) / den
    bb_re = q_re[..., None] * b_re - q_im[..., None] * b_im
    bb_im = q_re[..., None] * b_im + q_im[..., None] * b_re
    return ab_re, ab_im, bb_re, bb_im


def _s5_matrices(ab_re, ab_im, bb_re, bb_im, c_re, c_im, d, batch):
    G, P, H = bb_re.shape
    gh = G // 2
    eye = jnp.eye(gh, dtype=F32)

    def bmat(bb):
        t = jnp.einsum('zgph,gk->zghkp', bb.reshape(2, gh, P, H), eye)
        return t.reshape(2 * gh * H, gh * P)

    def cmat(cc):
        t = jnp.einsum('zghp,gk->kpzgh', cc.reshape(2, gh, H, P), eye)
        return t.reshape(gh * P, 2 * gh * H)

    bst = jnp.concatenate([bmat(bb_re), bmat(bb_im)], axis=1).astype(BF16)
    cst = jnp.concatenate([cmat(c_re), -cmat(c_im)], axis=0).astype(BF16)
    a_re = jnp.tile(ab_re.reshape(2, gh * P), (batch, 1))
    a_im = jnp.tile(ab_im.reshape(2, gh * P), (batch, 1))
    d_t = jnp.tile(d.reshape(2, gh * H), (batch, 1))
    return bst, cst, a_re, a_im, d_t


def _s5_scan(u3, bst, cst, a_re, a_im, d_t):
    L, S, hw = u3.shape
    ns = a_re.shape[1]
    ti = min(S5_TILE, L)
    blk = pl.BlockSpec((ti, S, hw), lambda t: (t, 0, 0))
    return pl.pallas_call(
        _s5_kernel,
        grid=(L // ti,),
        in_specs=[blk, _const_spec(bst.shape), _const_spec(cst.shape),
                  _const_spec(a_re.shape), _const_spec(a_im.shape), _const_spec(d_t.shape)],
        out_specs=blk,
        out_shape=jax.ShapeDtypeStruct((L, S, hw), F32),
        scratch_shapes=[pltpu.VMEM((ti, S, 2 * ns), F32), pltpu.VMEM((S, 2 * ns), F32)],
        compiler_params=_params("arbitrary"),
        name="s5_scan",
    )(u3, bst, cst, a_re, a_im, d_t)


def _even_post_kernel(x_ref, ga_ref, yb_ref, wglu_ref, wout_ref, g_ref, o_ref):
    ga = ga_ref[...]
    ya = ga * jax.nn.sigmoid(_dot(ga.astype(BF16), wglu_ref[...]))
    half = wout_ref.shape[0] // 2
    mix = _dot(ya.astype(BF16), wout_ref[0:half, :]) + _dot(yb_ref[0], wout_ref[half:2 * half, :])
    o_ref[0] = x_ref[0] + _rms(mix, g_ref[...])


def _even_post(x, ga, yb, w_glu, w_out, g):
    B, L, D = x.shape
    tm = min(ROW_TILE, L)
    tok = lambda b, r: (b, r, 0)
    return pl.pallas_call(
        _even_post_kernel,
        grid=(B, L // tm),
        in_specs=[pl.BlockSpec((1, tm, D), tok),
                  pl.BlockSpec((tm, 512), lambda b, r: (r, b)),
                  pl.BlockSpec((1, tm, 512), tok),
                  _const_spec(w_glu.shape), _const_spec(w_out.shape), _const_spec((1, D))],
        out_specs=pl.BlockSpec((1, tm, D), tok),
        out_shape=jax.ShapeDtypeStruct((B, L, D), F32),
        compiler_params=_params("parallel", "parallel"),
        name="even_post",
    )(x, ga, yb, w_glu, w_out, g)


def _mlp_kernel(x_ref, gpre_ref, gpost_ref, w1_ref, w2_ref, o_ref):
    x = x_ref[...]
    h = _rms(x, gpre_ref[...]).astype(BF16)
    d_ff = w1_ref.shape[1]
    fc = 1024
    acc = jnp.zeros(x.shape, F32)
    for f in range(0, d_ff, fc):
        a = jnp.maximum(_dot(h, w1_ref[:, f:f + fc]), 0.0)
        acc = acc + _dot((a * a).astype(BF16), w2_ref[f:f + fc, :])
    o_ref[...] = x + _rms(acc, gpost_ref[...])


def _mlp(x2, g_pre, g_post, w1, w2):
    N, D = x2.shape
    tm = min(ROW_TILE, N)
    row = pl.BlockSpec((tm, D), lambda r: (r, 0))
    return pl.pallas_call(
        _mlp_kernel,
        grid=(N // tm,),
        in_specs=[row, _const_spec((1, D)), _const_spec((1, D)),
                  _const_spec(w1.shape), _const_spec(w2.shape)],
        out_specs=row,
        out_shape=jax.ShapeDtypeStruct((N, D), F32),
        compiler_params=_params("parallel"),
        name="mlp",
    )(x2, g_pre, g_post, w1, w2)


def _odd_mixer_kernel(x_ref, gpre_ref, win_ref, pw_ref, ps_ref, lng_ref, lnb_ref, ws_ref,
                      bs_ref, wout_ref, gpost_ref, o_ref, pool_ref):
    r = pl.program_id(1)
    x = x_ref[0]
    tm = x.shape[0]
    pad = max(POOL_WINDOWS)
    gd = LANES
    n_grp = len(POOL_WINDOWS)
    width = n_grp * gd

    h = _rms(x, gpre_ref[...]).astype(BF16)
    xc = _dot(h, win_ref[:, 0:width])
    u = _dot(h, win_ref[:, width:2 * width])
    v = _dot(h, win_ref[:, 2 * width:3 * width])

    @pl.when(r == 0)
    def _():
        pool_ref[0:pad, :] = jnp.zeros((pad, width), F32)

    pool_ref[pad:pad + tm, :] = xc
    t = r * tm + lax.broadcasted_iota(jnp.int32, (tm, 1), 0)
    yc = []
    for g, w in enumerate(POOL_WINDOWS):
        cols = slice(g * gd, (g + 1) * gd)
        tot = xc[:, cols]
        for lag in range(1, w):
            tot = tot + pool_ref[pl.ds(pad - lag, tm), cols]
        cnt = jnp.minimum(t + 1, w).astype(F32)
        pooled = tot / cnt - xc[:, cols]
        yc.append(_dot(pooled.astype(BF16), pw_ref[g]))
    pool_ref[0:pad, :] = pool_ref[tm:tm + pad, :]
    y_c = jnp.concatenate(yc, axis=-1) * ps_ref[...]

    gv = jax.nn.gelu(v)
    mu = jnp.mean(gv, axis=-1, keepdims=True)
    vc = gv - mu
    vn = vc * lax.rsqrt(jnp.mean(vc * vc, axis=-1, keepdims=True) + EPS)
    vn = (vn * lng_ref[...] + lnb_ref[...]).astype(BF16)
    rows = lax.broadcasted_iota(jnp.int32, (CHUNK, CHUNK), 0)
    cols_i = lax.broadcasted_iota(jnp.int32, (CHUNK, CHUNK), 1)
    ws = [jnp.where(cols_i <= rows, ws_ref[g], 0.0).astype(BF16) for g in range(n_grp)]
    chunks = []
    for n in range(tm // CHUNK):
        parts = [_dot(ws[g], vn[n * CHUNK:(n + 1) * CHUNK, g * gd:(g + 1) * gd]) for g in range(n_grp)]
        chunks.append(jnp.concatenate(parts, axis=-1) + bs_ref[...])
    y_d = jax.nn.gelu(u) * jnp.concatenate(chunks, axis=0)

    mix = _dot(y_c.astype(BF16), wout_ref[0:width, :]) + _dot(y_d.astype(BF16), wout_ref[width:2 * width, :])
    o_ref[0] = x + _rms(mix, gpost_ref[...])


def _odd_mixer(x, g_pre, w_in, pool_w, pool_scale, ln_g, ln_b, w_s, bs_full, w_out, g_post):
    B, L, D = x.shape
    tm = min(ROW_TILE, L)
    width = pool_scale.shape[1]
    tok = lambda b, r: (b, r, 0)
    consts = (g_pre, w_in, pool_w, pool_scale, ln_g, ln_b, w_s, bs_full, w_out, g_post)
    return pl.pallas_call(
        _odd_mixer_kernel,
        grid=(B, L // tm),
        in_specs=[pl.BlockSpec((1, tm, D), tok)] + [_const_spec(c.shape) for c in consts],
        out_specs=pl.BlockSpec((1, tm, D), tok),
        out_shape=jax.ShapeDtypeStruct((B, L, D), F32),
        scratch_shapes=[pltpu.VMEM((tm + max(POOL_WINDOWS), width), F32)],
        compiler_params=_params("parallel", "arbitrary"),
        name="odd_mixer",
    )(x, *consts)


def _even_layer(x, g_pre, g_post, w_in, lam_re, lam_im, log_dt, b_re, b_im, c_re, c_im, d, w_glu,
                b_f, w_out):
    B, L, D = x.shape
    n_heads = b_f.shape[0]
    s5w = d.shape[0]
    fw = n_heads * HEAD_DIM
    qs = HEAD_DIM ** -0.5 * LOG2E
    w = jnp.concatenate([
        w_in[:, :s5w], w_in[:, s5w:s5w + fw] * qs, w_in[:, s5w + fw:s5w + 3 * fw],
        jnp.pad(w_in[:, s5w + 3 * fw:], ((0, 0), (0, LANES - n_heads)))], axis=1).astype(BF16)
    u, q, k, v, fl = _even_inproj(x, g_pre[None], w)

    fq, fk = _fox_prep(fl, b_f)
    y_b = _fox_attn(q, k, v, fq, fk)

    ab_re, ab_im, bb_re, bb_im = _s5_discretize(lam_re, lam_im, log_dt, b_re, b_im)
    bst, cst, a_re, a_im, d_t = _s5_matrices(ab_re, ab_im, bb_re, bb_im, c_re, c_im, d, B)
    ga = _s5_scan(u.reshape(L, 2 * B, s5w // 2), bst, cst, a_re, a_im, d_t).reshape(L, B * s5w)

    return _even_post(x, ga, y_b, w_glu.astype(BF16), w_out.astype(BF16), g_post[None])


def _odd_layer(x, g_pre, g_post, w_in, pool_w, pool_scale, ln_g, ln_b, w_s, b_s, w_out):
    bs_full = jnp.repeat(jnp.transpose(b_s), LANES, axis=1)
    return _odd_mixer(x, g_pre[None], w_in.astype(BF16), pool_w.astype(BF16), pool_scale[None],
                      ln_g[None], ln_b[None], w_s, bs_full, w_out.astype(BF16), g_post[None])


def kernel(x, mix_pre_g, mix_post_g, mlp_pre_g, mlp_post_g, w_in_even, s5_lam_re, s5_lam_im, s5_log_dt, s5_b_re, s5_b_im, s5_c_re, s5_c_im, s5_d, s5_w_glu, fox_b_f, w_out_even, w_in_odd, pool_w, pool_scale, sgu_ln_g, sgu_ln_b, sgu_w_s, sgu_b_s, w_out_odd, mlp_w1, mlp_w2):
    B, L, D = x.shape
    depth = mix_pre_g.shape[0]
    for l in range(depth):
        i = l // 2
        if l % 2 == 0:
            x = _even_layer(x, mix_pre_g[l], mix_post_g[l], w_in_even[i], s5_lam_re[i], s5_lam_im[i],
                            s5_log_dt[i], s5_b_re[i], s5_b_im[i], s5_c_re[i], s5_c_im[i], s5_d[i],
                            s5_w_glu[i], fox_b_f[i], w_out_even[i])
        else:
            x = _odd_layer(x, mix_pre_g[l], mix_post_g[l], w_in_odd[i], pool_w[i], pool_scale[i],
                           sgu_ln_g[i], sgu_ln_b[i], sgu_w_s[i], sgu_b_s[i], w_out_odd[i])
        x = _mlp(x.reshape(B * L, D), mlp_pre_g[l][None], mlp_post_g[l][None],
                 mlp_w1[l].astype(BF16), mlp_w2[l].astype(BF16)).reshape(B, L, D)
    return x
```

```python
import functools

import numpy as np
import jax
import jax.numpy as jnp
from jax import lax
from jax.experimental import pallas as pl
from jax.experimental.pallas import tpu as pltpu

F32 = jnp.float32
BF16 = jnp.bfloat16

EPS = 1e-6
LOG2E = 1.4426950408889634
NEG_BIG = -1e30

LANES = 128
SUBLANES = 8
VMEM_LIMIT = 56 * 1024 * 1024

S5_GROUP = 16
S5_STATE = 64
HEAD_DIM = 64
POOL_WINDOWS = (2, 4, 8, 16)
CHUNK = 128

ROW_TILE = 512
S5_TILE = 128
ATT_TQ = 1024
ATT_TK = 512
ATT_PAIRS = 2


def _params(*sem):
    return pltpu.CompilerParams(dimension_semantics=sem, vmem_limit_bytes=VMEM_LIMIT)


def _const_spec(shape):
    nd = len(shape)
    return pl.BlockSpec(shape, lambda *_: (0,) * nd, pipeline_mode=pl.Buffered(1))


def _rms(x, g):
    return x * lax.rsqrt(jnp.mean(x * x, axis=-1, keepdims=True) + EPS) * g


def _split3(x):
    hi = x.astype(BF16)
    r1 = x - hi.astype(F32)
    mid = r1.astype(BF16)
    lo = (r1 - mid.astype(F32)).astype(BF16)
    return hi, mid, lo


def _dot(a, b):
    return jnp.dot(a, b, preferred_element_type=F32)


def _even_inproj_kernel(x_ref, g_ref, w_ref, u0_ref, u1_ref, q_ref, k_ref, v_ref, f_ref):
    h = _rms(x_ref[0], g_ref[...]).astype(BF16)
    u0_ref[...] = _dot(h, w_ref[:, 0:256])
    u1_ref[...] = _dot(h, w_ref[:, 256:512])
    q_ref[0] = _dot(h, w_ref[:, 512:1024]).astype(BF16)
    k_ref[0] = _dot(h, w_ref[:, 1024:1536]).astype(BF16)
    v_ref[0] = _dot(h, w_ref[:, 1536:2048]).astype(BF16)
    f_ref[0] = _dot(h, w_ref[:, 2048:2176])


def _even_inproj(x, g, w):
    B, L, D = x.shape
    tm = min(ROW_TILE, L)
    tok = lambda b, r: (b, r, 0)
    return pl.pallas_call(
        _even_inproj_kernel,
        grid=(B, L // tm),
        in_specs=[pl.BlockSpec((1, tm, D), tok), _const_spec((1, D)), _const_spec(w.shape)],
        out_specs=[
            pl.BlockSpec((tm, 256), lambda b, r: (r, b)),
            pl.BlockSpec((tm, 256), lambda b, r: (r, b)),
            pl.BlockSpec((1, tm, 512), tok),
            pl.BlockSpec((1, tm, 512), tok),
            pl.BlockSpec((1, tm, 512), tok),
            pl.BlockSpec((1, tm, LANES), tok),
        ],
        out_shape=[
            jax.ShapeDtypeStruct((L, B * 256), F32),
            jax.ShapeDtypeStruct((L, B * 256), F32),
            jax.ShapeDtypeStruct((B, L, 512), BF16),
            jax.ShapeDtypeStruct((B, L, 512), BF16),
            jax.ShapeDtypeStruct((B, L, 512), BF16),
            jax.ShapeDtypeStruct((B, L, LANES), F32),
        ],
        compiler_params=_params("parallel", "parallel"),
        name="even_inproj",
    )(x, g, w)


def _fox_prep_kernel(f_ref, bf_ref, tri_ref, pq_ref, pk_ref, cq_ref, ck_ref,
                     fq_ref, fk_ref, carry_ref):
    @pl.when(pl.program_id(1) == 0)
    def _():
        carry_ref[...] = jnp.zeros_like(carry_ref)

    z = f_ref[0] + bf_ref[...]
    logf = jnp.minimum(z, 0.0) - jnp.log1p(jnp.exp(-jnp.abs(z)))
    tri = tri_ref[...]
    hi, mid, lo = _split3(logf)
    csum = _dot(tri, hi) + _dot(tri, mid) + _dot(tri, lo) + carry_ref[...]
    tm = csum.shape[0]
    carry_ref[...] = csum[tm - 1:tm, :]
    hi, mid, lo = _split3(csum * LOG2E)
    fq = _dot(hi, pq_ref[0]) + _dot(mid, pq_ref[1]) + _dot(lo, pq_ref[2]) + cq_ref[...]
    fk = _dot(hi, pk_ref[0]) + _dot(mid, pk_ref[1]) + _dot(lo, pk_ref[2]) + ck_ref[...]
    fq_ref[0] = fq.astype(BF16)
    fk_ref[0] = fk.astype(BF16)


def _fox_placement(n_heads):
    width = (n_heads // 2) * LANES
    pq = np.zeros((3, LANES, width), np.float32)
    pk = np.zeros((3, LANES, width), np.float32)
    cq = np.zeros((1, width), np.float32)
    ck = np.zeros((1, width), np.float32)
    for h in range(n_heads):
        base = (h // 2) * LANES + (HEAD_DIM if h % 2 == 0 else 0)
        for j in range(3):
            pq[j, h, base + j] = 1.0
            ck[0, base + j] = 1.0
            cq[0, base + 3 + j] = 1.0
            pk[j, h, base + 3 + j] = -1.0
    return pq, pk, cq, ck


def _fox_prep(fl, b_f):
    B, L, _ = fl.shape
    n_heads = b_f.shape[0]
    width = (n_heads // 2) * LANES
    tm = min(ROW_TILE, L)
    pq, pk, cq, ck = _fox_placement(n_heads)
    tri = jnp.asarray(np.tril(np.ones((tm, tm), np.float32)), BF16)
    bf = jnp.zeros((1, LANES), F32).at[0, :n_heads].set(b_f)
    tok = lambda b, r: (b, r, 0)
    return pl.pallas_call(
        _fox_prep_kernel,
        grid=(B, L // tm),
        in_specs=[pl.BlockSpec((1, tm, LANES), tok), _const_spec((1, LANES)),
                  _const_spec((tm, tm)), _const_spec(pq.shape), _const_spec(pk.shape),
                  _const_spec(cq.shape), _const_spec(ck.shape)],
        out_specs=[pl.BlockSpec((1, tm, width), tok), pl.BlockSpec((1, tm, width), tok)],
        out_shape=[jax.ShapeDtypeStruct((B, L, width), BF16)] * 2,
        scratch_shapes=[pltpu.VMEM((1, LANES), F32)],
        compiler_params=_params("parallel", "arbitrary"),
        name="fox_prep",
    )(fl, bf, tri, jnp.asarray(pq, BF16), jnp.asarray(pk, BF16), jnp.asarray(cq), jnp.asarray(ck))


def _fox_attn_kernel(q_ref, k_ref, v_ref, fq_ref, fk_ref, o_ref,
                     kaug_ref, vaug_ref, qaug_ref, m_ref, acc_ref):
    qi = pl.program_id(2)
    tq = q_ref.shape[1]
    L = k_ref.shape[1]
    tk = min(ATT_TK, L)
    n_heads = kaug_ref.shape[0]
    assert tq % tk == 0

    def merge(lanes, first, x_ref, f):
        x = x_ref[0, :, first * LANES:(first + 1) * LANES].astype(F32)
        return (jnp.where(lanes < HEAD_DIM, x, f(0)).astype(BF16),
                jnp.where(lanes >= HEAD_DIM, x, f(1)).astype(BF16))

    @pl.when(qi == 0)
    def _():
        lanes = lax.broadcasted_iota(jnp.int32, (L, LANES), 1)
        for pr in range(n_heads // 2):
            fk = fk_ref[0, :, pr * LANES:(pr + 1) * LANES].astype(F32)
            kaug_ref[2 * pr], kaug_ref[2 * pr + 1] = merge(lanes, pr, k_ref, lambda e: fk)
            ones = ((lanes == HEAD_DIM).astype(F32), (lanes == 0).astype(F32))
            vaug_ref[2 * pr], vaug_ref[2 * pr + 1] = merge(lanes, pr, v_ref, lambda e: ones[e])

    lanes_q = lax.broadcasted_iota(jnp.int32, (tq, LANES), 1)
    for pr in range(n_heads // 2):
        fq = fq_ref[0, :, pr * LANES:(pr + 1) * LANES].astype(F32)
        qaug_ref[2 * pr], qaug_ref[2 * pr + 1] = merge(lanes_q, pr, q_ref, lambda e: fq)

    m_ref[...] = jnp.full(m_ref.shape, NEG_BIG, F32)
    acc_ref[...] = jnp.zeros(acc_ref.shape, F32)

    def tile(j, r0, r1, masked):
        start = pl.multiple_of(j * tk, tk)
        for e in range(n_heads):
            kt = kaug_ref[e, pl.ds(start, tk), :]
            s = lax.dot_general(qaug_ref[e, r0:r1, :], kt, (((1,), (1,)), ((), ())),
                                preferred_element_type=F32)
            if masked:
                row = lax.broadcasted_iota(jnp.int32, s.shape, 0)
                col = lax.broadcasted_iota(jnp.int32, s.shape, 1)
                s = jnp.where(col <= row, s, NEG_BIG)
            m = m_ref[e, r0:r1, :]
            m_new = jnp.maximum(m, jnp.max(s, axis=1, keepdims=True))
            alpha = jnp.exp2(m - m_new)
            p = jnp.exp2(s - jnp.concatenate([m_new] * (tk // LANES), axis=1)).astype(BF16)
            acc_ref[e, r0:r1, :] = alpha * acc_ref[e, r0:r1, :] + _dot(p, vaug_ref[e, pl.ds(start, tk), :])
            m_ref[e, r0:r1, :] = m_new

    n_diag = tq // tk

    def body(j, c):
        tile(j, 0, tq, False)
        return c

    lax.fori_loop(0, qi * n_diag, body, 0)
    for d in range(n_diag):
        tile(qi * n_diag + d, d * tk, (d + 1) * tk, True)
        if (d + 1) * tk < tq:
            tile(qi * n_diag + d, (d + 1) * tk, tq, False)

    for pr in range(n_heads // 2):
        acc0, acc1 = acc_ref[2 * pr], acc_ref[2 * pr + 1]
        out0 = acc0 / acc0[:, HEAD_DIM:HEAD_DIM + 1]
        out1 = acc1 / acc1[:, 0:1]
        o_ref[0, :, pr * LANES:(pr + 1) * LANES] = jnp.where(lanes_q < HEAD_DIM, out0, out1).astype(BF16)


def _fox_attn(q, k, v, fq, fk):
    B, L, W = q.shape
    bw = ATT_PAIRS * LANES
    n_heads = 2 * ATT_PAIRS
    tq = min(ATT_TQ, L)
    qspec = pl.BlockSpec((1, tq, bw), lambda b, p, i: (b, i, p))
    kspec = pl.BlockSpec((1, L, bw), lambda b, p, i: (b, 0, p))
    return pl.pallas_call(
        _fox_attn_kernel,
        grid=(B, W // bw, L // tq),
        in_specs=[qspec, kspec, kspec, qspec, kspec],
        out_specs=qspec,
        out_shape=jax.ShapeDtypeStruct((B, L, W), BF16),
        scratch_shapes=[pltpu.VMEM((n_heads, L, LANES), BF16), pltpu.VMEM((n_heads, L, LANES), BF16),
                        pltpu.VMEM((n_heads, tq, LANES), BF16), pltpu.VMEM((n_heads, tq, LANES), F32),
                        pltpu.VMEM((n_heads, tq, LANES), F32)],
        compiler_params=_params("parallel", "parallel", "arbitrary"),
        name="fox_attn",
    )(q, k, v, fq, fk)


def _s5_kernel(u0_ref, u1_ref, bst_ref, cst_ref, are_ref, aim_ref, d_ref, o0_ref, o1_ref,
               x_ref, st_ref):
    rows = u0_ref.shape[0]
    nb = x_ref.shape[0] // 2
    hrows = rows // 2

    @pl.when(pl.program_id(0) == 0)
    def _():
        st_ref[...] = jnp.zeros_like(st_ref)

    def half_rows(ref, h, *lead):
        return ref[(*lead, pl.ds(h, hrows, stride=2), slice(None))]

    u_h = []
    for h in range(2):
        u = jnp.concatenate([half_rows(u0_ref, h), half_rows(u1_ref, h)], axis=1)
        u_h.append(u)
        bu = _dot(u.astype(BF16), bst_ref[h])
        for c in range(2 * nb):
            x_ref[c, pl.ds(h, hrows, stride=2), :] = bu[:, c * LANES:(c + 1) * LANES]

    a_re = [are_ref[:, c * LANES:(c + 1) * LANES] for c in range(nb)]
    a_im = [aim_ref[:, c * LANES:(c + 1) * LANES] for c in range(nb)]

    def step(i, state):
        r = pl.ds(pl.multiple_of(i * SUBLANES, SUBLANES), SUBLANES)
        new = []
        for c in range(nb):
            s_re, s_im = state[2 * c], state[2 * c + 1]
            n_re = a_re[c] * s_re - a_im[c] * s_im + x_ref[c, r, :]
            n_im = a_re[c] * s_im + a_im[c] * s_re + x_ref[nb + c, r, :]
            x_ref[c, r, :] = n_re
            x_ref[nb + c, r, :] = n_im
            new += [n_re, n_im]
        return tuple(new)

    state = lax.fori_loop(0, rows // SUBLANES, step,
                          tuple(st_ref[c] for c in range(2 * nb)), unroll=4)
    for c in range(2 * nb):
        st_ref[c] = state[c]

    for h in range(2):
        xs = jnp.concatenate([half_rows(x_ref, h, c) for c in range(nb)]
                             + [half_rows(x_ref, h, nb + c) for c in range(nb)], axis=1)
        y = _dot(xs.astype(BF16), cst_ref[h])
        g = jax.nn.gelu(y + d_ref[h:h + 1, :] * u_h[h])
        o0_ref[pl.ds(h, hrows, stride=2), :] = g[:, 0:LANES]
        o1_ref[pl.ds(h, hrows, stride=2), :] = g[:, LANES:2 * LANES]


def _s5_discretize(lam_re, lam_im, log_dt, b_re, b_im):
    dt = jnp.exp(log_dt)[:, None]
    mag = jnp.exp(lam_re * dt)
    ab_re = mag * jnp.cos(lam_im * dt)
    ab_im = mag * jnp.sin(lam_im * dt)
    den = lam_re * lam_re + lam_im * lam_im
    nr = ab_re - 1.0
    ni = ab_im
    q_re = (nr * lam_re + ni * lam_im) / den
    q_im = (ni * lam_re - nr * lam_im) / den
    bb_re = q_re[..., None] * b_re - q_im[..., None] * b_im
    bb_im = q_re[..., None] * b_im + q_im[..., None] * b_re
    return ab_re, ab_im, bb_re, bb_im


def _s5_matrices(ab_re, ab_im, bb_re, bb_im, c_re, c_im, d, batch):
    G, P, H = bb_re.shape
    gh = G // 2
    eye = jnp.eye(gh, dtype=F32)

    def bmat(bb):
        t = jnp.einsum('zgph,gk->zghkp', bb.reshape(2, gh, P, H), eye)
        return t.reshape(2 * gh * H, gh * P)

    def cmat(cc):
        t = jnp.einsum('zghp,gk->kpzgh', cc.reshape(2, gh, H, P), eye)
        return t.reshape(gh * P, 2 * gh * H)

    hw, ns = gh * H, gh * P
    bst = jnp.concatenate([bmat(bb_re), bmat(bb_im)], axis=1).reshape(2, hw, 2 * ns).astype(BF16)
    cst = jnp.concatenate([cmat(c_re), -cmat(c_im)], axis=0).reshape(2 * ns, 2, hw)
    cst = jnp.transpose(cst, (1, 0, 2)).astype(BF16)
    a_re = jnp.tile(ab_re.reshape(2, ns), (batch, 1))
    a_im = jnp.tile(ab_im.reshape(2, ns), (batch, 1))
    return bst, cst, a_re, a_im, d.reshape(2, hw)


def _s5_scan(u0, u1, bst, cst, a_re, a_im, d2):
    n_rows = u0.shape[0]
    ns = a_re.shape[1]
    rows = min(S5_TILE * SUBLANES, n_rows)
    blk = pl.BlockSpec((rows, LANES), lambda t: (t, 0))
    return pl.pallas_call(
        _s5_kernel,
        grid=(n_rows // rows,),
        in_specs=[blk, blk, _const_spec(bst.shape), _const_spec(cst.shape),
                  _const_spec(a_re.shape), _const_spec(a_im.shape), _const_spec(d2.shape)],
        out_specs=[blk, blk],
        out_shape=[jax.ShapeDtypeStruct((n_rows, LANES), F32)] * 2,
        scratch_shapes=[pltpu.VMEM((2 * ns // LANES, rows, LANES), F32),
                        pltpu.VMEM((2 * ns // LANES, SUBLANES, LANES), F32)],
        compiler_params=_params("arbitrary"),
        name="s5_scan",
    )(u0, u1, bst, cst, a_re, a_im, d2)


def _even_post_kernel(x_ref, g0_ref, g1_ref, yb_ref, wglu_ref, wout_ref, g_ref, o_ref):
    g0, g1 = g0_ref[...], g1_ref[...]
    ga = jnp.concatenate([g0[:, :LANES], g1[:, :LANES], g0[:, LANES:], g1[:, LANES:]], axis=1)
    ya = ga * jax.nn.sigmoid(_dot(ga.astype(BF16), wglu_ref[...]))
    half = wout_ref.shape[0] // 2
    mix = _dot(ya.astype(BF16), wout_ref[0:half, :]) + _dot(yb_ref[0], wout_ref[half:2 * half, :])
    o_ref[0] = x_ref[0] + _rms(mix, g_ref[...])


def _even_post(x, g0, g1, yb, w_glu, w_out, g):
    B, L, D = x.shape
    tm = min(ROW_TILE, L)
    tok = lambda b, r: (b, r, 0)
    return pl.pallas_call(
        _even_post_kernel,
        grid=(B, L // tm),
        in_specs=[pl.BlockSpec((1, tm, D), tok),
                  pl.BlockSpec((tm, 256), lambda b, r: (r, b)),
                  pl.BlockSpec((tm, 256), lambda b, r: (r, b)),
                  pl.BlockSpec((1, tm, 512), tok),
                  _const_spec(w_glu.shape), _const_spec(w_out.shape), _const_spec((1, D))],
        out_specs=pl.BlockSpec((1, tm, D), tok),
        out_shape=jax.ShapeDtypeStruct((B, L, D), F32),
        compiler_params=_params("parallel", "parallel"),
        name="even_post",
    )(x, g0, g1, yb, w_glu, w_out, g)


def _mlp_kernel(x_ref, gpre_ref, gpost_ref, w1_ref, w2_ref, o_ref):
    x = x_ref[...]
    h = _rms(x, gpre_ref[...]).astype(BF16)
    d_ff = w1_ref.shape[1]
    fc = 1024
    acc = jnp.zeros(x.shape, F32)
    for f in range(0, d_ff, fc):
        a = jnp.maximum(_dot(h, w1_ref[:, f:f + fc]), 0.0)
        acc = acc + _dot((a * a).astype(BF16), w2_ref[f:f + fc, :])
    o_ref[...] = x + _rms(acc, gpost_ref[...])


def _mlp(x2, g_pre, g_post, w1, w2):
    N, D = x2.shape
    tm = min(ROW_TILE, N)
    row = pl.BlockSpec((tm, D), lambda r: (r, 0))
    return pl.pallas_call(
        _mlp_kernel,
        grid=(N // tm,),
        in_specs=[row, _const_spec((1, D)), _const_spec((1, D)),
                  _const_spec(w1.shape), _const_spec(w2.shape)],
        out_specs=row,
        out_shape=jax.ShapeDtypeStruct((N, D), F32),
        compiler_params=_params("parallel"),
        name="mlp",
    )(x2, g_pre, g_post, w1, w2)


def _odd_mixer_kernel(x_ref, gpre_ref, win_ref, pw_ref, ps_ref, lng_ref, lnb_ref, ws_ref,
                      bs_ref, wout_ref, gpost_ref, o_ref, pool_ref):
    r = pl.program_id(1)
    x = x_ref[0]
    tm = x.shape[0]
    pad = max(POOL_WINDOWS)
    gd = LANES
    n_grp = len(POOL_WINDOWS)
    width = n_grp * gd

    h = _rms(x, gpre_ref[...]).astype(BF16)
    xc = _dot(h, win_ref[:, 0:width])
    u = _dot(h, win_ref[:, width:2 * width])
    v = _dot(h, win_ref[:, 2 * width:3 * width])

    @pl.when(r == 0)
    def _():
        pool_ref[0:pad, :] = jnp.zeros((pad, width), F32)

    pool_ref[pad:pad + tm, :] = xc
    t = r * tm + lax.broadcasted_iota(jnp.int32, (tm, 1), 0)
    yc = []
    for g, w in enumerate(POOL_WINDOWS):
        cols = slice(g * gd, (g + 1) * gd)
        tot = xc[:, cols]
        for lag in range(1, w):
            tot = tot + pool_ref[pl.ds(pad - lag, tm), cols]
        cnt = jnp.minimum(t + 1, w).astype(F32)
        pooled = tot / cnt - xc[:, cols]
        yc.append(_dot(pooled.astype(BF16), pw_ref[g]))
    pool_ref[0:pad, :] = pool_ref[tm:tm + pad, :]
    y_c = jnp.concatenate(yc, axis=-1) * ps_ref[...]

    gv = jax.nn.gelu(v)
    mu = jnp.mean(gv, axis=-1, keepdims=True)
    vc = gv - mu
    vn = vc * lax.rsqrt(jnp.mean(vc * vc, axis=-1, keepdims=True) + EPS)
    vn = (vn * lng_ref[...] + lnb_ref[...]).astype(BF16)
    rows = lax.broadcasted_iota(jnp.int32, (CHUNK, CHUNK), 0)
    cols_i = lax.broadcasted_iota(jnp.int32, (CHUNK, CHUNK), 1)
    ws = [jnp.where(cols_i <= rows, ws_ref[g], 0.0).astype(BF16) for g in range(n_grp)]
    chunks = []
    for n in range(tm // CHUNK):
        parts = [_dot(ws[g], vn[n * CHUNK:(n + 1) * CHUNK, g * gd:(g + 1) * gd]) for g in range(n_grp)]
        chunks.append(jnp.concatenate(parts, axis=-1) + bs_ref[...])
    y_d = jax.nn.gelu(u) * jnp.concatenate(chunks, axis=0)

    mix = _dot(y_c.astype(BF16), wout_ref[0:width, :]) + _dot(y_d.astype(BF16), wout_ref[width:2 * width, :])
    o_ref[0] = x + _rms(mix, gpost_ref[...])


def _odd_mixer(x, g_pre, w_in, pool_w, pool_scale, ln_g, ln_b, w_s, bs_full, w_out, g_post):
    B, L, D = x.shape
    tm = min(ROW_TILE, L)
    width = pool_scale.shape[1]
    tok = lambda b, r: (b, r, 0)
    consts = (g_pre, w_in, pool_w, pool_scale, ln_g, ln_b, w_s, bs_full, w_out, g_post)
    return pl.pallas_call(
        _odd_mixer_kernel,
        grid=(B, L // tm),
        in_specs=[pl.BlockSpec((1, tm, D), tok)] + [_const_spec(c.shape) for c in consts],
        out_specs=pl.BlockSpec((1, tm, D), tok),
        out_shape=jax.ShapeDtypeStruct((B, L, D), F32),
        scratch_shapes=[pltpu.VMEM((tm + max(POOL_WINDOWS), width), F32)],
        compiler_params=_params("parallel", "arbitrary"),
        name="odd_mixer",
    )(x, *consts)


def _even_layer(x, g_pre, g_post, w_in, lam_re, lam_im, log_dt, b_re, b_im, c_re, c_im, d, w_glu,
                b_f, w_out):
    B, L, D = x.shape
    n_heads = b_f.shape[0]
    s5w = d.shape[0]
    fw = n_heads * HEAD_DIM
    qs = HEAD_DIM ** -0.5 * LOG2E
    w_u = w_in[:, :s5w].reshape(D, 2, 2, LANES).transpose(0, 2, 1, 3).reshape(D, s5w)
    w = jnp.concatenate([
        w_u, w_in[:, s5w:s5w + fw] * qs, w_in[:, s5w + fw:s5w + 3 * fw],
        jnp.pad(w_in[:, s5w + 3 * fw:], ((0, 0), (0, LANES - n_heads)))], axis=1).astype(BF16)
    u0, u1, q, k, v, fl = _even_inproj(x, g_pre[None], w)

    fq, fk = _fox_prep(fl, b_f)
    y_b = _fox_attn(q, k, v, fq, fk)

    ab_re, ab_im, bb_re, bb_im = _s5_discretize(lam_re, lam_im, log_dt, b_re, b_im)
    bst, cst, a_re, a_im, d2 = _s5_matrices(ab_re, ab_im, bb_re, bb_im, c_re, c_im, d, B)
    rows = L * 2 * B
    g0, g1 = _s5_scan(u0.reshape(rows, LANES), u1.reshape(rows, LANES), bst, cst, a_re, a_im, d2)

    return _even_post(x, g0.reshape(L, B * 2 * LANES), g1.reshape(L, B * 2 * LANES), y_b,
                      w_glu.astype(BF16), w_out.astype(BF16), g_post[None])


def _odd_layer(x, g_pre, g_post, w_in, pool_w, pool_scale, ln_g, ln_b, w_s, b_s, w_out):
    bs_full = jnp.repeat(jnp.transpose(b_s), LANES, axis=1)
    return _odd_mixer(x, g_pre[None], w_in.astype(BF16), pool_w.astype(BF16), pool_scale[None],
                      ln_g[None], ln_b[None], w_s, bs_full, w_out.astype(BF16), g_post[None])


def kernel(x, mix_pre_g, mix_post_g, mlp_pre_g, mlp_post_g, w_in_even, s5_lam_re, s5_lam_im, s5_log_dt, s5_b_re, s5_b_im, s5_c_re, s5_c_im, s5_d, s5_w_glu, fox_b_f, w_out_even, w_in_odd, pool_w, pool_scale, sgu_ln_g, sgu_ln_b, sgu_w_s, sgu_b_s, w_out_odd, mlp_w1, mlp_w2):
    B, L, D = x.shape
    depth = mix_pre_g.shape[0]
    for l in range(depth):
        i = l // 2
        if l % 2 == 0:
            x = _even_layer(x, mix_pre_g[l], mix_post_g[l], w_in_even[i], s5_lam_re[i], s5_lam_im[i],
                            s5_log_dt[i], s5_b_re[i], s5_b_im[i], s5_c_re[i], s5_c_im[i], s5_d[i],
                            s5_w_glu[i], fox_b_f[i], w_out_even[i])
        else:
            x = _odd_layer(x, mix_pre_g[l], mix_post_g[l], w_in_odd[i], pool_w[i], pool_scale[i],
                           sgu_ln_g[i], sgu_ln_b[i], sgu_w_s[i], sgu_b_s[i], w_out_odd[i])
        x = _mlp(x.reshape(B * L, D), mlp_pre_g[l][None], mlp_post_g[l][None],
                 mlp_w1[l].astype(BF16), mlp_w2[l].astype(BF16)).reshape(B, L, D)
    return x
```

```python
import functools

import numpy as np
import jax
import jax.numpy as jnp
from jax import lax
from jax.experimental import pallas as pl
from jax.experimental.pallas import tpu as pltpu

F32 = jnp.float32
BF16 = jnp.bfloat16

EPS = 1e-6
LOG2E = 1.4426950408889634
NEG_BIG = -1e30

LANES = 128
SUBLANES = 8
VMEM_LIMIT = 56 * 1024 * 1024

S5_GROUP = 16
S5_STATE = 64
HEAD_DIM = 64
POOL_WINDOWS = (2, 4, 8, 16)
CHUNK = 128

ROW_TILE = 512
POST_TILE = 256
MLP_FF_CHUNK = 1024
S5_TILE = 128
ATT_TQ = 1024
ATT_TK = 512
ATT_PAIRS = 2


def _params(*sem):
    return pltpu.CompilerParams(dimension_semantics=sem, vmem_limit_bytes=VMEM_LIMIT)


def _const_spec(shape):
    nd = len(shape)
    return pl.BlockSpec(shape, lambda *_: (0,) * nd, pipeline_mode=pl.Buffered(1))


def _rms(x, g):
    return x * lax.rsqrt(jnp.mean(x * x, axis=-1, keepdims=True) + EPS) * g


def _split3(x):
    hi = x.astype(BF16)
    r1 = x - hi.astype(F32)
    mid = r1.astype(BF16)
    lo = (r1 - mid.astype(F32)).astype(BF16)
    return hi, mid, lo


def _dot(a, b):
    return jnp.dot(a, b, preferred_element_type=F32)


def _even_inproj_kernel(x_ref, g_ref, w_ref, u0_ref, u1_ref, q_ref, k_ref, v_ref, f_ref):
    nb, tm, _ = x_ref.shape
    for b in range(nb):
        h = _rms(x_ref[b], g_ref[...]).astype(BF16)
        zu = _dot(h, w_ref[:, 0:512])
        for half in range(2):
            rows = pl.ds(2 * b + half, tm, stride=2 * nb)
            u0_ref[rows, :] = zu[:, half * LANES:(half + 1) * LANES]
            u1_ref[rows, :] = zu[:, (2 + half) * LANES:(3 + half) * LANES]
        q_ref[b] = _dot(h, w_ref[:, 512:1024]).astype(BF16)
        k_ref[b] = _dot(h, w_ref[:, 1024:1536]).astype(BF16)
        v_ref[b] = _dot(h, w_ref[:, 1536:2048]).astype(BF16)
        f_ref[b] = _dot(h, w_ref[:, 2048:2176])


def _even_inproj(x, g, w):
    B, L, D = x.shape
    tm = min(ROW_TILE, L)
    tok = lambda r: (0, r, 0)
    ublk = pl.BlockSpec((tm * 2 * B, LANES), lambda r: (r, 0))
    return pl.pallas_call(
        _even_inproj_kernel,
        grid=(L // tm,),
        in_specs=[pl.BlockSpec((B, tm, D), tok), _const_spec((1, D)), _const_spec(w.shape)],
        out_specs=[
            ublk, ublk,
            pl.BlockSpec((B, tm, 512), tok),
            pl.BlockSpec((B, tm, 512), tok),
            pl.BlockSpec((B, tm, 512), tok),
            pl.BlockSpec((B, tm, LANES), tok),
        ],
        out_shape=[
            jax.ShapeDtypeStruct((L * 2 * B, LANES), F32),
            jax.ShapeDtypeStruct((L * 2 * B, LANES), F32),
            jax.ShapeDtypeStruct((B, L, 512), BF16),
            jax.ShapeDtypeStruct((B, L, 512), BF16),
            jax.ShapeDtypeStruct((B, L, 512), BF16),
            jax.ShapeDtypeStruct((B, L, LANES), F32),
        ],
        compiler_params=_params("parallel"),
        name="even_inproj",
    )(x, g, w)


def _fox_prep_kernel(f_ref, bf_ref, tri_ref, pq_ref, pk_ref, cq_ref, ck_ref,
                     fq_ref, fk_ref, carry_ref):
    @pl.when(pl.program_id(1) == 0)
    def _():
        carry_ref[...] = jnp.zeros_like(carry_ref)

    z = f_ref[0] + bf_ref[...]
    logf = jnp.minimum(z, 0.0) - jnp.log1p(jnp.exp(-jnp.abs(z)))
    tri = tri_ref[...]
    hi, mid, lo = _split3(logf)
    csum = _dot(tri, hi) + _dot(tri, mid) + _dot(tri, lo) + carry_ref[...]
    tm = csum.shape[0]
    carry_ref[...] = csum[tm - 1:tm, :]
    hi, mid, lo = _split3(csum * LOG2E)
    fq = _dot(hi, pq_ref[0]) + _dot(mid, pq_ref[1]) + _dot(lo, pq_ref[2]) + cq_ref[...]
    fk = _dot(hi, pk_ref[0]) + _dot(mid, pk_ref[1]) + _dot(lo, pk_ref[2]) + ck_ref[...]
    fq_ref[0] = fq.astype(BF16)
    fk_ref[0] = fk.astype(BF16)


def _fox_placement(n_heads):
    width = (n_heads // 2) * LANES
    pq = np.zeros((3, LANES, width), np.float32)
    pk = np.zeros((3, LANES, width), np.float32)
    cq = np.zeros((1, width), np.float32)
    ck = np.zeros((1, width), np.float32)
    for h in range(n_heads):
        base = (h // 2) * LANES + (HEAD_DIM if h % 2 == 0 else 0)
        for j in range(3):
            pq[j, h, base + j] = 1.0
            ck[0, base + j] = 1.0
            cq[0, base + 3 + j] = 1.0
            pk[j, h, base + 3 + j] = -1.0
    return pq, pk, cq, ck


def _fox_prep(fl, b_f):
    B, L, _ = fl.shape
    n_heads = b_f.shape[0]
    width = (n_heads // 2) * LANES
    tm = min(ROW_TILE, L)
    pq, pk, cq, ck = _fox_placement(n_heads)
    tri = jnp.asarray(np.tril(np.ones((tm, tm), np.float32)), BF16)
    bf = jnp.zeros((1, LANES), F32).at[0, :n_heads].set(b_f)
    tok = lambda b, r: (b, r, 0)
    return pl.pallas_call(
        _fox_prep_kernel,
        grid=(B, L // tm),
        in_specs=[pl.BlockSpec((1, tm, LANES), tok), _const_spec((1, LANES)),
                  _const_spec((tm, tm)), _const_spec(pq.shape), _const_spec(pk.shape),
                  _const_spec(cq.shape), _const_spec(ck.shape)],
        out_specs=[pl.BlockSpec((1, tm, width), tok), pl.BlockSpec((1, tm, width), tok)],
        out_shape=[jax.ShapeDtypeStruct((B, L, width), BF16)] * 2,
        scratch_shapes=[pltpu.VMEM((1, LANES), F32)],
        compiler_params=_params("parallel", "arbitrary"),
        name="fox_prep",
    )(fl, bf, tri, jnp.asarray(pq, BF16), jnp.asarray(pk, BF16), jnp.asarray(cq), jnp.asarray(ck))


def _fox_attn_kernel(q_ref, k_ref, v_ref, fq_ref, fk_ref, o_ref,
                     kaug_ref, vaug_ref, qaug_ref, m_ref, acc_ref):
    qi = pl.program_id(2)
    tq = q_ref.shape[1]
    L = k_ref.shape[1]
    tk = min(ATT_TK, L)
    n_heads = kaug_ref.shape[0]
    assert tq % tk == 0

    def merge(lanes, first, x_ref, f):
        x = x_ref[0, :, first * LANES:(first + 1) * LANES].astype(F32)
        return (jnp.where(lanes < HEAD_DIM, x, f(0)).astype(BF16),
                jnp.where(lanes >= HEAD_DIM, x, f(1)).astype(BF16))

    @pl.when(qi == 0)
    def _():
        lanes = lax.broadcasted_iota(jnp.int32, (L, LANES), 1)
        for pr in range(n_heads // 2):
            fk = fk_ref[0, :, pr * LANES:(pr + 1) * LANES].astype(F32)
            kaug_ref[2 * pr], kaug_ref[2 * pr + 1] = merge(lanes, pr, k_ref, lambda e: fk)
            ones = ((lanes == HEAD_DIM).astype(F32), (lanes == 0).astype(F32))
            vaug_ref[2 * pr], vaug_ref[2 * pr + 1] = merge(lanes, pr, v_ref, lambda e: ones[e])

    lanes_q = lax.broadcasted_iota(jnp.int32, (tq, LANES), 1)
    for pr in range(n_heads // 2):
        fq = fq_ref[0, :, pr * LANES:(pr + 1) * LANES].astype(F32)
        qaug_ref[2 * pr], qaug_ref[2 * pr + 1] = merge(lanes_q, pr, q_ref, lambda e: fq)

    m_ref[...] = jnp.full(m_ref.shape, NEG_BIG, F32)
    acc_ref[...] = jnp.zeros(acc_ref.shape, F32)

    def tile(j, r0, r1, masked):
        start = pl.multiple_of(j * tk, tk)
        for e in range(n_heads):
            kt = kaug_ref[e, pl.ds(start, tk), :]
            s = lax.dot_general(qaug_ref[e, r0:r1, :], kt, (((1,), (1,)), ((), ())),
                                preferred_element_type=F32)
            if masked:
                row = lax.broadcasted_iota(jnp.int32, s.shape, 0)
                col = lax.broadcasted_iota(jnp.int32, s.shape, 1)
                s = jnp.where(col <= row, s, NEG_BIG)
            m = m_ref[e, r0:r1, :]
            m_new = jnp.maximum(m, jnp.max(s, axis=1, keepdims=True))
            alpha = jnp.exp2(m - m_new)
            p = jnp.exp2(s - jnp.concatenate([m_new] * (tk // LANES), axis=1)).astype(BF16)
            acc_ref[e, r0:r1, :] = alpha * acc_ref[e, r0:r1, :] + _dot(p, vaug_ref[e, pl.ds(start, tk), :])
            m_ref[e, r0:r1, :] = m_new

    n_diag = tq // tk

    def body(j, c):
        tile(j, 0, tq, False)
        return c

    lax.fori_loop(0, qi * n_diag, body, 0)
    for d in range(n_diag):
        tile(qi * n_diag + d, d * tk, (d + 1) * tk, True)
        if (d + 1) * tk < tq:
            tile(qi * n_diag + d, (d + 1) * tk, tq, False)

    for pr in range(n_heads // 2):
        acc0, acc1 = acc_ref[2 * pr], acc_ref[2 * pr + 1]
        out0 = acc0 / acc0[:, HEAD_DIM:HEAD_DIM + 1]
        out1 = acc1 / acc1[:, 0:1]
        o_ref[0, :, pr * LANES:(pr + 1) * LANES] = jnp.where(lanes_q < HEAD_DIM, out0, out1).astype(BF16)


def _fox_attn(q, k, v, fq, fk):
    B, L, W = q.shape
    bw = ATT_PAIRS * LANES
    n_heads = 2 * ATT_PAIRS
    tq = min(ATT_TQ, L)
    qspec = pl.BlockSpec((1, tq, bw), lambda b, p, i: (b, i, p))
    kspec = pl.BlockSpec((1, L, bw), lambda b, p, i: (b, 0, p))
    return pl.pallas_call(
        _fox_attn_kernel,
        grid=(B, W // bw, L // tq),
        in_specs=[qspec, kspec, kspec, qspec, kspec],
        out_specs=qspec,
        out_shape=jax.ShapeDtypeStruct((B, L, W), BF16),
        scratch_shapes=[pltpu.VMEM((n_heads, L, LANES), BF16), pltpu.VMEM((n_heads, L, LANES), BF16),
                        pltpu.VMEM((n_heads, tq, LANES), BF16), pltpu.VMEM((n_heads, tq, LANES), F32),
                        pltpu.VMEM((n_heads, tq, LANES), F32)],
        compiler_params=_params("parallel", "parallel", "arbitrary"),
        name="fox_attn",
    )(q, k, v, fq, fk)


def _s5_kernel(u0_ref, u1_ref, bst_ref, cst_ref, are_ref, aim_ref, d_ref, o0_ref, o1_ref,
               x_ref, st_ref):
    rows = u0_ref.shape[0]
    nb = x_ref.shape[0] // 2
    hrows = rows // 2

    @pl.when(pl.program_id(0) == 0)
    def _():
        st_ref[...] = jnp.zeros_like(st_ref)

    def half_rows(ref, h, *lead):
        return ref[(*lead, pl.ds(h, hrows, stride=2), slice(None))]

    u_h = []
    for h in range(2):
        u = jnp.concatenate([half_rows(u0_ref, h), half_rows(u1_ref, h)], axis=1)
        u_h.append(u)
        bu = _dot(u.astype(BF16), bst_ref[h])
        for c in range(2 * nb):
            x_ref[c, pl.ds(h, hrows, stride=2), :] = bu[:, c * LANES:(c + 1) * LANES]

    a_re = [are_ref[:, c * LANES:(c + 1) * LANES] for c in range(nb)]
    a_im = [aim_ref[:, c * LANES:(c + 1) * LANES] for c in range(nb)]

    def step(i, state):
        r = pl.ds(pl.multiple_of(i * SUBLANES, SUBLANES), SUBLANES)
        new = []
        for c in range(nb):
            s_re, s_im = state[2 * c], state[2 * c + 1]
            n_re = a_re[c] * s_re - a_im[c] * s_im + x_ref[c, r, :]
            n_im = a_re[c] * s_im + a_im[c] * s_re + x_ref[nb + c, r, :]
            x_ref[c, r, :] = n_re
            x_ref[nb + c, r, :] = n_im
            new += [n_re, n_im]
        return tuple(new)

    state = lax.fori_loop(0, rows // SUBLANES, step,
                          tuple(st_ref[c] for c in range(2 * nb)), unroll=4)
    for c in range(2 * nb):
        st_ref[c] = state[c]

    for h in range(2):
        xs = jnp.concatenate([half_rows(x_ref, h, c) for c in range(nb)]
                             + [half_rows(x_ref, h, nb + c) for c in range(nb)], axis=1)
        y = _dot(xs.astype(BF16), cst_ref[h])
        g = jax.nn.gelu(y + d_ref[h:h + 1, :] * u_h[h])
        o0_ref[pl.ds(h, hrows, stride=2), :] = g[:, 0:LANES]
        o1_ref[pl.ds(h, hrows, stride=2), :] = g[:, LANES:2 * LANES]


def _s5_discretize(lam_re, lam_im, log_dt, b_re, b_im):
    dt = jnp.exp(log_dt)[:, None]
    mag = jnp.exp(lam_re * dt)
    ab_re = mag * jnp.cos(lam_im * dt)
    ab_im = mag * jnp.sin(lam_im * dt)
    den = lam_re * lam_re + lam_im * lam_im
    nr = ab_re - 1.0
    ni = ab_im
    q_re = (nr * lam_re + ni * lam_im) / den
    q_im = (ni * lam_re - nr * lam_im) / den
    bb_re = q_re[..., None] * b_re - q_im[..., None] * b_im
    bb_im = q_re[..., None] * b_im + q_im[..., None] * b_re
    return ab_re, ab_im, bb_re, bb_im


def _s5_matrices(ab_re, ab_im, bb_re, bb_im, c_re, c_im, d, batch):
    G, P, H = bb_re.shape
    gh = G // 2
    eye = jnp.eye(gh, dtype=F32)

    def bmat(bb):
        t = jnp.einsum('zgph,gk->zghkp', bb.reshape(2, gh, P, H), eye)
        return t.reshape(2 * gh * H, gh * P)

    def cmat(cc):
        t = jnp.einsum('zghp,gk->kpzgh', cc.reshape(2, gh, H, P), eye)
        return t.reshape(gh * P, 2 * gh * H)

    hw, ns = gh * H, gh * P
    bst = jnp.concatenate([bmat(bb_re), bmat(bb_im)], axis=1).reshape(2, hw, 2 * ns).astype(BF16)
    cst = jnp.concatenate([cmat(c_re), -cmat(c_im)], axis=0).reshape(2 * ns, 2, hw)
    cst = jnp.transpose(cst, (1, 0, 2)).astype(BF16)
    a_re = jnp.tile(ab_re.reshape(2, ns), (batch, 1))
    a_im = jnp.tile(ab_im.reshape(2, ns), (batch, 1))
    return bst, cst, a_re, a_im, d.reshape(2, hw)


def _s5_scan(u0, u1, bst, cst, a_re, a_im, d2):
    n_rows = u0.shape[0]
    ns = a_re.shape[1]
    rows = min(S5_TILE * SUBLANES, n_rows)
    blk = pl.BlockSpec((rows, LANES), lambda t: (t, 0))
    return pl.pallas_call(
        _s5_kernel,
        grid=(n_rows // rows,),
        in_specs=[blk, blk, _const_spec(bst.shape), _const_spec(cst.shape),
                  _const_spec(a_re.shape), _const_spec(a_im.shape), _const_spec(d2.shape)],
        out_specs=[blk, blk],
        out_shape=[jax.ShapeDtypeStruct((n_rows, LANES), F32)] * 2,
        scratch_shapes=[pltpu.VMEM((2 * ns // LANES, rows, LANES), F32),
                        pltpu.VMEM((2 * ns // LANES, SUBLANES, LANES), F32)],
        compiler_params=_params("arbitrary"),
        name="s5_scan",
    )(u0, u1, bst, cst, a_re, a_im, d2)


def _mlp_rows(x, gpre_ref, gpost_ref, w1_ref, w2_ref):
    h = _rms(x, gpre_ref[...]).astype(BF16)
    d_ff = w1_ref.shape[1]
    acc = jnp.zeros(x.shape, F32)
    for f in range(0, d_ff, MLP_FF_CHUNK):
        a = jnp.maximum(_dot(h, w1_ref[:, f:f + MLP_FF_CHUNK]), 0.0)
        acc = acc + _dot((a * a).astype(BF16), w2_ref[f:f + MLP_FF_CHUNK, :])
    return x + _rms(acc, gpost_ref[...])


def _even_post_kernel(x_ref, g0_ref, g1_ref, yb_ref, wglu_ref, wout_ref, g_ref,
                      gpre_ref, gpost_ref, w1_ref, w2_ref, o_ref):
    nb, tm, _ = x_ref.shape
    half_w = wout_ref.shape[0] // 2
    x1 = []
    for b in range(nb):
        blocks = [ref[pl.ds(2 * b + half, tm, stride=2 * nb), :]
                  for half in range(2) for ref in (g0_ref, g1_ref)]
        ga = jnp.concatenate(blocks, axis=1)
        ya = ga * jax.nn.sigmoid(_dot(ga.astype(BF16), wglu_ref[...]))
        mix = (_dot(ya.astype(BF16), wout_ref[0:half_w, :])
               + _dot(yb_ref[b], wout_ref[half_w:2 * half_w, :]))
        x1.append(x_ref[b] + _rms(mix, g_ref[...]))
    out = _mlp_rows(jnp.concatenate(x1, axis=0), gpre_ref, gpost_ref, w1_ref, w2_ref)
    for b in range(nb):
        o_ref[b] = out[b * tm:(b + 1) * tm]


def _even_post(x, g0, g1, yb, w_glu, w_out, g, g_pre, g_post, w1, w2):
    B, L, D = x.shape
    tm = min(POST_TILE, L)
    tok = lambda r: (0, r, 0)
    gblk = pl.BlockSpec((tm * 2 * B, LANES), lambda r: (r, 0))
    consts = (w_glu, w_out, g, g_pre, g_post, w1, w2)
    return pl.pallas_call(
        _even_post_kernel,
        grid=(L // tm,),
        in_specs=[pl.BlockSpec((B, tm, D), tok), gblk, gblk, pl.BlockSpec((B, tm, 512), tok)]
                 + [_const_spec(c.shape) for c in consts],
        out_specs=pl.BlockSpec((B, tm, D), tok),
        out_shape=jax.ShapeDtypeStruct((B, L, D), F32),
        compiler_params=_params("parallel"),
        name="even_post_mlp",
    )(x, g0, g1, yb, *consts)


def _mlp_kernel(x_ref, gpre_ref, gpost_ref, w1_ref, w2_ref, o_ref):
    o_ref[...] = _mlp_rows(x_ref[...], gpre_ref, gpost_ref, w1_ref, w2_ref)


def _mlp(x2, g_pre, g_post, w1, w2):
    N, D = x2.shape
    tm = min(ROW_TILE, N)
    row = pl.BlockSpec((tm, D), lambda r: (r, 0))
    return pl.pallas_call(
        _mlp_kernel,
        grid=(N // tm,),
        in_specs=[row, _const_spec((1, D)), _const_spec((1, D)),
                  _const_spec(w1.shape), _const_spec(w2.shape)],
        out_specs=row,
        out_shape=jax.ShapeDtypeStruct((N, D), F32),
        compiler_params=_params("parallel"),
        name="mlp",
    )(x2, g_pre, g_post, w1, w2)


def _odd_mixer_kernel(x_ref, gpre_ref, win_ref, pw_ref, ps_ref, lng_ref, lnb_ref, ws_ref,
                      bs_ref, wout_ref, gpost_ref, o_ref, pool_ref):
    r = pl.program_id(1)
    x = x_ref[0]
    tm = x.shape[0]
    pad = max(POOL_WINDOWS)
    gd = LANES
    n_grp = len(POOL_WINDOWS)
    width = n_grp * gd

    h = _rms(x, gpre_ref[...]).astype(BF16)
    xc = _dot(h, win_ref[:, 0:width])
    u = _dot(h, win_ref[:, width:2 * width])
    v = _dot(h, win_ref[:, 2 * width:3 * width])

    @pl.when(r == 0)
    def _():
        pool_ref[0:pad, :] = jnp.zeros((pad, width), F32)

    pool_ref[pad:pad + tm, :] = xc
    t = r * tm + lax.broadcasted_iota(jnp.int32, (tm, 1), 0)
    yc = []
    for g, w in enumerate(POOL_WINDOWS):
        cols = slice(g * gd, (g + 1) * gd)
        tot = xc[:, cols]
        for lag in range(1, w):
            tot = tot + pool_ref[pl.ds(pad - lag, tm), cols]
        cnt = jnp.minimum(t + 1, w).astype(F32)
        pooled = tot / cnt - xc[:, cols]
        yc.append(_dot(pooled.astype(BF16), pw_ref[g]))
    pool_ref[0:pad, :] = pool_ref[tm:tm + pad, :]
    y_c = jnp.concatenate(yc, axis=-1) * ps_ref[...]

    gv = jax.nn.gelu(v)
    mu = jnp.mean(gv, axis=-1, keepdims=True)
    vc = gv - mu
    vn = vc * lax.rsqrt(jnp.mean(vc * vc, axis=-1, keepdims=True) + EPS)
    vn = (vn * lng_ref[...] + lnb_ref[...]).astype(BF16)
    rows = lax.broadcasted_iota(jnp.int32, (CHUNK, CHUNK), 0)
    cols_i = lax.broadcasted_iota(jnp.int32, (CHUNK, CHUNK), 1)
    ws = [jnp.where(cols_i <= rows, ws_ref[g], 0.0).astype(BF16) for g in range(n_grp)]
    chunks = []
    for n in range(tm // CHUNK):
        parts = [_dot(ws[g], vn[n * CHUNK:(n + 1) * CHUNK, g * gd:(g + 1) * gd]) for g in range(n_grp)]
        chunks.append(jnp.concatenate(parts, axis=-1) + bs_ref[...])
    y_d = jax.nn.gelu(u) * jnp.concatenate(chunks, axis=0)

    mix = _dot(y_c.astype(BF16), wout_ref[0:width, :]) + _dot(y_d.astype(BF16), wout_ref[width:2 * width, :])
    o_ref[0] = x + _rms(mix, gpost_ref[...])


def _odd_mixer(x, g_pre, w_in, pool_w, pool_scale, ln_g, ln_b, w_s, bs_full, w_out, g_post):
    B, L, D = x.shape
    tm = min(ROW_TILE, L)
    width = pool_scale.shape[1]
    tok = lambda b, r: (b, r, 0)
    consts = (g_pre, w_in, pool_w, pool_scale, ln_g, ln_b, w_s, bs_full, w_out, g_post)
    return pl.pallas_call(
        _odd_mixer_kernel,
        grid=(B, L // tm),
        in_specs=[pl.BlockSpec((1, tm, D), tok)] + [_const_spec(c.shape) for c in consts],
        out_specs=pl.BlockSpec((1, tm, D), tok),
        out_shape=jax.ShapeDtypeStruct((B, L, D), F32),
        scratch_shapes=[pltpu.VMEM((tm + max(POOL_WINDOWS), width), F32)],
        compiler_params=_params("parallel", "arbitrary"),
        name="odd_mixer",
    )(x, *consts)


def _even_layer(x, g_pre, g_post, w_in, lam_re, lam_im, log_dt, b_re, b_im, c_re, c_im, d, w_glu,
                b_f, w_out, mlp_g_pre, mlp_g_post, w1, w2):
    B, L, D = x.shape
    n_heads = b_f.shape[0]
    s5w = d.shape[0]
    fw = n_heads * HEAD_DIM
    qs = HEAD_DIM ** -0.5 * LOG2E
    w_u = w_in[:, :s5w].reshape(D, 2, 2, LANES).transpose(0, 2, 1, 3).reshape(D, s5w)
    w = jnp.concatenate([
        w_u, w_in[:, s5w:s5w + fw] * qs, w_in[:, s5w + fw:s5w + 3 * fw],
        jnp.pad(w_in[:, s5w + 3 * fw:], ((0, 0), (0, LANES - n_heads)))], axis=1).astype(BF16)
    u0, u1, q, k, v, fl = _even_inproj(x, g_pre[None], w)

    fq, fk = _fox_prep(fl, b_f)
    y_b = _fox_attn(q, k, v, fq, fk)

    ab_re, ab_im, bb_re, bb_im = _s5_discretize(lam_re, lam_im, log_dt, b_re, b_im)
    bst, cst, a_re, a_im, d2 = _s5_matrices(ab_re, ab_im, bb_re, bb_im, c_re, c_im, d, B)
    g0, g1 = _s5_scan(u0, u1, bst, cst, a_re, a_im, d2)

    return _even_post(x, g0, g1, y_b, w_glu.astype(BF16), w_out.astype(BF16), g_post[None],
                      mlp_g_pre[None], mlp_g_post[None], w1.astype(BF16), w2.astype(BF16))


def _odd_layer(x, g_pre, g_post, w_in, pool_w, pool_scale, ln_g, ln_b, w_s, b_s, w_out):
    bs_full = jnp.repeat(jnp.transpose(b_s), LANES, axis=1)
    return _odd_mixer(x, g_pre[None], w_in.astype(BF16), pool_w.astype(BF16), pool_scale[None],
                      ln_g[None], ln_b[None], w_s, bs_full, w_out.astype(BF16), g_post[None])


def kernel(x, mix_pre_g, mix_post_g, mlp_pre_g, mlp_post_g, w_in_even, s5_lam_re, s5_lam_im, s5_log_dt, s5_b_re, s5_b_im, s5_c_re, s5_c_im, s5_d, s5_w_glu, fox_b_f, w_out_even, w_in_odd, pool_w, pool_scale, sgu_ln_g, sgu_ln_b, sgu_w_s, sgu_b_s, w_out_odd, mlp_w1, mlp_w2):
    B, L, D = x.shape
    depth = mix_pre_g.shape[0]
    for l in range(depth):
        i = l // 2
        if l % 2 == 0:
            x = _even_layer(x, mix_pre_g[l], mix_post_g[l], w_in_even[i], s5_lam_re[i], s5_lam_im[i],
                            s5_log_dt[i], s5_b_re[i], s5_b_im[i], s5_c_re[i], s5_c_im[i], s5_d[i],
                            s5_w_glu[i], fox_b_f[i], w_out_even[i],
                            mlp_pre_g[l], mlp_post_g[l], mlp_w1[l], mlp_w2[l])
        else:
            x = _odd_layer(x, mix_pre_g[l], mix_post_g[l], w_in_odd[i], pool_w[i], pool_scale[i],
                           sgu_ln_g[i], sgu_ln_b[i], sgu_w_s[i], sgu_b_s[i], w_out_odd[i])
            x = _mlp(x.reshape(B * L, D), mlp_pre_g[l][None], mlp_post_g[l][None],
                     mlp_w1[l].astype(BF16), mlp_w2[l].astype(BF16)).reshape(B, L, D)
    return x
```

```python
import functools

import numpy as np
import jax
import jax.numpy as jnp
from jax import lax
from jax.experimental import pallas as pl
from jax.experimental.pallas import tpu as pltpu

F32 = jnp.float32
BF16 = jnp.bfloat16

EPS = 1e-6
LOG2E = 1.4426950408889634
NEG_BIG = -1e30

LANES = 128
SUBLANES = 8
VMEM_LIMIT = 56 * 1024 * 1024

S5_GROUP = 16
S5_STATE = 64
HEAD_DIM = 64
POOL_WINDOWS = (2, 4, 8, 16)
CHUNK = 128

ROW_TILE = 512
POST_TILE = 256
MLP_FF_CHUNK = 1024
S5_TILE = 128
ATT_TQ = 1024
ATT_TK = 512
ATT_PAIRS = 2


def _params(*sem):
    return pltpu.CompilerParams(dimension_semantics=sem, vmem_limit_bytes=VMEM_LIMIT)


def _const_spec(shape):
    nd = len(shape)
    return pl.BlockSpec(shape, lambda *_: (0,) * nd, pipeline_mode=pl.Buffered(1))


def _rms(x, g):
    return x * lax.rsqrt(jnp.mean(x * x, axis=-1, keepdims=True) + EPS) * g


def _split3(x):
    hi = x.astype(BF16)
    r1 = x - hi.astype(F32)
    mid = r1.astype(BF16)
    lo = (r1 - mid.astype(F32)).astype(BF16)
    return hi, mid, lo


def _dot(a, b):
    return jnp.dot(a, b, preferred_element_type=F32)


def _even_inproj_kernel(x_ref, g_ref, w_ref, u0_ref, u1_ref, q_ref, k_ref, v_ref, f_ref):
    nb, tm, _ = x_ref.shape
    for b in range(nb):
        h = _rms(x_ref[b], g_ref[...]).astype(BF16)
        zu = _dot(h, w_ref[:, 0:512])
        for half in range(2):
            rows = pl.ds(2 * b + half, tm, stride=2 * nb)
            u0_ref[rows, :] = zu[:, half * LANES:(half + 1) * LANES]
            u1_ref[rows, :] = zu[:, (2 + half) * LANES:(3 + half) * LANES]
        q_ref[b] = _dot(h, w_ref[:, 512:1024]).astype(BF16)
        k_ref[b] = _dot(h, w_ref[:, 1024:1536]).astype(BF16)
        v_ref[b] = _dot(h, w_ref[:, 1536:2048]).astype(BF16)
        f_ref[b] = _dot(h, w_ref[:, 2048:2176])


def _even_inproj(x, g, w):
    B, L, D = x.shape
    tm = min(ROW_TILE, L)
    tok = lambda r: (0, r, 0)
    ublk = pl.BlockSpec((tm * 2 * B, LANES), lambda r: (r, 0))
    return pl.pallas_call(
        _even_inproj_kernel,
        grid=(L // tm,),
        in_specs=[pl.BlockSpec((B, tm, D), tok), _const_spec((1, D)), _const_spec(w.shape)],
        out_specs=[
            ublk, ublk,
            pl.BlockSpec((B, tm, 512), tok),
            pl.BlockSpec((B, tm, 512), tok),
            pl.BlockSpec((B, tm, 512), tok),
            pl.BlockSpec((B, tm, LANES), tok),
        ],
        out_shape=[
            jax.ShapeDtypeStruct((L * 2 * B, LANES), F32),
            jax.ShapeDtypeStruct((L * 2 * B, LANES), F32),
            jax.ShapeDtypeStruct((B, L, 512), BF16),
            jax.ShapeDtypeStruct((B, L, 512), BF16),
            jax.ShapeDtypeStruct((B, L, 512), BF16),
            jax.ShapeDtypeStruct((B, L, LANES), F32),
        ],
        compiler_params=_params("parallel"),
        name="even_inproj",
    )(x, g, w)


def _fox_prep_kernel(f_ref, bf_ref, tri_ref, pq_ref, pk_ref, cq_ref, ck_ref,
                     fq_ref, fk_ref, carry_ref):
    @pl.when(pl.program_id(1) == 0)
    def _():
        carry_ref[...] = jnp.zeros_like(carry_ref)

    z = f_ref[0] + bf_ref[...]
    logf = jnp.minimum(z, 0.0) - jnp.log1p(jnp.exp(-jnp.abs(z)))
    tri = tri_ref[...]
    hi, mid, lo = _split3(logf)
    csum = _dot(tri, hi) + _dot(tri, mid) + _dot(tri, lo) + carry_ref[...]
    tm = csum.shape[0]
    carry_ref[...] = csum[tm - 1:tm, :]
    hi, mid, lo = _split3(csum * LOG2E)
    fq = _dot(hi, pq_ref[0]) + _dot(mid, pq_ref[1]) + _dot(lo, pq_ref[2]) + cq_ref[...]
    fk = _dot(hi, pk_ref[0]) + _dot(mid, pk_ref[1]) + _dot(lo, pk_ref[2]) + ck_ref[...]
    fq_ref[0] = fq.astype(BF16)
    fk_ref[0] = fk.astype(BF16)


def _fox_placement(n_heads):
    width = (n_heads // 2) * LANES
    pq = np.zeros((3, LANES, width), np.float32)
    pk = np.zeros((3, LANES, width), np.float32)
    cq = np.zeros((1, width), np.float32)
    ck = np.zeros((1, width), np.float32)
    for h in range(n_heads):
        base = (h // 2) * LANES + (HEAD_DIM if h % 2 == 0 else 0)
        for j in range(3):
            pq[j, h, base + j] = 1.0
            ck[0, base + j] = 1.0
            cq[0, base + 3 + j] = 1.0
            pk[j, h, base + 3 + j] = -1.0
    return pq, pk, cq, ck


def _fox_prep(fl, b_f):
    B, L, _ = fl.shape
    n_heads = b_f.shape[0]
    width = (n_heads // 2) * LANES
    tm = min(ROW_TILE, L)
    pq, pk, cq, ck = _fox_placement(n_heads)
    tri = jnp.asarray(np.tril(np.ones((tm, tm), np.float32)), BF16)
    bf = jnp.zeros((1, LANES), F32).at[0, :n_heads].set(b_f)
    tok = lambda b, r: (b, r, 0)
    return pl.pallas_call(
        _fox_prep_kernel,
        grid=(B, L // tm),
        in_specs=[pl.BlockSpec((1, tm, LANES), tok), _const_spec((1, LANES)),
                  _const_spec((tm, tm)), _const_spec(pq.shape), _const_spec(pk.shape),
                  _const_spec(cq.shape), _const_spec(ck.shape)],
        out_specs=[pl.BlockSpec((1, tm, width), tok), pl.BlockSpec((1, tm, width), tok)],
        out_shape=[jax.ShapeDtypeStruct((B, L, width), BF16)] * 2,
        scratch_shapes=[pltpu.VMEM((1, LANES), F32)],
        compiler_params=_params("parallel", "arbitrary"),
        name="fox_prep",
    )(fl, bf, tri, jnp.asarray(pq, BF16), jnp.asarray(pk, BF16), jnp.asarray(cq), jnp.asarray(ck))


def _fox_attn_kernel(q_ref, k_ref, v_ref, fq_ref, fk_ref, o_ref,
                     kaug_ref, vaug_ref, qaug_ref, m_ref, acc_ref):
    qi = pl.program_id(2)
    tq = q_ref.shape[1]
    L = k_ref.shape[1]
    tk = min(ATT_TK, L)
    n_heads = kaug_ref.shape[0]
    assert tq % tk == 0

    def merge(lanes, first, x_ref, f):
        x = x_ref[0, :, first * LANES:(first + 1) * LANES].astype(F32)
        return (jnp.where(lanes < HEAD_DIM, x, f(0)).astype(BF16),
                jnp.where(lanes >= HEAD_DIM, x, f(1)).astype(BF16))

    @pl.when(qi == 0)
    def _():
        lanes = lax.broadcasted_iota(jnp.int32, (L, LANES), 1)
        for pr in range(n_heads // 2):
            fk = fk_ref[0, :, pr * LANES:(pr + 1) * LANES].astype(F32)
            kaug_ref[2 * pr], kaug_ref[2 * pr + 1] = merge(lanes, pr, k_ref, lambda e: fk)
            ones = ((lanes == HEAD_DIM).astype(F32), (lanes == 0).astype(F32))
            vaug_ref[2 * pr], vaug_ref[2 * pr + 1] = merge(lanes, pr, v_ref, lambda e: ones[e])

    lanes_q = lax.broadcasted_iota(jnp.int32, (tq, LANES), 1)
    for pr in range(n_heads // 2):
        fq = fq_ref[0, :, pr * LANES:(pr + 1) * LANES].astype(F32)
        qaug_ref[2 * pr], qaug_ref[2 * pr + 1] = merge(lanes_q, pr, q_ref, lambda e: fq)

    m_ref[...] = jnp.full(m_ref.shape, NEG_BIG, F32)
    acc_ref[...] = jnp.zeros(acc_ref.shape, F32)

    def tile(j, r0, r1, masked):
        start = pl.multiple_of(j * tk, tk)
        for e in range(n_heads):
            kt = kaug_ref[e, pl.ds(start, tk), :]
            s = lax.dot_general(qaug_ref[e, r0:r1, :], kt, (((1,), (1,)), ((), ())),
                                preferred_element_type=F32)
            if masked:
                row = lax.broadcasted_iota(jnp.int32, s.shape, 0)
                col = lax.broadcasted_iota(jnp.int32, s.shape, 1)
                s = jnp.where(col <= row, s, NEG_BIG)
            m = m_ref[e, r0:r1, :]
            m_new = jnp.maximum(m, jnp.max(s, axis=1, keepdims=True))
            alpha = jnp.exp2(m - m_new)
            p = jnp.exp2(s - jnp.concatenate([m_new] * (tk // LANES), axis=1)).astype(BF16)
            acc_ref[e, r0:r1, :] = alpha * acc_ref[e, r0:r1, :] + _dot(p, vaug_ref[e, pl.ds(start, tk), :])
            m_ref[e, r0:r1, :] = m_new

    n_diag = tq // tk

    def body(j, c):
        tile(j, 0, tq, False)
        return c

    lax.fori_loop(0, qi * n_diag, body, 0)
    for d in range(n_diag):
        tile(qi * n_diag + d, d * tk, (d + 1) * tk, True)
        if (d + 1) * tk < tq:
            tile(qi * n_diag + d, (d + 1) * tk, tq, False)

    for pr in range(n_heads // 2):
        acc0, acc1 = acc_ref[2 * pr], acc_ref[2 * pr + 1]
        out0 = acc0 / acc0[:, HEAD_DIM:HEAD_DIM + 1]
        out1 = acc1 / acc1[:, 0:1]
        o_ref[0, :, pr * LANES:(pr + 1) * LANES] = jnp.where(lanes_q < HEAD_DIM, out0, out1).astype(BF16)


def _fox_attn(q, k, v, fq, fk):
    B, L, W = q.shape
    bw = ATT_PAIRS * LANES
    n_heads = 2 * ATT_PAIRS
    tq = min(ATT_TQ, L)
    qspec = pl.BlockSpec((1, tq, bw), lambda b, p, i: (b, i, p))
    kspec = pl.BlockSpec((1, L, bw), lambda b, p, i: (b, 0, p))
    return pl.pallas_call(
        _fox_attn_kernel,
        grid=(B, W // bw, L // tq),
        in_specs=[qspec, kspec, kspec, qspec, kspec],
        out_specs=qspec,
        out_shape=jax.ShapeDtypeStruct((B, L, W), BF16),
        scratch_shapes=[pltpu.VMEM((n_heads, L, LANES), BF16), pltpu.VMEM((n_heads, L, LANES), BF16),
                        pltpu.VMEM((n_heads, tq, LANES), BF16), pltpu.VMEM((n_heads, tq, LANES), F32),
                        pltpu.VMEM((n_heads, tq, LANES), F32)],
        compiler_params=_params("parallel", "parallel", "arbitrary"),
        name="fox_attn",
    )(q, k, v, fq, fk)


def _s5_kernel(u0_ref, u1_ref, bst_ref, cst_ref, are_ref, aim_ref, d_ref, o0_ref, o1_ref,
               x_ref, st_ref):
    rows = u0_ref.shape[0]
    nb = x_ref.shape[0] // 2
    hrows = rows // 2

    @pl.when(pl.program_id(0) == 0)
    def _():
        st_ref[...] = jnp.zeros_like(st_ref)

    def half_rows(ref, h, *lead):
        return ref[(*lead, pl.ds(h, hrows, stride=2), slice(None))]

    u_h = []
    for h in range(2):
        u = jnp.concatenate([half_rows(u0_ref, h), half_rows(u1_ref, h)], axis=1)
        u_h.append(u)
        bu = _dot(u.astype(BF16), bst_ref[h])
        for c in range(2 * nb):
            x_ref[c, pl.ds(h, hrows, stride=2), :] = bu[:, c * LANES:(c + 1) * LANES]

    a_re = [are_ref[:, c * LANES:(c + 1) * LANES] for c in range(nb)]
    a_im = [aim_ref[:, c * LANES:(c + 1) * LANES] for c in range(nb)]

    def step(i, state):
        r = pl.ds(pl.multiple_of(i * SUBLANES, SUBLANES), SUBLANES)
        new = []
        for c in range(nb):
            s_re, s_im = state[2 * c], state[2 * c + 1]
            n_re = a_re[c] * s_re - a_im[c] * s_im + x_ref[c, r, :]
            n_im = a_re[c] * s_im + a_im[c] * s_re + x_ref[nb + c, r, :]
            x_ref[c, r, :] = n_re
            x_ref[nb + c, r, :] = n_im
            new += [n_re, n_im]
        return tuple(new)

    state = lax.fori_loop(0, rows // SUBLANES, step,
                          tuple(st_ref[c] for c in range(2 * nb)), unroll=4)
    for c in range(2 * nb):
        st_ref[c] = state[c]

    for h in range(2):
        xs = jnp.concatenate([half_rows(x_ref, h, c) for c in range(nb)]
                             + [half_rows(x_ref, h, nb + c) for c in range(nb)], axis=1)
        y = _dot(xs.astype(BF16), cst_ref[h])
        g = jax.nn.gelu(y + d_ref[h:h + 1, :] * u_h[h])
        o0_ref[pl.ds(h, hrows, stride=2), :] = g[:, 0:LANES]
        o1_ref[pl.ds(h, hrows, stride=2), :] = g[:, LANES:2 * LANES]


def _s5_discretize(lam_re, lam_im, log_dt, b_re, b_im):
    dt = jnp.exp(log_dt)[:, None]
    mag = jnp.exp(lam_re * dt)
    ab_re = mag * jnp.cos(lam_im * dt)
    ab_im = mag * jnp.sin(lam_im * dt)
    den = lam_re * lam_re + lam_im * lam_im
    nr = ab_re - 1.0
    ni = ab_im
    q_re = (nr * lam_re + ni * lam_im) / den
    q_im = (ni * lam_re - nr * lam_im) / den
    bb_re = q_re[..., None] * b_re - q_im[..., None] * b_im
    bb_im = q_re[..., None] * b_im + q_im[..., None] * b_re
    return ab_re, ab_im, bb_re, bb_im


def _s5_matrices(ab_re, ab_im, bb_re, bb_im, c_re, c_im, d, batch):
    G, P, H = bb_re.shape
    gh = G // 2
    eye = jnp.eye(gh, dtype=F32)

    def bmat(bb):
        t = jnp.einsum('zgph,gk->zghkp', bb.reshape(2, gh, P, H), eye)
        return t.reshape(2 * gh * H, gh * P)

    def cmat(cc):
        t = jnp.einsum('zghp,gk->kpzgh', cc.reshape(2, gh, H, P), eye)
        return t.reshape(gh * P, 2 * gh * H)

    hw, ns = gh * H, gh * P
    bst = jnp.concatenate([bmat(bb_re), bmat(bb_im)], axis=1).reshape(2, hw, 2 * ns).astype(BF16)
    cst = jnp.concatenate([cmat(c_re), -cmat(c_im)], axis=0).reshape(2 * ns, 2, hw)
    cst = jnp.transpose(cst, (1, 0, 2)).astype(BF16)
    a_re = jnp.tile(ab_re.reshape(2, ns), (batch, 1))
    a_im = jnp.tile(ab_im.reshape(2, ns), (batch, 1))
    return bst, cst, a_re, a_im, d.reshape(2, hw)


def _s5_scan(u0, u1, bst, cst, a_re, a_im, d2):
    n_rows = u0.shape[0]
    ns = a_re.shape[1]
    rows = min(S5_TILE * SUBLANES, n_rows)
    blk = pl.BlockSpec((rows, LANES), lambda t: (t, 0))
    return pl.pallas_call(
        _s5_kernel,
        grid=(n_rows // rows,),
        in_specs=[blk, blk, _const_spec(bst.shape), _const_spec(cst.shape),
                  _const_spec(a_re.shape), _const_spec(a_im.shape), _const_spec(d2.shape)],
        out_specs=[blk, blk],
        out_shape=[jax.ShapeDtypeStruct((n_rows, LANES), F32)] * 2,
        scratch_shapes=[pltpu.VMEM((2 * ns // LANES, rows, LANES), F32),
                        pltpu.VMEM((2 * ns // LANES, SUBLANES, LANES), F32)],
        compiler_params=_params("arbitrary"),
        name="s5_scan",
    )(u0, u1, bst, cst, a_re, a_im, d2)


def _mlp_rows(x, gpre_ref, gpost_ref, w1_ref, w2_ref):
    h = _rms(x, gpre_ref[...]).astype(BF16)
    d_ff = w1_ref.shape[1]
    acc = jnp.zeros(x.shape, F32)
    for f in range(0, d_ff, MLP_FF_CHUNK):
        a = jnp.maximum(_dot(h, w1_ref[:, f:f + MLP_FF_CHUNK]), 0.0)
        acc = acc + _dot((a * a).astype(BF16), w2_ref[f:f + MLP_FF_CHUNK, :])
    return x + _rms(acc, gpost_ref[...])


def _even_post_kernel(x_ref, g0_ref, g1_ref, yb_ref, wglu_ref, wout_ref, g_ref,
                      gpre_ref, gpost_ref, w1_ref, w2_ref, o_ref):
    nb, tm, _ = x_ref.shape
    half_w = wout_ref.shape[0] // 2
    x1 = []
    for b in range(nb):
        blocks = [ref[pl.ds(2 * b + half, tm, stride=2 * nb), :]
                  for half in range(2) for ref in (g0_ref, g1_ref)]
        ga = jnp.concatenate(blocks, axis=1)
        ya = ga * jax.nn.sigmoid(_dot(ga.astype(BF16), wglu_ref[...]))
        mix = (_dot(ya.astype(BF16), wout_ref[0:half_w, :])
               + _dot(yb_ref[b], wout_ref[half_w:2 * half_w, :]))
        x1.append(x_ref[b] + _rms(mix, g_ref[...]))
    out = _mlp_rows(jnp.concatenate(x1, axis=0), gpre_ref, gpost_ref, w1_ref, w2_ref)
    for b in range(nb):
        o_ref[b] = out[b * tm:(b + 1) * tm]


def _even_post(x, g0, g1, yb, w_glu, w_out, g, g_pre, g_post, w1, w2):
    B, L, D = x.shape
    tm = min(POST_TILE, L)
    tok = lambda r: (0, r, 0)
    gblk = pl.BlockSpec((tm * 2 * B, LANES), lambda r: (r, 0))
    consts = (w_glu, w_out, g, g_pre, g_post, w1, w2)
    return pl.pallas_call(
        _even_post_kernel,
        grid=(L // tm,),
        in_specs=[pl.BlockSpec((B, tm, D), tok), gblk, gblk, pl.BlockSpec((B, tm, 512), tok)]
                 + [_const_spec(c.shape) for c in consts],
        out_specs=pl.BlockSpec((B, tm, D), tok),
        out_shape=jax.ShapeDtypeStruct((B, L, D), F32),
        compiler_params=_params("parallel"),
        name="even_post_mlp",
    )(x, g0, g1, yb, *consts)


def _odd_mixer_rows(x, t0, pool_ref, gpre_ref, win_ref, pw_ref, ps_ref, lng_ref, lnb_ref, ws,
                    bs_ref, wout_ref, gpost_ref):
    tm = x.shape[0]
    pad = max(POOL_WINDOWS)
    gd = LANES
    n_grp = len(POOL_WINDOWS)
    width = n_grp * gd

    h = _rms(x, gpre_ref[...]).astype(BF16)
    xc = _dot(h, win_ref[:, 0:width])
    u = _dot(h, win_ref[:, width:2 * width])
    v = _dot(h, win_ref[:, 2 * width:3 * width])

    pool_ref[pad:pad + tm, :] = xc
    t = t0 + lax.broadcasted_iota(jnp.int32, (tm, 1), 0)
    yc = []
    for g, w in enumerate(POOL_WINDOWS):
        cols = slice(g * gd, (g + 1) * gd)
        tot = xc[:, cols]
        for lag in range(1, w):
            tot = tot + pool_ref[pl.ds(pad - lag, tm), cols]
        cnt = jnp.minimum(t + 1, w).astype(F32)
        pooled = tot / cnt - xc[:, cols]
        yc.append(_dot(pooled.astype(BF16), pw_ref[g]))
    pool_ref[0:pad, :] = pool_ref[tm:tm + pad, :]
    y_c = jnp.concatenate(yc, axis=-1) * ps_ref[...]

    gv = jax.nn.gelu(v)
    mu = jnp.mean(gv, axis=-1, keepdims=True)
    vc = gv - mu
    vn = vc * lax.rsqrt(jnp.mean(vc * vc, axis=-1, keepdims=True) + EPS)
    vn = (vn * lng_ref[...] + lnb_ref[...]).astype(BF16)
    chunks = []
    for n in range(tm // CHUNK):
        parts = [_dot(ws[g], vn[n * CHUNK:(n + 1) * CHUNK, g * gd:(g + 1) * gd]) for g in range(n_grp)]
        chunks.append(jnp.concatenate(parts, axis=-1) + bs_ref[...])
    y_d = jax.nn.gelu(u) * jnp.concatenate(chunks, axis=0)

    mix = _dot(y_c.astype(BF16), wout_ref[0:width, :]) + _dot(y_d.astype(BF16), wout_ref[width:2 * width, :])
    return x + _rms(mix, gpost_ref[...])


def _odd_layer_kernel(x_ref, gpre_ref, win_ref, pw_ref, ps_ref, lng_ref, lnb_ref, ws_ref,
                      bs_ref, wout_ref, gpost_ref, mgpre_ref, mgpost_ref, w1_ref, w2_ref,
                      o_ref, pool_ref):
    r = pl.program_id(0)
    nb, tm, _ = x_ref.shape
    pad = max(POOL_WINDOWS)

    @pl.when(r == 0)
    def _():
        pool_ref[:, 0:pad, :] = jnp.zeros((nb, pad, pool_ref.shape[2]), F32)

    rows = lax.broadcasted_iota(jnp.int32, (CHUNK, CHUNK), 0)
    cols = lax.broadcasted_iota(jnp.int32, (CHUNK, CHUNK), 1)
    ws = [jnp.where(cols <= rows, ws_ref[g], 0.0).astype(BF16) for g in range(ws_ref.shape[0])]
    x1 = [_odd_mixer_rows(x_ref[b], r * tm, pool_ref.at[b], gpre_ref, win_ref, pw_ref, ps_ref,
                          lng_ref, lnb_ref, ws, bs_ref, wout_ref, gpost_ref) for b in range(nb)]
    out = _mlp_rows(jnp.concatenate(x1, axis=0), mgpre_ref, mgpost_ref, w1_ref, w2_ref)
    for b in range(nb):
        o_ref[b] = out[b * tm:(b + 1) * tm]


def _odd_layer_call(x, *consts):
    B, L, D = x.shape
    tm = min(POST_TILE, L)
    width = consts[3].shape[1]
    tok = lambda r: (0, r, 0)
    return pl.pallas_call(
        _odd_layer_kernel,
        grid=(L // tm,),
        in_specs=[pl.BlockSpec((B, tm, D), tok)] + [_const_spec(c.shape) for c in consts],
        out_specs=pl.BlockSpec((B, tm, D), tok),
        out_shape=jax.ShapeDtypeStruct((B, L, D), F32),
        scratch_shapes=[pltpu.VMEM((B, tm + max(POOL_WINDOWS), width), F32)],
        compiler_params=_params("arbitrary"),
        name="odd_layer",
    )(x, *consts)


def _even_layer(x, g_pre, g_post, w_in, lam_re, lam_im, log_dt, b_re, b_im, c_re, c_im, d, w_glu,
                b_f, w_out, mlp_g_pre, mlp_g_post, w1, w2):
    B, L, D = x.shape
    n_heads = b_f.shape[0]
    s5w = d.shape[0]
    fw = n_heads * HEAD_DIM
    qs = HEAD_DIM ** -0.5 * LOG2E
    w_u = w_in[:, :s5w].reshape(D, 2, 2, LANES).transpose(0, 2, 1, 3).reshape(D, s5w)
    w = jnp.concatenate([
        w_u, w_in[:, s5w:s5w + fw] * qs, w_in[:, s5w + fw:s5w + 3 * fw],
        jnp.pad(w_in[:, s5w + 3 * fw:], ((0, 0), (0, LANES - n_heads)))], axis=1).astype(BF16)
    u0, u1, q, k, v, fl = _even_inproj(x, g_pre[None], w)

    fq, fk = _fox_prep(fl, b_f)
    y_b = _fox_attn(q, k, v, fq, fk)

    ab_re, ab_im, bb_re, bb_im = _s5_discretize(lam_re, lam_im, log_dt, b_re, b_im)
    bst, cst, a_re, a_im, d2 = _s5_matrices(ab_re, ab_im, bb_re, bb_im, c_re, c_im, d, B)
    g0, g1 = _s5_scan(u0, u1, bst, cst, a_re, a_im, d2)

    return _even_post(x, g0, g1, y_b, w_glu.astype(BF16), w_out.astype(BF16), g_post[None],
                      mlp_g_pre[None], mlp_g_post[None], w1.astype(BF16), w2.astype(BF16))


def _odd_layer(x, g_pre, g_post, w_in, pool_w, pool_scale, ln_g, ln_b, w_s, b_s, w_out,
               mlp_g_pre, mlp_g_post, w1, w2):
    bs_full = jnp.repeat(jnp.transpose(b_s), LANES, axis=1)
    return _odd_layer_call(x, g_pre[None], w_in.astype(BF16), pool_w.astype(BF16), pool_scale[None],
                           ln_g[None], ln_b[None], w_s, bs_full, w_out.astype(BF16), g_post[None],
                           mlp_g_pre[None], mlp_g_post[None], w1.astype(BF16), w2.astype(BF16))


def kernel(x, mix_pre_g, mix_post_g, mlp_pre_g, mlp_post_g, w_in_even, s5_lam_re, s5_lam_im, s5_log_dt, s5_b_re, s5_b_im, s5_c_re, s5_c_im, s5_d, s5_w_glu, fox_b_f, w_out_even, w_in_odd, pool_w, pool_scale, sgu_ln_g, sgu_ln_b, sgu_w_s, sgu_b_s, w_out_odd, mlp_w1, mlp_w2):
    B, L, D = x.shape
    depth = mix_pre_g.shape[0]
    for l in range(depth):
        i = l // 2
        if l % 2 == 0:
            x = _even_layer(x, mix_pre_g[l], mix_post_g[l], w_in_even[i], s5_lam_re[i], s5_lam_im[i],
                            s5_log_dt[i], s5_b_re[i], s5_b_im[i], s5_c_re[i], s5_c_im[i], s5_d[i],
                            s5_w_glu[i], fox_b_f[i], w_out_even[i],
                            mlp_pre_g[l], mlp_post_g[l], mlp_w1[l], mlp_w2[l])
        else:
            x = _odd_layer(x, mix_pre_g[l], mix_post_g[l], w_in_odd[i], pool_w[i], pool_scale[i],
                           sgu_ln_g[i], sgu_ln_b[i], sgu_w_s[i], sgu_b_s[i], w_out_odd[i],
                           mlp_pre_g[l], mlp_post_g[l], mlp_w1[l], mlp_w2[l])
    return x
```

```python
import functools

import numpy as np
import jax
import jax.numpy as jnp
from jax import lax
from jax.experimental import pallas as pl
from jax.experimental.pallas import tpu as pltpu

F32 = jnp.float32
BF16 = jnp.bfloat16

EPS = 1e-6
LOG2E = 1.4426950408889634
NEG_BIG = -1e30

LANES = 128
SUBLANES = 8
VMEM_LIMIT = 56 * 1024 * 1024

S5_GROUP = 16
S5_STATE = 64
HEAD_DIM = 64
POOL_WINDOWS = (2, 4, 8, 16)
CHUNK = 128

ROW_TILE = 512
POST_TILE = 256
MLP_FF_CHUNK = 1024
S5_TILE = 128
ATT_TQ = 1024
ATT_TK = 512
ATT_PAIRS = 2


def _params(*sem):
    return pltpu.CompilerParams(dimension_semantics=sem, vmem_limit_bytes=VMEM_LIMIT)


def _const_spec(shape):
    nd = len(shape)
    return pl.BlockSpec(shape, lambda *_: (0,) * nd, pipeline_mode=pl.Buffered(1))


def _layer_spec(shape, layer):
    nd = len(shape)
    return pl.BlockSpec((None,) + tuple(shape[1:]), lambda *_: (layer,) + (0,) * (nd - 1),
                        pipeline_mode=pl.Buffered(1))


def _rms(x, g):
    return x * lax.rsqrt(jnp.mean(x * x, axis=-1, keepdims=True) + EPS) * g


def _split3(x):
    hi = x.astype(BF16)
    r1 = x - hi.astype(F32)
    mid = r1.astype(BF16)
    lo = (r1 - mid.astype(F32)).astype(BF16)
    return hi, mid, lo


def _dot(a, b):
    return jnp.dot(a, b, preferred_element_type=F32)


def _even_inproj_kernel(x_ref, g_ref, w_ref, wf_ref, u0_ref, u1_ref, q_ref, k_ref, v_ref, f_ref):
    nb, tm, _ = x_ref.shape
    s5w = 4 * LANES
    fw = q_ref.shape[2]
    q_scale = HEAD_DIM ** -0.5 * LOG2E
    for b in range(nb):
        h = _rms(x_ref[b], g_ref[...]).astype(BF16)
        zu = _dot(h, w_ref[:, 0:s5w])
        for half in range(2):
            rows = pl.ds(2 * b + half, tm, stride=2 * nb)
            u0_ref[rows, :] = zu[:, 2 * half * LANES:(2 * half + 1) * LANES]
            u1_ref[rows, :] = zu[:, (2 * half + 1) * LANES:(2 * half + 2) * LANES]
        q_ref[b] = (_dot(h, w_ref[:, s5w:s5w + fw]) * q_scale).astype(BF16)
        k_ref[b] = _dot(h, w_ref[:, s5w + fw:s5w + 2 * fw]).astype(BF16)
        v_ref[b] = _dot(h, w_ref[:, s5w + 2 * fw:s5w + 3 * fw]).astype(BF16)
        f_ref[b] = _dot(h, wf_ref[...])


def _even_inproj(x, g, w, wf):
    B, L, D = x.shape
    tm = min(ROW_TILE, L)
    tok = lambda r: (0, r, 0)
    ublk = pl.BlockSpec((tm * 2 * B, LANES), lambda r: (r, 0))
    return pl.pallas_call(
        _even_inproj_kernel,
        grid=(L // tm,),
        in_specs=[pl.BlockSpec((B, tm, D), tok), _const_spec((1, D)), _const_spec(w.shape),
                  _const_spec(wf.shape)],
        out_specs=[
            ublk, ublk,
            pl.BlockSpec((B, tm, 512), tok),
            pl.BlockSpec((B, tm, 512), tok),
            pl.BlockSpec((B, tm, 512), tok),
            pl.BlockSpec((B, tm, LANES), tok),
        ],
        out_shape=[
            jax.ShapeDtypeStruct((L * 2 * B, LANES), F32),
            jax.ShapeDtypeStruct((L * 2 * B, LANES), F32),
            jax.ShapeDtypeStruct((B, L, 512), BF16),
            jax.ShapeDtypeStruct((B, L, 512), BF16),
            jax.ShapeDtypeStruct((B, L, 512), BF16),
            jax.ShapeDtypeStruct((B, L, LANES), F32),
        ],
        compiler_params=_params("parallel"),
        name="even_inproj",
    )(x, g, w, wf)


def _fox_prep_kernel(f_ref, bf_ref, tri_ref, place_ref, feat_ref, carry_ref):
    @pl.when(pl.program_id(1) == 0)
    def _():
        carry_ref[...] = jnp.zeros_like(carry_ref)

    z = f_ref[0] + bf_ref[...]
    logf = jnp.minimum(z, 0.0) - jnp.log1p(jnp.exp(-jnp.abs(z)))
    c3 = _dot(tri_ref[...], jnp.concatenate(_split3(logf), axis=1))
    csum = c3[:, 0:LANES] + c3[:, LANES:2 * LANES] + c3[:, 2 * LANES:3 * LANES] + carry_ref[...]
    tm = csum.shape[0]
    carry_ref[...] = csum[tm - 1:tm, :]
    pieces = jnp.concatenate(_split3(csum * LOG2E), axis=1)
    feat_ref[0] = _dot(pieces, place_ref[...]).astype(BF16)


def _fox_placement(n_heads):
    width = (n_heads // 2) * LANES
    place = np.zeros((3 * LANES, width), np.float32)
    ones = np.zeros((2, LANES), np.float32)
    for h in range(n_heads):
        base = HEAD_DIM if h % 2 == 0 else 0
        for j in range(3):
            place[j * LANES + h, (h // 2) * LANES + base + j] = 1.0
            place[j * LANES + h, (h // 2) * LANES + base + 3 + j] = -1.0
            ones[0, base + 3 + j] = 1.0
            ones[1, base + j] = 1.0
    return place, ones


def _fox_prep(fl, b_f):
    B, L, _ = fl.shape
    n_heads = b_f.shape[0]
    width = (n_heads // 2) * LANES
    tm = min(ROW_TILE, L)
    place, ones = _fox_placement(n_heads)
    tri = jnp.asarray(np.tril(np.ones((tm, tm), np.float32)), BF16)
    bf = jnp.zeros((1, LANES), F32).at[0, :n_heads].set(b_f)
    tok = lambda b, r: (b, r, 0)
    feat = pl.pallas_call(
        _fox_prep_kernel,
        grid=(B, L // tm),
        in_specs=[pl.BlockSpec((1, tm, LANES), tok), _const_spec((1, LANES)),
                  _const_spec((tm, tm)), _const_spec(place.shape)],
        out_specs=pl.BlockSpec((1, tm, width), tok),
        out_shape=jax.ShapeDtypeStruct((B, L, width), BF16),
        scratch_shapes=[pltpu.VMEM((1, LANES), F32)],
        compiler_params=_params("parallel", "arbitrary"),
        name="fox_prep",
    )(fl, bf, tri, jnp.asarray(place, BF16))
    return feat, jnp.asarray(ones)


def _fox_attn_kernel(q_ref, k_ref, v_ref, fq_ref, fk_ref, ones_ref, o_ref,
                     kaug_ref, vaug_ref, qaug_ref, m_ref, acc_ref):
    qi = pl.program_id(2)
    tq = q_ref.shape[1]
    L = k_ref.shape[1]
    tk = min(ATT_TK, L)
    n_heads = kaug_ref.shape[0]
    assert tq % tk == 0

    def merge(lanes, first, x_ref, f):
        x = x_ref[0, :, first * LANES:(first + 1) * LANES].astype(F32)
        return (jnp.where(lanes < HEAD_DIM, x, f(0)).astype(BF16),
                jnp.where(lanes >= HEAD_DIM, x, f(1)).astype(BF16))

    def features(f_ref, pr, side):
        one = ones_ref[side:side + 1, :]
        return f_ref[0, :, pr * LANES:(pr + 1) * LANES].astype(F32) * (1.0 - one) + one

    @pl.when(qi == 0)
    def _():
        lanes = lax.broadcasted_iota(jnp.int32, (L, LANES), 1)
        for pr in range(n_heads // 2):
            fk = features(fk_ref, pr, 1)
            kaug_ref[2 * pr], kaug_ref[2 * pr + 1] = merge(lanes, pr, k_ref, lambda e: fk)
            ones = ((lanes == HEAD_DIM).astype(F32), (lanes == 0).astype(F32))
            vaug_ref[2 * pr], vaug_ref[2 * pr + 1] = merge(lanes, pr, v_ref, lambda e: ones[e])

    lanes_q = lax.broadcasted_iota(jnp.int32, (tq, LANES), 1)
    for pr in range(n_heads // 2):
        fq = features(fq_ref, pr, 0)
        qaug_ref[2 * pr], qaug_ref[2 * pr + 1] = merge(lanes_q, pr, q_ref, lambda e: fq)

    m_ref[...] = jnp.full(m_ref.shape, NEG_BIG, F32)
    acc_ref[...] = jnp.zeros(acc_ref.shape, F32)

    def tile(j, r0, r1, masked):
        start = pl.multiple_of(j * tk, tk)
        for e in range(n_heads):
            kt = kaug_ref[e, pl.ds(start, tk), :]
            s = lax.dot_general(qaug_ref[e, r0:r1, :], kt, (((1,), (1,)), ((), ())),
                                preferred_element_type=F32)
            if masked:
                row = lax.broadcasted_iota(jnp.int32, s.shape, 0)
                col = lax.broadcasted_iota(jnp.int32, s.shape, 1)
                s = jnp.where(col <= row, s, NEG_BIG)
            m = m_ref[e, r0:r1, :]
            m_new = jnp.maximum(m, jnp.max(s, axis=1, keepdims=True))
            alpha = jnp.exp2(m - m_new)
            p = jnp.exp2(s - jnp.concatenate([m_new] * (tk // LANES), axis=1)).astype(BF16)
            acc_ref[e, r0:r1, :] = alpha * acc_ref[e, r0:r1, :] + _dot(p, vaug_ref[e, pl.ds(start, tk), :])
            m_ref[e, r0:r1, :] = m_new

    n_diag = tq // tk

    def body(j, c):
        tile(j, 0, tq, False)
        return c

    lax.fori_loop(0, qi * n_diag, body, 0)
    for d in range(n_diag):
        tile(qi * n_diag + d, d * tk, (d + 1) * tk, True)
        if (d + 1) * tk < tq:
            tile(qi * n_diag + d, (d + 1) * tk, tq, False)

    for pr in range(n_heads // 2):
        acc0, acc1 = acc_ref[2 * pr], acc_ref[2 * pr + 1]
        out0 = acc0 / acc0[:, HEAD_DIM:HEAD_DIM + 1]
        out1 = acc1 / acc1[:, 0:1]
        o_ref[0, :, pr * LANES:(pr + 1) * LANES] = jnp.where(lanes_q < HEAD_DIM, out0, out1).astype(BF16)


def _fox_attn(q, k, v, feat, ones):
    B, L, W = q.shape
    bw = ATT_PAIRS * LANES
    n_heads = 2 * ATT_PAIRS
    tq = min(ATT_TQ, L)
    qspec = pl.BlockSpec((1, tq, bw), lambda b, p, i: (b, i, p))
    kspec = pl.BlockSpec((1, L, bw), lambda b, p, i: (b, 0, p))
    return pl.pallas_call(
        _fox_attn_kernel,
        grid=(B, W // bw, L // tq),
        in_specs=[qspec, kspec, kspec, qspec, kspec, _const_spec(ones.shape)],
        out_specs=qspec,
        out_shape=jax.ShapeDtypeStruct((B, L, W), BF16),
        scratch_shapes=[pltpu.VMEM((n_heads, L, LANES), BF16), pltpu.VMEM((n_heads, L, LANES), BF16),
                        pltpu.VMEM((n_heads, tq, LANES), BF16), pltpu.VMEM((n_heads, tq, LANES), F32),
                        pltpu.VMEM((n_heads, tq, LANES), F32)],
        compiler_params=_params("parallel", "parallel", "arbitrary"),
        name="fox_attn",
    )(q, k, v, feat, feat, ones)


def _s5_kernel(u0_ref, u1_ref, bst_ref, cst_ref, are_ref, aim_ref, d_ref, o0_ref, o1_ref,
               x_ref, st_ref):
    rows = u0_ref.shape[0]
    nb = x_ref.shape[0] // 2
    hrows = rows // 2

    @pl.when(pl.program_id(0) == 0)
    def _():
        st_ref[...] = jnp.zeros_like(st_ref)

    def half_rows(ref, h, *lead):
        return ref[(*lead, pl.ds(h, hrows, stride=2), slice(None))]

    u_h = []
    for h in range(2):
        u = jnp.concatenate([half_rows(u0_ref, h), half_rows(u1_ref, h)], axis=1)
        u_h.append(u)
        bu = _dot(u.astype(BF16), bst_ref[h])
        for c in range(2 * nb):
            x_ref[c, pl.ds(h, hrows, stride=2), :] = bu[:, c * LANES:(c + 1) * LANES]

    a_re = [are_ref[:, c * LANES:(c + 1) * LANES] for c in range(nb)]
    a_im = [aim_ref[:, c * LANES:(c + 1) * LANES] for c in range(nb)]

    def step(i, state):
        r = pl.ds(pl.multiple_of(i * SUBLANES, SUBLANES), SUBLANES)
        new = []
        for c in range(nb):
            s_re, s_im = state[2 * c], state[2 * c + 1]
            n_re = a_re[c] * s_re - a_im[c] * s_im + x_ref[c, r, :]
            n_im = a_re[c] * s_im + a_im[c] * s_re + x_ref[nb + c, r, :]
            x_ref[c, r, :] = n_re
            x_ref[nb + c, r, :] = n_im
            new += [n_re, n_im]
        return tuple(new)

    state = lax.fori_loop(0, rows // SUBLANES, step,
                          tuple(st_ref[c] for c in range(2 * nb)), unroll=4)
    for c in range(2 * nb):
        st_ref[c] = state[c]

    for h in range(2):
        xs = jnp.concatenate([half_rows(x_ref, h, c) for c in range(nb)]
                             + [half_rows(x_ref, h, nb + c) for c in range(nb)], axis=1)
        y = _dot(xs.astype(BF16), cst_ref[h])
        g = jax.nn.gelu(y + d_ref[h:h + 1, :] * u_h[h])
        o0_ref[pl.ds(h, hrows, stride=2), :] = g[:, 0:LANES]
        o1_ref[pl.ds(h, hrows, stride=2), :] = g[:, LANES:2 * LANES]


def _s5_discretize(lam_re, lam_im, log_dt, b_re, b_im):
    dt = jnp.exp(log_dt)[:, None]
    mag = jnp.exp(lam_re * dt)
    ab_re = mag * jnp.cos(lam_im * dt)
    ab_im = mag * jnp.sin(lam_im * dt)
    den = lam_re * lam_re + lam_im * lam_im
    nr = ab_re - 1.0
    ni = ab_im
    q_re = (nr * lam_re + ni * lam_im) / den
    q_im = (ni * lam_re - nr * lam_im) / den
    bb_re = q_re[..., None] * b_re - q_im[..., None] * b_im
    bb_im = q_re[..., None] * b_im + q_im[..., None] * b_re
    return ab_re, ab_im, bb_re, bb_im


def _s5_matrices(ab_re, ab_im, bb_re, bb_im, c_re, c_im, d, batch):
    G, P, H = bb_re.shape
    gh = G // 2
    eye = jnp.eye(gh, dtype=F32)

    def bmat(bb):
        t = jnp.einsum('zgph,gk->zghkp', bb.reshape(2, gh, P, H), eye)
        return t.reshape(2 * gh * H, gh * P)

    def cmat(cc):
        t = jnp.einsum('zghp,gk->kpzgh', cc.reshape(2, gh, H, P), eye)
        return t.reshape(gh * P, 2 * gh * H)

    hw, ns = gh * H, gh * P
    bst = jnp.concatenate([bmat(bb_re), bmat(bb_im)], axis=1).reshape(2, hw, 2 * ns).astype(BF16)
    cst = jnp.concatenate([cmat(c_re), -cmat(c_im)], axis=0).reshape(2 * ns, 2, hw)
    cst = jnp.transpose(cst, (1, 0, 2)).astype(BF16)
    a_re = jnp.tile(ab_re.reshape(2, ns), (batch, 1))
    a_im = jnp.tile(ab_im.reshape(2, ns), (batch, 1))
    return bst, cst, a_re, a_im, d.reshape(2, hw)


def _s5_scan(u0, u1, bst, cst, a_re, a_im, d2):
    n_rows = u0.shape[0]
    ns = a_re.shape[1]
    rows = min(S5_TILE * SUBLANES, n_rows)
    blk = pl.BlockSpec((rows, LANES), lambda t: (t, 0))
    return pl.pallas_call(
        _s5_kernel,
        grid=(n_rows // rows,),
        in_specs=[blk, blk, _const_spec(bst.shape), _const_spec(cst.shape),
                  _const_spec(a_re.shape), _const_spec(a_im.shape), _const_spec(d2.shape)],
        out_specs=[blk, blk],
        out_shape=[jax.ShapeDtypeStruct((n_rows, LANES), F32)] * 2,
        scratch_shapes=[pltpu.VMEM((2 * ns // LANES, rows, LANES), F32),
                        pltpu.VMEM((2 * ns // LANES, SUBLANES, LANES), F32)],
        compiler_params=_params("arbitrary"),
        name="s5_scan",
    )(u0, u1, bst, cst, a_re, a_im, d2)


def _mlp_rows(x, gpre_ref, gpost_ref, w1_ref, w2_ref):
    h = _rms(x, gpre_ref[...]).astype(BF16)
    d_ff = w1_ref.shape[1]
    acc = jnp.zeros(x.shape, F32)
    for f in range(0, d_ff, MLP_FF_CHUNK):
        a = jnp.maximum(_dot(h, w1_ref[:, f:f + MLP_FF_CHUNK]), 0.0)
        acc = acc + _dot((a * a).astype(BF16), w2_ref[f:f + MLP_FF_CHUNK, :])
    return x + _rms(acc, gpost_ref[...])


def _even_post_kernel(x_ref, g0_ref, g1_ref, yb_ref, wglu_ref, wout_ref, g_ref,
                      gpre_ref, gpost_ref, w1_ref, w2_ref, o_ref):
    nb, tm, _ = x_ref.shape
    half_w = wout_ref.shape[0] // 2
    x1 = []
    for b in range(nb):
        blocks = [ref[pl.ds(2 * b + half, tm, stride=2 * nb), :]
                  for half in range(2) for ref in (g0_ref, g1_ref)]
        ga = jnp.concatenate(blocks, axis=1)
        ya = ga * jax.nn.sigmoid(_dot(ga.astype(BF16), wglu_ref[...]))
        mix = (_dot(ya.astype(BF16), wout_ref[0:half_w, :])
               + _dot(yb_ref[b], wout_ref[half_w:2 * half_w, :]))
        x1.append(x_ref[b] + _rms(mix, g_ref[...]))
    out = _mlp_rows(jnp.concatenate(x1, axis=0), gpre_ref, gpost_ref, w1_ref, w2_ref)
    for b in range(nb):
        o_ref[b] = out[b * tm:(b + 1) * tm]


def _even_post(x, g0, g1, yb, w_glu, w_out, g, g_pre, g_post, w1s, w2s, layer):
    B, L, D = x.shape
    tm = min(POST_TILE, L)
    tok = lambda r: (0, r, 0)
    gblk = pl.BlockSpec((tm * 2 * B, LANES), lambda r: (r, 0))
    consts = (w_glu, w_out, g, g_pre, g_post)
    return pl.pallas_call(
        _even_post_kernel,
        grid=(L // tm,),
        in_specs=[pl.BlockSpec((B, tm, D), tok), gblk, gblk, pl.BlockSpec((B, tm, 512), tok)]
                 + [_const_spec(c.shape) for c in consts]
                 + [_layer_spec(w1s.shape, layer), _layer_spec(w2s.shape, layer)],
        out_specs=pl.BlockSpec((B, tm, D), tok),
        out_shape=jax.ShapeDtypeStruct((B, L, D), F32),
        compiler_params=_params("parallel"),
        name="even_post_mlp",
    )(x, g0, g1, yb, *consts, w1s, w2s)


def _odd_mixer_rows(x, t0, pool_ref, gpre_ref, win_ref, pw_ref, ps_ref, lng_ref, lnb_ref, ws,
                    bs_ref, wout_ref, gpost_ref):
    tm = x.shape[0]
    pad = max(POOL_WINDOWS)
    gd = LANES
    n_grp = len(POOL_WINDOWS)
    width = n_grp * gd

    h = _rms(x, gpre_ref[...]).astype(BF16)
    xc = _dot(h, win_ref[:, 0:width])
    u = _dot(h, win_ref[:, width:2 * width])
    v = _dot(h, win_ref[:, 2 * width:3 * width])

    pool_ref[pad:pad + tm, :] = xc
    t = t0 + lax.broadcasted_iota(jnp.int32, (tm, 1), 0)
    yc = []
    for g, w in enumerate(POOL_WINDOWS):
        cols = slice(g * gd, (g + 1) * gd)
        tot = xc[:, cols]
        for lag in range(1, w):
            tot = tot + pool_ref[pl.ds(pad - lag, tm), cols]
        cnt = jnp.minimum(t + 1, w).astype(F32)
        pooled = tot / cnt - xc[:, cols]
        yc.append(_dot(pooled.astype(BF16), pw_ref[g]))
    pool_ref[0:pad, :] = pool_ref[tm:tm + pad, :]
    y_c = jnp.concatenate(yc, axis=-1) * ps_ref[...]

    gv = jax.nn.gelu(v)
    mu = jnp.mean(gv, axis=-1, keepdims=True)
    vc = gv - mu
    vn = vc * lax.rsqrt(jnp.mean(vc * vc, axis=-1, keepdims=True) + EPS)
    vn = (vn * lng_ref[...] + lnb_ref[...]).astype(BF16)
    chunks = []
    for n in range(tm // CHUNK):
        parts = [_dot(ws[g], vn[n * CHUNK:(n + 1) * CHUNK, g * gd:(g + 1) * gd]) for g in range(n_grp)]
        chunks.append(jnp.concatenate(parts, axis=-1) + bs_ref[...])
    y_d = jax.nn.gelu(u) * jnp.concatenate(chunks, axis=0)

    mix = _dot(y_c.astype(BF16), wout_ref[0:width, :]) + _dot(y_d.astype(BF16), wout_ref[width:2 * width, :])
    return x + _rms(mix, gpost_ref[...])


def _odd_layer_kernel(x_ref, gpre_ref, win_ref, pw_ref, ps_ref, lng_ref, lnb_ref, ws_ref,
                      bs_ref, wout_ref, gpost_ref, mgpre_ref, mgpost_ref, w1_ref, w2_ref,
                      o_ref, pool_ref):
    r = pl.program_id(0)
    nb, tm, _ = x_ref.shape
    pad = max(POOL_WINDOWS)

    @pl.when(r == 0)
    def _():
        pool_ref[:, 0:pad, :] = jnp.zeros((nb, pad, pool_ref.shape[2]), F32)

    rows = lax.broadcasted_iota(jnp.int32, (CHUNK, CHUNK), 0)
    cols = lax.broadcasted_iota(jnp.int32, (CHUNK, CHUNK), 1)
    ws = [jnp.where(cols <= rows, ws_ref[g], 0.0).astype(BF16) for g in range(ws_ref.shape[0])]
    x1 = [_odd_mixer_rows(x_ref[b], r * tm, pool_ref.at[b], gpre_ref, win_ref, pw_ref, ps_ref,
                          lng_ref, lnb_ref, ws, bs_ref, wout_ref, gpost_ref) for b in range(nb)]
    out = _mlp_rows(jnp.concatenate(x1, axis=0), mgpre_ref, mgpost_ref, w1_ref, w2_ref)
    for b in range(nb):
        o_ref[b] = out[b * tm:(b + 1) * tm]


def _odd_layer_call(x, consts, w1s, w2s, layer):
    B, L, D = x.shape
    tm = min(POST_TILE, L)
    width = consts[3].shape[1]
    tok = lambda r: (0, r, 0)
    return pl.pallas_call(
        _odd_layer_kernel,
        grid=(L // tm,),
        in_specs=[pl.BlockSpec((B, tm, D), tok)] + [_const_spec(c.shape) for c in consts]
                 + [_layer_spec(w1s.shape, layer), _layer_spec(w2s.shape, layer)],
        out_specs=pl.BlockSpec((B, tm, D), tok),
        out_shape=jax.ShapeDtypeStruct((B, L, D), F32),
        scratch_shapes=[pltpu.VMEM((B, tm + max(POOL_WINDOWS), width), F32)],
        compiler_params=_params("arbitrary"),
        name="odd_layer",
    )(x, *consts, w1s, w2s)


def _even_layer(x, g_pre, g_post, w_in, lam_re, lam_im, log_dt, b_re, b_im, c_re, c_im, d, w_glu,
                b_f, w_out, mlp_g_pre, mlp_g_post, w1s, w2s, layer):
    B, L, D = x.shape
    n_heads = b_f.shape[0]
    s5w = d.shape[0]
    fw = n_heads * HEAD_DIM
    assert s5w == 4 * LANES and w_in.shape[1] == s5w + 3 * fw + n_heads
    wf = jnp.pad(w_in[:, s5w + 3 * fw:], ((0, 0), (0, LANES - n_heads))).astype(BF16)
    u0, u1, q, k, v, fl = _even_inproj(x, g_pre[None], w_in.astype(BF16), wf)

    feat, ones = _fox_prep(fl, b_f)
    y_b = _fox_attn(q, k, v, feat, ones)

    ab_re, ab_im, bb_re, bb_im = _s5_discretize(lam_re, lam_im, log_dt, b_re, b_im)
    bst, cst, a_re, a_im, d2 = _s5_matrices(ab_re, ab_im, bb_re, bb_im, c_re, c_im, d, B)
    g0, g1 = _s5_scan(u0, u1, bst, cst, a_re, a_im, d2)

    return _even_post(x, g0, g1, y_b, w_glu.astype(BF16), w_out.astype(BF16), g_post[None],
                      mlp_g_pre[None], mlp_g_post[None], w1s, w2s, layer)


def _odd_layer(x, g_pre, g_post, w_in, pool_w, pool_scale, ln_g, ln_b, w_s, b_s, w_out,
               mlp_g_pre, mlp_g_post, w1s, w2s, layer):
    bs_full = jnp.repeat(jnp.transpose(b_s), LANES, axis=1)
    consts = (g_pre[None], w_in.astype(BF16), pool_w.astype(BF16), pool_scale[None], ln_g[None],
              ln_b[None], w_s, bs_full, w_out.astype(BF16), g_post[None], mlp_g_pre[None],
              mlp_g_post[None])
    return _odd_layer_call(x, consts, w1s, w2s, layer)


def kernel(x, mix_pre_g, mix_post_g, mlp_pre_g, mlp_post_g, w_in_even, s5_lam_re, s5_lam_im, s5_log_dt, s5_b_re, s5_b_im, s5_c_re, s5_c_im, s5_d, s5_w_glu, fox_b_f, w_out_even, w_in_odd, pool_w, pool_scale, sgu_ln_g, sgu_ln_b, sgu_w_s, sgu_b_s, w_out_odd, mlp_w1, mlp_w2):
    depth = mix_pre_g.shape[0]
    w1s, w2s = mlp_w1.astype(BF16), mlp_w2.astype(BF16)
    for l in range(depth):
        i = l // 2
        if l % 2 == 0:
            x = _even_layer(x, mix_pre_g[l], mix_post_g[l], w_in_even[i], s5_lam_re[i], s5_lam_im[i],
                            s5_log_dt[i], s5_b_re[i], s5_b_im[i], s5_c_re[i], s5_c_im[i], s5_d[i],
                            s5_w_glu[i], fox_b_f[i], w_out_even[i],
                            mlp_pre_g[l], mlp_post_g[l], w1s, w2s, l)
        else:
            x = _odd_layer(x, mix_pre_g[l], mix_post_g[l], w_in_odd[i], pool_w[i], pool_scale[i],
                           sgu_ln_g[i], sgu_ln_b[i], sgu_w_s[i], sgu_b_s[i], w_out_odd[i],
                           mlp_pre_g[l], mlp_post_g[l], w1s, w2s, l)
    return x
```

```python
import functools

import numpy as np
import jax
import jax.numpy as jnp
from jax import lax
from jax.experimental import pallas as pl
from jax.experimental.pallas import tpu as pltpu

F32 = jnp.float32
BF16 = jnp.bfloat16

EPS = 1e-6
LOG2E = 1.4426950408889634
NEG_BIG = -1e30

LANES = 128
SUBLANES = 8
VMEM_LIMIT = 56 * 1024 * 1024

S5_GROUP = 16
S5_STATE = 64
HEAD_DIM = 64
POOL_WINDOWS = (2, 4, 8, 16)
CHUNK = 128

ROW_TILE = 512
POST_TILE = 256
MLP_FF_CHUNK = 1024
S5_TILE = 128
ATT_TQ = 1024
ATT_TK = 512
ATT_PAIRS = 2


def _params(*sem):
    return pltpu.CompilerParams(dimension_semantics=sem, vmem_limit_bytes=VMEM_LIMIT)


def _const_spec(shape):
    nd = len(shape)
    return pl.BlockSpec(shape, lambda *_: (0,) * nd, pipeline_mode=pl.Buffered(1))


def _layer_spec(shape, layer):
    nd = len(shape)
    return pl.BlockSpec((None,) + tuple(shape[1:]), lambda *_: (layer,) + (0,) * (nd - 1),
                        pipeline_mode=pl.Buffered(1))


def _rms(x, g):
    return x * lax.rsqrt(jnp.mean(x * x, axis=-1, keepdims=True) + EPS) * g


def _split3(x):
    hi = x.astype(BF16)
    r1 = x - hi.astype(F32)
    mid = r1.astype(BF16)
    lo = (r1 - mid.astype(F32)).astype(BF16)
    return hi, mid, lo


def _dot(a, b):
    return jnp.dot(a, b, preferred_element_type=F32)


def _even_inproj_kernel(x_ref, g_ref, w_ref, wf_ref, u0_ref, u1_ref, q_ref, k_ref, v_ref, f_ref):
    nb, tm, _ = x_ref.shape
    s5w = 4 * LANES
    fw = q_ref.shape[2]
    q_scale = HEAD_DIM ** -0.5 * LOG2E
    for b in range(nb):
        h = _rms(x_ref[b], g_ref[...]).astype(BF16)
        zu = _dot(h, w_ref[:, 0:s5w])
        for half in range(2):
            rows = pl.ds(2 * b + half, tm, stride=2 * nb)
            u0_ref[rows, :] = zu[:, 2 * half * LANES:(2 * half + 1) * LANES]
            u1_ref[rows, :] = zu[:, (2 * half + 1) * LANES:(2 * half + 2) * LANES]
        q_ref[b] = (_dot(h, w_ref[:, s5w:s5w + fw]) * q_scale).astype(BF16)
        k_ref[b] = _dot(h, w_ref[:, s5w + fw:s5w + 2 * fw]).astype(BF16)
        v_ref[b] = _dot(h, w_ref[:, s5w + 2 * fw:s5w + 3 * fw]).astype(BF16)
        f_ref[b] = _dot(h, wf_ref[...])


def _even_inproj(x, g, w, wf):
    B, L, D = x.shape
    tm = min(ROW_TILE, L)
    tok = lambda r: (0, r, 0)
    ublk = pl.BlockSpec((tm * 2 * B, LANES), lambda r: (r, 0))
    return pl.pallas_call(
        _even_inproj_kernel,
        grid=(L // tm,),
        in_specs=[pl.BlockSpec((B, tm, D), tok), _const_spec((1, D)), _const_spec(w.shape),
                  _const_spec(wf.shape)],
        out_specs=[
            ublk, ublk,
            pl.BlockSpec((B, tm, 512), tok),
            pl.BlockSpec((B, tm, 512), tok),
            pl.BlockSpec((B, tm, 512), tok),
            pl.BlockSpec((B, tm, LANES), tok),
        ],
        out_shape=[
            jax.ShapeDtypeStruct((L * 2 * B, LANES), F32),
            jax.ShapeDtypeStruct((L * 2 * B, LANES), F32),
            jax.ShapeDtypeStruct((B, L, 512), BF16),
            jax.ShapeDtypeStruct((B, L, 512), BF16),
            jax.ShapeDtypeStruct((B, L, 512), BF16),
            jax.ShapeDtypeStruct((B, L, LANES), F32),
        ],
        compiler_params=_params("parallel"),
        name="even_inproj",
    )(x, g, w, wf)


def _fox_prep_kernel(f_ref, bf_ref, tri_ref, place_ref, feat_ref, carry_ref):
    @pl.when(pl.program_id(1) == 0)
    def _():
        carry_ref[...] = jnp.zeros_like(carry_ref)

    z = f_ref[0] + bf_ref[...]
    logf = jnp.minimum(z, 0.0) - jnp.log1p(jnp.exp(-jnp.abs(z)))
    c3 = _dot(tri_ref[...], jnp.concatenate(_split3(logf), axis=1))
    csum = c3[:, 0:LANES] + c3[:, LANES:2 * LANES] + c3[:, 2 * LANES:3 * LANES] + carry_ref[...]
    tm = csum.shape[0]
    carry_ref[...] = csum[tm - 1:tm, :]
    pieces = jnp.concatenate(_split3(csum * LOG2E), axis=1)
    feat_ref[0] = _dot(pieces, place_ref[...]).astype(BF16)


def _fox_placement(n_heads):
    width = (n_heads // 2) * LANES
    place = np.zeros((3 * LANES, width), np.float32)
    ones = np.zeros((2, LANES), np.float32)
    for h in range(n_heads):
        base = HEAD_DIM if h % 2 == 0 else 0
        for j in range(3):
            place[j * LANES + h, (h // 2) * LANES + base + j] = 1.0
            place[j * LANES + h, (h // 2) * LANES + base + 3 + j] = -1.0
            ones[0, base + 3 + j] = 1.0
            ones[1, base + j] = 1.0
    return place, ones


def _fox_prep(fl, b_f):
    B, L, _ = fl.shape
    n_heads = b_f.shape[0]
    width = (n_heads // 2) * LANES
    tm = min(ROW_TILE, L)
    place, ones = _fox_placement(n_heads)
    tri = jnp.asarray(np.tril(np.ones((tm, tm), np.float32)), BF16)
    bf = jnp.zeros((1, LANES), F32).at[0, :n_heads].set(b_f)
    tok = lambda b, r: (b, r, 0)
    feat = pl.pallas_call(
        _fox_prep_kernel,
        grid=(B, L // tm),
        in_specs=[pl.BlockSpec((1, tm, LANES), tok), _const_spec((1, LANES)),
                  _const_spec((tm, tm)), _const_spec(place.shape)],
        out_specs=pl.BlockSpec((1, tm, width), tok),
        out_shape=jax.ShapeDtypeStruct((B, L, width), BF16),
        scratch_shapes=[pltpu.VMEM((1, LANES), F32)],
        compiler_params=_params("parallel", "arbitrary"),
        name="fox_prep",
    )(fl, bf, tri, jnp.asarray(place, BF16))
    return feat, jnp.asarray(ones)


def _fox_attn_kernel(q_ref, k_ref, v_ref, fq_ref, fk_ref, ones_ref, o_ref,
                     kaug_ref, vaug_ref, qaug_ref, m_ref, acc_ref):
    qi = pl.program_id(2)
    tq = q_ref.shape[1]
    L = k_ref.shape[1]
    tk = min(ATT_TK, L)
    n_heads = kaug_ref.shape[0]
    assert tq % tk == 0

    def merge(lanes, first, x_ref, f):
        x = x_ref[0, :, first * LANES:(first + 1) * LANES].astype(F32)
        return (jnp.where(lanes < HEAD_DIM, x, f(0)).astype(BF16),
                jnp.where(lanes >= HEAD_DIM, x, f(1)).astype(BF16))

    def features(f_ref, pr, side):
        one = ones_ref[side:side + 1, :]
        return f_ref[0, :, pr * LANES:(pr + 1) * LANES].astype(F32) * (1.0 - one) + one

    @pl.when(qi == 0)
    def _():
        lanes = lax.broadcasted_iota(jnp.int32, (L, LANES), 1)
        for pr in range(n_heads // 2):
            fk = features(fk_ref, pr, 1)
            kaug_ref[2 * pr], kaug_ref[2 * pr + 1] = merge(lanes, pr, k_ref, lambda e: fk)
            ones = ((lanes == HEAD_DIM).astype(F32), (lanes == 0).astype(F32))
            vaug_ref[2 * pr], vaug_ref[2 * pr + 1] = merge(lanes, pr, v_ref, lambda e: ones[e])

    lanes_q = lax.broadcasted_iota(jnp.int32, (tq, LANES), 1)
    for pr in range(n_heads // 2):
        fq = features(fq_ref, pr, 0)
        qaug_ref[2 * pr], qaug_ref[2 * pr + 1] = merge(lanes_q, pr, q_ref, lambda e: fq)

    def tile(j, r0, r1, masked, first=False):
        start = pl.multiple_of(j * tk, tk)
        for e in range(n_heads):
            kt = kaug_ref[e, pl.ds(start, tk), :]
            s = lax.dot_general(qaug_ref[e, r0:r1, :], kt, (((1,), (1,)), ((), ())),
                                preferred_element_type=F32)
            if masked:
                row = lax.broadcasted_iota(jnp.int32, s.shape, 0)
                col = lax.broadcasted_iota(jnp.int32, s.shape, 1)
                s = jnp.where(col <= row, s, NEG_BIG)
            m_new = jnp.broadcast_to(jnp.max(s, axis=1, keepdims=True), (r1 - r0, LANES))
            if not first:
                m = m_ref[e, r0:r1, :]
                m_new = jnp.maximum(m, m_new)
            p = jnp.exp2((s - jnp.concatenate([m_new] * (tk // LANES), axis=1)).astype(BF16))
            pv = _dot(p, vaug_ref[e, pl.ds(start, tk), :])
            acc_ref[e, r0:r1, :] = pv if first else jnp.exp2(m - m_new) * acc_ref[e, r0:r1, :] + pv
            m_ref[e, r0:r1, :] = m_new

    n_diag = tq // tk

    tile(qi * n_diag, 0, tk, True, first=True)
    if tk < tq:
        tile(qi * n_diag, tk, tq, False, first=True)

    def body(j, c):
        for d in range(n_diag):
            tile(j * n_diag + d, 0, tq, False)
        return c

    lax.fori_loop(0, qi, body, 0)
    for d in range(1, n_diag):
        tile(qi * n_diag + d, d * tk, (d + 1) * tk, True)
        if (d + 1) * tk < tq:
            tile(qi * n_diag + d, (d + 1) * tk, tq, False)

    for pr in range(n_heads // 2):
        acc0, acc1 = acc_ref[2 * pr], acc_ref[2 * pr + 1]
        out0 = acc0 / acc0[:, HEAD_DIM:HEAD_DIM + 1]
        out1 = acc1 / acc1[:, 0:1]
        o_ref[0, :, pr * LANES:(pr + 1) * LANES] = jnp.where(lanes_q < HEAD_DIM, out0, out1).astype(BF16)


def _fox_attn(q, k, v, feat, ones):
    B, L, W = q.shape
    bw = ATT_PAIRS * LANES
    n_heads = 2 * ATT_PAIRS
    tq = min(ATT_TQ, L)
    qspec = pl.BlockSpec((1, tq, bw), lambda b, p, i: (b, i, p))
    kspec = pl.BlockSpec((1, L, bw), lambda b, p, i: (b, 0, p))
    return pl.pallas_call(
        _fox_attn_kernel,
        grid=(B, W // bw, L // tq),
        in_specs=[qspec, kspec, kspec, qspec, kspec, _const_spec(ones.shape)],
        out_specs=qspec,
        out_shape=jax.ShapeDtypeStruct((B, L, W), BF16),
        scratch_shapes=[pltpu.VMEM((n_heads, L, LANES), BF16), pltpu.VMEM((n_heads, L, LANES), BF16),
                        pltpu.VMEM((n_heads, tq, LANES), BF16), pltpu.VMEM((n_heads, tq, LANES), F32),
                        pltpu.VMEM((n_heads, tq, LANES), F32)],
        compiler_params=_params("parallel", "parallel", "arbitrary"),
        name="fox_attn",
    )(q, k, v, feat, feat, ones)


def _s5_kernel(u0_ref, u1_ref, bst_ref, cst_ref, are_ref, aim_ref, d_ref, o0_ref, o1_ref,
               x_ref, st_ref):
    rows = u0_ref.shape[0]
    nb = x_ref.shape[0] // 2
    hrows = rows // 2

    @pl.when(pl.program_id(0) == 0)
    def _():
        st_ref[...] = jnp.zeros_like(st_ref)

    def half_rows(ref, h, *lead):
        return ref[(*lead, pl.ds(h, hrows, stride=2), slice(None))]

    u_h = []
    for h in range(2):
        u = jnp.concatenate([half_rows(u0_ref, h), half_rows(u1_ref, h)], axis=1)
        u_h.append(u)
        bu = _dot(u.astype(BF16), bst_ref[h])
        for c in range(2 * nb):
            x_ref[c, pl.ds(h, hrows, stride=2), :] = bu[:, c * LANES:(c + 1) * LANES]

    a_re = [are_ref[:, c * LANES:(c + 1) * LANES] for c in range(nb)]
    a_im = [aim_ref[:, c * LANES:(c + 1) * LANES] for c in range(nb)]

    def step(i, state):
        r = pl.ds(pl.multiple_of(i * SUBLANES, SUBLANES), SUBLANES)
        new = []
        for c in range(nb):
            s_re, s_im = state[2 * c], state[2 * c + 1]
            n_re = a_re[c] * s_re - a_im[c] * s_im + x_ref[c, r, :]
            n_im = a_re[c] * s_im + a_im[c] * s_re + x_ref[nb + c, r, :]
            x_ref[c, r, :] = n_re
            x_ref[nb + c, r, :] = n_im
            new += [n_re, n_im]
        return tuple(new)

    state = lax.fori_loop(0, rows // SUBLANES, step,
                          tuple(st_ref[c] for c in range(2 * nb)), unroll=4)
    for c in range(2 * nb):
        st_ref[c] = state[c]

    for h in range(2):
        xs = jnp.concatenate([half_rows(x_ref, h, c) for c in range(nb)]
                             + [half_rows(x_ref, h, nb + c) for c in range(nb)], axis=1)
        y = _dot(xs.astype(BF16), cst_ref[h])
        g = jax.nn.gelu(y + d_ref[h:h + 1, :] * u_h[h])
        o0_ref[pl.ds(h, hrows, stride=2), :] = g[:, 0:LANES]
        o1_ref[pl.ds(h, hrows, stride=2), :] = g[:, LANES:2 * LANES]


def _s5_discretize(lam_re, lam_im, log_dt, b_re, b_im):
    dt = jnp.exp(log_dt)[:, None]
    mag = jnp.exp(lam_re * dt)
    ab_re = mag * jnp.cos(lam_im * dt)
    ab_im = mag * jnp.sin(lam_im * dt)
    den = lam_re * lam_re + lam_im * lam_im
    nr = ab_re - 1.0
    ni = ab_im
    q_re = (nr * lam_re + ni * lam_im) / den
    q_im = (ni * lam_re - nr * lam_im) / den
    bb_re = q_re[..., None] * b_re - q_im[..., None] * b_im
    bb_im = q_re[..., None] * b_im + q_im[..., None] * b_re
    return ab_re, ab_im, bb_re, bb_im


def _s5_matrices(ab_re, ab_im, bb_re, bb_im, c_re, c_im, d, batch):
    G, P, H = bb_re.shape
    gh = G // 2
    hw, ns = gh * H, gh * P
    eye = jnp.eye(gh, dtype=F32)

    def bmat(bb):
        t = jnp.transpose(bb.reshape(2, gh, P, H), (0, 1, 3, 2))
        return (t[:, :, :, None, :] * eye[None, :, None, :, None]).reshape(2, hw, ns)

    def cmat(cc):
        t = jnp.transpose(cc.reshape(2, gh, H, P), (0, 1, 3, 2))
        return (t[:, :, :, None, :] * eye[None, :, None, :, None]).reshape(2, ns, hw)

    bst = jnp.concatenate([bmat(bb_re), bmat(bb_im)], axis=2).astype(BF16)
    cst = jnp.concatenate([cmat(c_re), -cmat(c_im)], axis=1).astype(BF16)
    a_re = jnp.tile(ab_re.reshape(2, ns), (batch, 1))
    a_im = jnp.tile(ab_im.reshape(2, ns), (batch, 1))
    return bst, cst, a_re, a_im, d.reshape(2, hw)


def _s5_scan(u0, u1, bst, cst, a_re, a_im, d2):
    n_rows = u0.shape[0]
    ns = a_re.shape[1]
    rows = min(S5_TILE * SUBLANES, n_rows)
    blk = pl.BlockSpec((rows, LANES), lambda t: (t, 0))
    return pl.pallas_call(
        _s5_kernel,
        grid=(n_rows // rows,),
        in_specs=[blk, blk, _const_spec(bst.shape), _const_spec(cst.shape),
                  _const_spec(a_re.shape), _const_spec(a_im.shape), _const_spec(d2.shape)],
        out_specs=[blk, blk],
        out_shape=[jax.ShapeDtypeStruct((n_rows, LANES), F32)] * 2,
        scratch_shapes=[pltpu.VMEM((2 * ns // LANES, rows, LANES), F32),
                        pltpu.VMEM((2 * ns // LANES, SUBLANES, LANES), F32)],
        compiler_params=_params("arbitrary"),
        name="s5_scan",
    )(u0, u1, bst, cst, a_re, a_im, d2)


def _mlp_rows(x, gpre_ref, gpost_ref, w1_ref, w2_ref):
    h = _rms(x, gpre_ref[...]).astype(BF16)
    d_ff = w1_ref.shape[1]
    acc = jnp.zeros(x.shape, F32)
    for f in range(0, d_ff, MLP_FF_CHUNK):
        a = jnp.maximum(_dot(h, w1_ref[:, f:f + MLP_FF_CHUNK]), 0.0)
        acc = acc + _dot((a * a).astype(BF16), w2_ref[f:f + MLP_FF_CHUNK, :])
    return x + _rms(acc, gpost_ref[...])


def _even_post_kernel(x_ref, g0_ref, g1_ref, yb_ref, wglu_ref, wout_ref, g_ref,
                      gpre_ref, gpost_ref, w1_ref, w2_ref, o_ref):
    nb, tm, _ = x_ref.shape
    half_w = wout_ref.shape[0] // 2
    x1 = []
    for b in range(nb):
        blocks = [ref[pl.ds(2 * b + half, tm, stride=2 * nb), :]
                  for half in range(2) for ref in (g0_ref, g1_ref)]
        ga = jnp.concatenate(blocks, axis=1)
        ya = ga * jax.nn.sigmoid(_dot(ga.astype(BF16), wglu_ref[...]))
        mix = (_dot(ya.astype(BF16), wout_ref[0:half_w, :])
               + _dot(yb_ref[b], wout_ref[half_w:2 * half_w, :]))
        x1.append(x_ref[b] + _rms(mix, g_ref[...]))
    out = _mlp_rows(jnp.concatenate(x1, axis=0), gpre_ref, gpost_ref, w1_ref, w2_ref)
    for b in range(nb):
        o_ref[b] = out[b * tm:(b + 1) * tm]


def _even_post(x, g0, g1, yb, w_glu, w_out, g, g_pre, g_post, w1s, w2s, layer):
    B, L, D = x.shape
    tm = min(POST_TILE, L)
    tok = lambda r: (0, r, 0)
    gblk = pl.BlockSpec((tm * 2 * B, LANES), lambda r: (r, 0))
    consts = (w_glu, w_out, g, g_pre, g_post)
    return pl.pallas_call(
        _even_post_kernel,
        grid=(L // tm,),
        in_specs=[pl.BlockSpec((B, tm, D), tok), gblk, gblk, pl.BlockSpec((B, tm, 512), tok)]
                 + [_const_spec(c.shape) for c in consts]
                 + [_layer_spec(w1s.shape, layer), _layer_spec(w2s.shape, layer)],
        out_specs=pl.BlockSpec((B, tm, D), tok),
        out_shape=jax.ShapeDtypeStruct((B, L, D), F32),
        compiler_params=_params("parallel"),
        name="even_post_mlp",
    )(x, g0, g1, yb, *consts, w1s, w2s)


def _odd_mixer_rows(x, t0, pool_ref, gpre_ref, win_ref, pw_ref, ps_ref, lng_ref, lnb_ref, ws,
                    bs_ref, wout_ref, gpost_ref):
    tm = x.shape[0]
    pad = max(POOL_WINDOWS)
    gd = LANES
    n_grp = len(POOL_WINDOWS)
    width = n_grp * gd

    h = _rms(x, gpre_ref[...]).astype(BF16)
    xc = _dot(h, win_ref[:, 0:width])
    u = _dot(h, win_ref[:, width:2 * width])
    v = _dot(h, win_ref[:, 2 * width:3 * width])

    pool_ref[pad:pad + tm, :] = xc
    t = t0 + lax.broadcasted_iota(jnp.int32, (tm, 1), 0)
    yc = []
    for g, w in enumerate(POOL_WINDOWS):
        cols = slice(g * gd, (g + 1) * gd)
        tot = xc[:, cols]
        for lag in range(1, w):
            tot = tot + pool_ref[pl.ds(pad - lag, tm), cols]
        cnt = jnp.minimum(t + 1, w).astype(F32)
        pooled = tot / cnt - xc[:, cols]
        yc.append(_dot(pooled.astype(BF16), pw_ref[g]))
    pool_ref[0:pad, :] = pool_ref[tm:tm + pad, :]
    y_c = jnp.concatenate(yc, axis=-1) * ps_ref[...]

    gv = jax.nn.gelu(v)
    mu = jnp.mean(gv, axis=-1, keepdims=True)
    vc = gv - mu
    vn = vc * lax.rsqrt(jnp.mean(vc * vc, axis=-1, keepdims=True) + EPS)
    vn = (vn * lng_ref[...] + lnb_ref[...]).astype(BF16)
    chunks = []
    for n in range(tm // CHUNK):
        parts = [_dot(ws[g], vn[n * CHUNK:(n + 1) * CHUNK, g * gd:(g + 1) * gd]) for g in range(n_grp)]
        chunks.append(jnp.concatenate(parts, axis=-1) + bs_ref[...])
    y_d = jax.nn.gelu(u) * jnp.concatenate(chunks, axis=0)

    mix = _dot(y_c.astype(BF16), wout_ref[0:width, :]) + _dot(y_d.astype(BF16), wout_ref[width:2 * width, :])
    return x + _rms(mix, gpost_ref[...])


def _odd_layer_kernel(x_ref, gpre_ref, win_ref, pw_ref, ps_ref, lng_ref, lnb_ref, ws_ref,
                      bs_ref, wout_ref, gpost_ref, mgpre_ref, mgpost_ref, w1_ref, w2_ref,
                      o_ref, pool_ref):
    r = pl.program_id(0)
    nb, tm, _ = x_ref.shape
    pad = max(POOL_WINDOWS)

    @pl.when(r == 0)
    def _():
        pool_ref[:, 0:pad, :] = jnp.zeros((nb, pad, pool_ref.shape[2]), F32)

    rows = lax.broadcasted_iota(jnp.int32, (CHUNK, CHUNK), 0)
    cols = lax.broadcasted_iota(jnp.int32, (CHUNK, CHUNK), 1)
    ws = [jnp.where(cols <= rows, ws_ref[g], 0.0).astype(BF16) for g in range(ws_ref.shape[0])]
    x1 = [_odd_mixer_rows(x_ref[b], r * tm, pool_ref.at[b], gpre_ref, win_ref, pw_ref, ps_ref,
                          lng_ref, lnb_ref, ws, bs_ref, wout_ref, gpost_ref) for b in range(nb)]
    out = _mlp_rows(jnp.concatenate(x1, axis=0), mgpre_ref, mgpost_ref, w1_ref, w2_ref)
    for b in range(nb):
        o_ref[b] = out[b * tm:(b + 1) * tm]


def _odd_layer_call(x, consts, w1s, w2s, layer):
    B, L, D = x.shape
    tm = min(POST_TILE, L)
    width = consts[3].shape[1]
    tok = lambda r: (0, r, 0)
    return pl.pallas_call(
        _odd_layer_kernel,
        grid=(L // tm,),
        in_specs=[pl.BlockSpec((B, tm, D), tok)] + [_const_spec(c.shape) for c in consts]
                 + [_layer_spec(w1s.shape, layer), _layer_spec(w2s.shape, layer)],
        out_specs=pl.BlockSpec((B, tm, D), tok),
        out_shape=jax.ShapeDtypeStruct((B, L, D), F32),
        scratch_shapes=[pltpu.VMEM((B, tm + max(POOL_WINDOWS), width), F32)],
        compiler_params=_params("arbitrary"),
        name="odd_layer",
    )(x, *consts, w1s, w2s)


def _even_layer(x, g_pre, g_post, w_in, lam_re, lam_im, log_dt, b_re, b_im, c_re, c_im, d, w_glu,
                b_f, w_out, mlp_g_pre, mlp_g_post, w1s, w2s, layer):
    B, L, D = x.shape
    n_heads = b_f.shape[0]
    s5w = d.shape[0]
    fw = n_heads * HEAD_DIM
    assert s5w == 4 * LANES and w_in.shape[1] == s5w + 3 * fw + n_heads
    wf = jnp.pad(w_in[:, s5w + 3 * fw:], ((0, 0), (0, LANES - n_heads))).astype(BF16)
    u0, u1, q, k, v, fl = _even_inproj(x, g_pre[None], w_in.astype(BF16), wf)

    feat, ones = _fox_prep(fl, b_f)
    y_b = _fox_attn(q, k, v, feat, ones)

    ab_re, ab_im, bb_re, bb_im = _s5_discretize(lam_re, lam_im, log_dt, b_re, b_im)
    bst, cst, a_re, a_im, d2 = _s5_matrices(ab_re, ab_im, bb_re, bb_im, c_re, c_im, d, B)
    g0, g1 = _s5_scan(u0, u1, bst, cst, a_re, a_im, d2)

    return _even_post(x, g0, g1, y_b, w_glu.astype(BF16), w_out.astype(BF16), g_post[None],
                      mlp_g_pre[None], mlp_g_post[None], w1s, w2s, layer)


def _odd_layer(x, g_pre, g_post, w_in, pool_w, pool_scale, ln_g, ln_b, w_s, b_s, w_out,
               mlp_g_pre, mlp_g_post, w1s, w2s, layer):
    bs_full = jnp.repeat(jnp.transpose(b_s), LANES, axis=1)
    consts = (g_pre[None], w_in.astype(BF16), pool_w.astype(BF16), pool_scale[None], ln_g[None],
              ln_b[None], w_s, bs_full, w_out.astype(BF16), g_post[None], mlp_g_pre[None],
              mlp_g_post[None])
    return _odd_layer_call(x, consts, w1s, w2s, layer)


def kernel(x, mix_pre_g, mix_post_g, mlp_pre_g, mlp_post_g, w_in_even, s5_lam_re, s5_lam_im, s5_log_dt, s5_b_re, s5_b_im, s5_c_re, s5_c_im, s5_d, s5_w_glu, fox_b_f, w_out_even, w_in_odd, pool_w, pool_scale, sgu_ln_g, sgu_ln_b, sgu_w_s, sgu_b_s, w_out_odd, mlp_w1, mlp_w2):
    depth = mix_pre_g.shape[0]
    w1s, w2s = mlp_w1.astype(BF16), mlp_w2.astype(BF16)
    for l in range(depth):
        i = l // 2
        if l % 2 == 0:
            x = _even_layer(x, mix_pre_g[l], mix_post_g[l], w_in_even[i], s5_lam_re[i], s5_lam_im[i],
                            s5_log_dt[i], s5_b_re[i], s5_b_im[i], s5_c_re[i], s5_c_im[i], s5_d[i],
                            s5_w_glu[i], fox_b_f[i], w_out_even[i],
                            mlp_pre_g[l], mlp_post_g[l], w1s, w2s, l)
        else:
            x = _odd_layer(x, mix_pre_g[l], mix_post_g[l], w_in_odd[i], pool_w[i], pool_scale[i],
                           sgu_ln_g[i], sgu_ln_b[i], sgu_w_s[i], sgu_b_s[i], w_out_odd[i],
                           mlp_pre_g[l], mlp_post_g[l], w1s, w2s, l)
    return x
```

```python
import functools

import numpy as np
import jax
import jax.numpy as jnp
from jax import lax
from jax.experimental import pallas as pl
from jax.experimental.pallas import tpu as pltpu

F32 = jnp.float32
BF16 = jnp.bfloat16

EPS = 1e-6
LOG2E = 1.4426950408889634
NEG_BIG = -1e30

LANES = 128
SUBLANES = 8
VMEM_LIMIT = 56 * 1024 * 1024

S5_GROUP = 16
S5_STATE = 64
HEAD_DIM = 64
POOL_WINDOWS = (2, 4, 8, 16)
CHUNK = 128

ROW_TILE = 512
POST_TILE = 256
MLP_FF_CHUNK = 1024
S5_TILE = 128
ATT_TQ = 1024
ATT_TK = 512
ATT_PAIRS = 2


def _params(*sem):
    return pltpu.CompilerParams(dimension_semantics=sem, vmem_limit_bytes=VMEM_LIMIT)


def _const_spec(shape):
    nd = len(shape)
    return pl.BlockSpec(shape, lambda *_: (0,) * nd, pipeline_mode=pl.Buffered(1))


def _layer_spec(shape, layer):
    nd = len(shape)
    return pl.BlockSpec((None,) + tuple(shape[1:]), lambda *_: (layer,) + (0,) * (nd - 1),
                        pipeline_mode=pl.Buffered(1))


def _rms(x, g):
    return x * lax.rsqrt(jnp.mean(x * x, axis=-1, keepdims=True) + EPS) * g


def _split3(x):
    hi = x.astype(BF16)
    r1 = x - hi.astype(F32)
    mid = r1.astype(BF16)
    lo = (r1 - mid.astype(F32)).astype(BF16)
    return hi, mid, lo


def _dot(a, b):
    return jnp.dot(a, b, preferred_element_type=F32)


def _even_inproj_kernel(x_ref, g_ref, w_ref, wf_ref, u0_ref, u1_ref, q_ref, k_ref, v_ref, f_ref):
    nb, tm, _ = x_ref.shape
    s5w = 4 * LANES
    fw = q_ref.shape[2]
    q_scale = HEAD_DIM ** -0.5 * LOG2E
    for b in range(nb):
        h = _rms(x_ref[b], g_ref[...]).astype(BF16)
        zu = _dot(h, w_ref[:, 0:s5w])
        for half in range(2):
            rows = pl.ds(2 * b + half, tm, stride=2 * nb)
            u0_ref[rows, :] = zu[:, 2 * half * LANES:(2 * half + 1) * LANES]
            u1_ref[rows, :] = zu[:, (2 * half + 1) * LANES:(2 * half + 2) * LANES]
        q_ref[b] = (_dot(h, w_ref[:, s5w:s5w + fw]) * q_scale).astype(BF16)
        k_ref[b] = _dot(h, w_ref[:, s5w + fw:s5w + 2 * fw]).astype(BF16)
        v_ref[b] = _dot(h, w_ref[:, s5w + 2 * fw:s5w + 3 * fw]).astype(BF16)
        f_ref[b] = _dot(h, wf_ref[...])


def _even_inproj(x, g, w, wf):
    B, L, D = x.shape
    tm = min(ROW_TILE, L)
    tok = lambda r: (0, r, 0)
    ublk = pl.BlockSpec((tm * 2 * B, LANES), lambda r: (r, 0))
    return pl.pallas_call(
        _even_inproj_kernel,
        grid=(L // tm,),
        in_specs=[pl.BlockSpec((B, tm, D), tok), _const_spec((1, D)), _const_spec(w.shape),
                  _const_spec(wf.shape)],
        out_specs=[
            ublk, ublk,
            pl.BlockSpec((B, tm, 512), tok),
            pl.BlockSpec((B, tm, 512), tok),
            pl.BlockSpec((B, tm, 512), tok),
            pl.BlockSpec((B, tm, LANES), tok),
        ],
        out_shape=[
            jax.ShapeDtypeStruct((L * 2 * B, LANES), F32),
            jax.ShapeDtypeStruct((L * 2 * B, LANES), F32),
            jax.ShapeDtypeStruct((B, L, 512), BF16),
            jax.ShapeDtypeStruct((B, L, 512), BF16),
            jax.ShapeDtypeStruct((B, L, 512), BF16),
            jax.ShapeDtypeStruct((B, L, LANES), F32),
        ],
        compiler_params=_params("parallel"),
        name="even_inproj",
    )(x, g, w, wf)


def _fox_prep_kernel(f_ref, bf_ref, tri_ref, place_ref, feat_ref, carry_ref):
    @pl.when(pl.program_id(1) == 0)
    def _():
        carry_ref[...] = jnp.zeros_like(carry_ref)

    z = f_ref[0] + bf_ref[...]
    logf = jnp.minimum(z, 0.0) - jnp.log1p(jnp.exp(-jnp.abs(z)))
    c3 = _dot(tri_ref[...], jnp.concatenate(_split3(logf), axis=1))
    csum = c3[:, 0:LANES] + c3[:, LANES:2 * LANES] + c3[:, 2 * LANES:3 * LANES] + carry_ref[...]
    tm = csum.shape[0]
    carry_ref[...] = csum[tm - 1:tm, :]
    pieces = jnp.concatenate(_split3(csum * LOG2E), axis=1)
    feat_ref[0] = _dot(pieces, place_ref[...]).astype(BF16)


def _fox_placement(n_heads):
    width = (n_heads // 2) * LANES
    place = np.zeros((3 * LANES, width), np.float32)
    ones = np.zeros((2, LANES), np.float32)
    for h in range(n_heads):
        base = HEAD_DIM if h % 2 == 0 else 0
        for j in range(3):
            place[j * LANES + h, (h // 2) * LANES + base + j] = 1.0
            place[j * LANES + h, (h // 2) * LANES + base + 3 + j] = -1.0
            ones[0, base + 3 + j] = 1.0
            ones[1, base + j] = 1.0
    return place, ones


def _fox_prep(fl, b_f):
    B, L, _ = fl.shape
    n_heads = b_f.shape[0]
    width = (n_heads // 2) * LANES
    tm = min(ROW_TILE, L)
    place, ones = _fox_placement(n_heads)
    tri = jnp.asarray(np.tril(np.ones((tm, tm), np.float32)), BF16)
    bf = jnp.zeros((1, LANES), F32).at[0, :n_heads].set(b_f)
    tok = lambda b, r: (b, r, 0)
    feat = pl.pallas_call(
        _fox_prep_kernel,
        grid=(B, L // tm),
        in_specs=[pl.BlockSpec((1, tm, LANES), tok), _const_spec((1, LANES)),
                  _const_spec((tm, tm)), _const_spec(place.shape)],
        out_specs=pl.BlockSpec((1, tm, width), tok),
        out_shape=jax.ShapeDtypeStruct((B, L, width), BF16),
        scratch_shapes=[pltpu.VMEM((1, LANES), F32)],
        compiler_params=_params("parallel", "arbitrary"),
        name="fox_prep",
    )(fl, bf, tri, jnp.asarray(place, BF16))
    return feat, jnp.asarray(ones)


def _fox_attn_kernel(q_ref, k_ref, v_ref, fq_ref, fk_ref, ones_ref, *refs):
    n_riders = (len(refs) - 6) // 2
    o_ref = refs[n_riders]
    kaug_ref, vaug_ref, qaug_ref, m_ref, acc_ref = refs[2 * n_riders + 1:]
    for src, dst in zip(refs[:n_riders], refs[n_riders + 1:2 * n_riders + 1]):
        dst[...] = src[...].astype(BF16)

    qi = pl.program_id(2)
    tq = q_ref.shape[1]
    L = k_ref.shape[1]
    tk = min(ATT_TK, L)
    n_heads = kaug_ref.shape[0]
    assert tq % tk == 0

    def merge(lanes, first, x_ref, f):
        x = x_ref[0, :, first * LANES:(first + 1) * LANES].astype(F32)
        return (jnp.where(lanes < HEAD_DIM, x, f(0)).astype(BF16),
                jnp.where(lanes >= HEAD_DIM, x, f(1)).astype(BF16))

    def features(f_ref, pr, side):
        one = ones_ref[side:side + 1, :]
        return f_ref[0, :, pr * LANES:(pr + 1) * LANES].astype(F32) * (1.0 - one) + one

    @pl.when(qi == 0)
    def _():
        lanes = lax.broadcasted_iota(jnp.int32, (L, LANES), 1)
        for pr in range(n_heads // 2):
            fk = features(fk_ref, pr, 1)
            kaug_ref[2 * pr], kaug_ref[2 * pr + 1] = merge(lanes, pr, k_ref, lambda e: fk)
            ones = ((lanes == HEAD_DIM).astype(F32), (lanes == 0).astype(F32))
            vaug_ref[2 * pr], vaug_ref[2 * pr + 1] = merge(lanes, pr, v_ref, lambda e: ones[e])

    lanes_q = lax.broadcasted_iota(jnp.int32, (tq, LANES), 1)
    for pr in range(n_heads // 2):
        fq = features(fq_ref, pr, 0)
        qaug_ref[2 * pr], qaug_ref[2 * pr + 1] = merge(lanes_q, pr, q_ref, lambda e: fq)

    def tile(j, r0, r1, masked, first=False):
        start = pl.multiple_of(j * tk, tk)
        for e in range(n_heads):
            kt = kaug_ref[e, pl.ds(start, tk), :]
            s = lax.dot_general(qaug_ref[e, r0:r1, :], kt, (((1,), (1,)), ((), ())),
                                preferred_element_type=F32)
            if masked:
                row = lax.broadcasted_iota(jnp.int32, s.shape, 0)
                col = lax.broadcasted_iota(jnp.int32, s.shape, 1)
                s = jnp.where(col <= row, s, NEG_BIG)
            m_new = jnp.broadcast_to(jnp.max(s, axis=1, keepdims=True), (r1 - r0, LANES))
            if not first:
                m = m_ref[e, r0:r1, :]
                m_new = jnp.maximum(m, m_new)
            p = jnp.exp2((s - jnp.concatenate([m_new] * (tk // LANES), axis=1)).astype(BF16))
            pv = _dot(p, vaug_ref[e, pl.ds(start, tk), :])
            acc_ref[e, r0:r1, :] = pv if first else jnp.exp2(m - m_new) * acc_ref[e, r0:r1, :] + pv
            m_ref[e, r0:r1, :] = m_new

    n_diag = tq // tk

    tile(qi * n_diag, 0, tk, True, first=True)
    if tk < tq:
        tile(qi * n_diag, tk, tq, False, first=True)

    def body(j, c):
        for d in range(n_diag):
            tile(j * n_diag + d, 0, tq, False)
        return c

    lax.fori_loop(0, qi, body, 0)
    for d in range(1, n_diag):
        tile(qi * n_diag + d, d * tk, (d + 1) * tk, True)
        if (d + 1) * tk < tq:
            tile(qi * n_diag + d, (d + 1) * tk, tq, False)

    for pr in range(n_heads // 2):
        acc0, acc1 = acc_ref[2 * pr], acc_ref[2 * pr + 1]
        out0 = acc0 / acc0[:, HEAD_DIM:HEAD_DIM + 1]
        out1 = acc1 / acc1[:, 0:1]
        o_ref[0, :, pr * LANES:(pr + 1) * LANES] = jnp.where(lanes_q < HEAD_DIM, out0, out1).astype(BF16)


def _fox_attn(q, k, v, feat, ones, riders):
    B, L, W = q.shape
    bw = ATT_PAIRS * LANES
    n_heads = 2 * ATT_PAIRS
    tq = min(ATT_TQ, L)
    grid = (B, W // bw, L // tq)
    n_steps = grid[0] * grid[1] * grid[2]
    qspec = pl.BlockSpec((1, tq, bw), lambda b, p, i: (b, i, p))
    kspec = pl.BlockSpec((1, L, bw), lambda b, p, i: (b, 0, p))
    step = lambda b, p, i: ((b * grid[1] + p) * grid[2] + i, 0)
    rspecs = [pl.BlockSpec((r.shape[0] // n_steps, r.shape[1]), step) for r in riders]
    out = pl.pallas_call(
        _fox_attn_kernel,
        grid=grid,
        in_specs=[qspec, kspec, kspec, qspec, kspec, _const_spec(ones.shape)] + rspecs,
        out_specs=[qspec] + rspecs,
        out_shape=[jax.ShapeDtypeStruct((B, L, W), BF16)]
                  + [jax.ShapeDtypeStruct(r.shape, BF16) for r in riders],
        scratch_shapes=[pltpu.VMEM((n_heads, L, LANES), BF16), pltpu.VMEM((n_heads, L, LANES), BF16),
                        pltpu.VMEM((n_heads, tq, LANES), BF16), pltpu.VMEM((n_heads, tq, LANES), F32),
                        pltpu.VMEM((n_heads, tq, LANES), F32)],
        compiler_params=_params("parallel", "parallel", "arbitrary"),
        name="fox_attn",
    )(q, k, v, feat, feat, ones, *riders)
    return out[0], out[1:]


def _s5_kernel(u0_ref, u1_ref, bst_ref, cst_ref, are_ref, aim_ref, d_ref, o0_ref, o1_ref,
               x_ref, st_ref):
    rows = u0_ref.shape[0]
    nb = x_ref.shape[0] // 2
    hrows = rows // 2

    @pl.when(pl.program_id(0) == 0)
    def _():
        st_ref[...] = jnp.zeros_like(st_ref)

    def half_rows(ref, h, *lead):
        return ref[(*lead, pl.ds(h, hrows, stride=2), slice(None))]

    u_h = []
    for h in range(2):
        u = jnp.concatenate([half_rows(u0_ref, h), half_rows(u1_ref, h)], axis=1)
        u_h.append(u)
        bu = _dot(u.astype(BF16), bst_ref[h])
        for c in range(2 * nb):
            x_ref[c, pl.ds(h, hrows, stride=2), :] = bu[:, c * LANES:(c + 1) * LANES]

    a_re = [are_ref[:, c * LANES:(c + 1) * LANES] for c in range(nb)]
    a_im = [aim_ref[:, c * LANES:(c + 1) * LANES] for c in range(nb)]

    def step(i, state):
        r = pl.ds(pl.multiple_of(i * SUBLANES, SUBLANES), SUBLANES)
        new = []
        for c in range(nb):
            s_re, s_im = state[2 * c], state[2 * c + 1]
            n_re = a_re[c] * s_re - a_im[c] * s_im + x_ref[c, r, :]
            n_im = a_re[c] * s_im + a_im[c] * s_re + x_ref[nb + c, r, :]
            x_ref[c, r, :] = n_re
            x_ref[nb + c, r, :] = n_im
            new += [n_re, n_im]
        return tuple(new)

    state = lax.fori_loop(0, rows // SUBLANES, step,
                          tuple(st_ref[c] for c in range(2 * nb)), unroll=4)
    for c in range(2 * nb):
        st_ref[c] = state[c]

    for h in range(2):
        xs = jnp.concatenate([half_rows(x_ref, h, c) for c in range(nb)]
                             + [half_rows(x_ref, h, nb + c) for c in range(nb)], axis=1)
        y = _dot(xs.astype(BF16), cst_ref[h])
        g = jax.nn.gelu(y + d_ref[h:h + 1, :] * u_h[h])
        o0_ref[pl.ds(h, hrows, stride=2), :] = g[:, 0:LANES]
        o1_ref[pl.ds(h, hrows, stride=2), :] = g[:, LANES:2 * LANES]


def _s5_discretize(lam_re, lam_im, log_dt, b_re, b_im):
    dt = jnp.exp(log_dt)[:, None]
    mag = jnp.exp(lam_re * dt)
    ab_re = mag * jnp.cos(lam_im * dt)
    ab_im = mag * jnp.sin(lam_im * dt)
    den = lam_re * lam_re + lam_im * lam_im
    nr = ab_re - 1.0
    ni = ab_im
    q_re = (nr * lam_re + ni * lam_im) / den
    q_im = (ni * lam_re - nr * lam_im) / den
    bb_re = q_re[..., None] * b_re - q_im[..., None] * b_im
    bb_im = q_re[..., None] * b_im + q_im[..., None] * b_re
    return ab_re, ab_im, bb_re, bb_im


def _s5_matrices(ab_re, ab_im, bb_re, bb_im, c_re, c_im, d, batch):
    G, P, H = bb_re.shape
    gh = G // 2
    hw, ns = gh * H, gh * P
    eye = jnp.eye(gh, dtype=F32)

    def bmat(bb):
        t = jnp.transpose(bb.reshape(2, gh, P, H), (0, 1, 3, 2))
        return (t[:, :, :, None, :] * eye[None, :, None, :, None]).reshape(2, hw, ns)

    def cmat(cc):
        t = jnp.transpose(cc.reshape(2, gh, H, P), (0, 1, 3, 2))
        return (t[:, :, :, None, :] * eye[None, :, None, :, None]).reshape(2, ns, hw)

    bst = jnp.concatenate([bmat(bb_re), bmat(bb_im)], axis=2).astype(BF16)
    cst = jnp.concatenate([cmat(c_re), -cmat(c_im)], axis=1).astype(BF16)
    a_re = jnp.tile(ab_re.reshape(2, ns), (batch, 1))
    a_im = jnp.tile(ab_im.reshape(2, ns), (batch, 1))
    return bst, cst, a_re, a_im, d.reshape(2, hw)


def _s5_scan(u0, u1, bst, cst, a_re, a_im, d2):
    n_rows = u0.shape[0]
    ns = a_re.shape[1]
    rows = min(S5_TILE * SUBLANES, n_rows)
    blk = pl.BlockSpec((rows, LANES), lambda t: (t, 0))
    return pl.pallas_call(
        _s5_kernel,
        grid=(n_rows // rows,),
        in_specs=[blk, blk, _const_spec(bst.shape), _const_spec(cst.shape),
                  _const_spec(a_re.shape), _const_spec(a_im.shape), _const_spec(d2.shape)],
        out_specs=[blk, blk],
        out_shape=[jax.ShapeDtypeStruct((n_rows, LANES), F32)] * 2,
        scratch_shapes=[pltpu.VMEM((2 * ns // LANES, rows, LANES), F32),
                        pltpu.VMEM((2 * ns // LANES, SUBLANES, LANES), F32)],
        compiler_params=_params("arbitrary"),
        name="s5_scan",
    )(u0, u1, bst, cst, a_re, a_im, d2)


def _mlp_rows(x, gpre_ref, gpost_ref, w1_ref, w2_ref):
    h = _rms(x, gpre_ref[...]).astype(BF16)
    d_ff = w1_ref.shape[1]
    acc = jnp.zeros(x.shape, F32)
    for f in range(0, d_ff, MLP_FF_CHUNK):
        a = jnp.maximum(_dot(h, w1_ref[:, f:f + MLP_FF_CHUNK]), 0.0)
        acc = acc + _dot((a * a).astype(BF16), w2_ref[f:f + MLP_FF_CHUNK, :])
    return x + _rms(acc, gpost_ref[...])


def _even_post_kernel(x_ref, g0_ref, g1_ref, yb_ref, wglu_ref, wout_ref, g_ref,
                      gpre_ref, gpost_ref, w1_ref, w2_ref, o_ref):
    nb, tm, _ = x_ref.shape
    half_w = wout_ref.shape[0] // 2
    x1 = []
    for b in range(nb):
        blocks = [ref[pl.ds(2 * b + half, tm, stride=2 * nb), :]
                  for half in range(2) for ref in (g0_ref, g1_ref)]
        ga = jnp.concatenate(blocks, axis=1)
        ya = ga * jax.nn.sigmoid(_dot(ga.astype(BF16), wglu_ref[...]))
        mix = (_dot(ya.astype(BF16), wout_ref[0:half_w, :])
               + _dot(yb_ref[b], wout_ref[half_w:2 * half_w, :]))
        x1.append(x_ref[b] + _rms(mix, g_ref[...]))
    out = _mlp_rows(jnp.concatenate(x1, axis=0), gpre_ref, gpost_ref, w1_ref, w2_ref)
    for b in range(nb):
        o_ref[b] = out[b * tm:(b + 1) * tm]


def _even_post(x, g0, g1, yb, w_glu, w_out, g, g_pre, g_post, w1s, w2s, layer):
    B, L, D = x.shape
    tm = min(POST_TILE, L)
    tok = lambda r: (0, r, 0)
    gblk = pl.BlockSpec((tm * 2 * B, LANES), lambda r: (r, 0))
    consts = (w_glu, w_out, g, g_pre, g_post)
    return pl.pallas_call(
        _even_post_kernel,
        grid=(L // tm,),
        in_specs=[pl.BlockSpec((B, tm, D), tok), gblk, gblk, pl.BlockSpec((B, tm, 512), tok)]
                 + [_const_spec(c.shape) for c in consts]
                 + [_layer_spec(w1s.shape, layer), _layer_spec(w2s.shape, layer)],
        out_specs=pl.BlockSpec((B, tm, D), tok),
        out_shape=jax.ShapeDtypeStruct((B, L, D), F32),
        compiler_params=_params("parallel"),
        name="even_post_mlp",
    )(x, g0, g1, yb, *consts, w1s, w2s)


def _odd_mixer_rows(x, t0, pool_ref, gpre_ref, win_ref, pw_ref, ps_ref, lng_ref, lnb_ref, ws,
                    bs_ref, wout_ref, gpost_ref):
    tm = x.shape[0]
    pad = max(POOL_WINDOWS)
    gd = LANES
    n_grp = len(POOL_WINDOWS)
    width = n_grp * gd

    h = _rms(x, gpre_ref[...]).astype(BF16)
    xc = _dot(h, win_ref[:, 0:width])
    u = _dot(h, win_ref[:, width:2 * width])
    v = _dot(h, win_ref[:, 2 * width:3 * width])

    pool_ref[pad:pad + tm, :] = xc
    t = t0 + lax.broadcasted_iota(jnp.int32, (tm, 1), 0)
    yc = []
    for g, w in enumerate(POOL_WINDOWS):
        cols = slice(g * gd, (g + 1) * gd)
        tot = xc[:, cols]
        for lag in range(1, w):
            tot = tot + pool_ref[pl.ds(pad - lag, tm), cols]
        cnt = jnp.minimum(t + 1, w).astype(F32)
        pooled = tot / cnt - xc[:, cols]
        yc.append(_dot(pooled.astype(BF16), pw_ref[g]))
    pool_ref[0:pad, :] = pool_ref[tm:tm + pad, :]
    y_c = jnp.concatenate(yc, axis=-1) * ps_ref[...]

    gv = jax.nn.gelu(v)
    mu = jnp.mean(gv, axis=-1, keepdims=True)
    vc = gv - mu
    vn = vc * lax.rsqrt(jnp.mean(vc * vc, axis=-1, keepdims=True) + EPS)
    vn = (vn * lng_ref[...] + lnb_ref[...]).astype(BF16)
    chunks = []
    for n in range(tm // CHUNK):
        parts = [_dot(ws[g], vn[n * CHUNK:(n + 1) * CHUNK, g * gd:(g + 1) * gd]) for g in range(n_grp)]
        chunks.append(jnp.concatenate(parts, axis=-1) + bs_ref[...])
    y_d = jax.nn.gelu(u) * jnp.concatenate(chunks, axis=0)

    mix = _dot(y_c.astype(BF16), wout_ref[0:width, :]) + _dot(y_d.astype(BF16), wout_ref[width:2 * width, :])
    return x + _rms(mix, gpost_ref[...])


def _odd_layer_kernel(x_ref, gpre_ref, win_ref, pw_ref, ps_ref, lng_ref, lnb_ref, ws_ref,
                      bs_ref, wout_ref, gpost_ref, mgpre_ref, mgpost_ref, w1_ref, w2_ref,
                      o_ref, pool_ref):
    r = pl.program_id(0)
    nb, tm, _ = x_ref.shape
    pad = max(POOL_WINDOWS)

    @pl.when(r == 0)
    def _():
        pool_ref[:, 0:pad, :] = jnp.zeros((nb, pad, pool_ref.shape[2]), F32)

    rows = lax.broadcasted_iota(jnp.int32, (CHUNK, CHUNK), 0)
    cols = lax.broadcasted_iota(jnp.int32, (CHUNK, CHUNK), 1)
    ws = [jnp.where(cols <= rows, ws_ref[g], 0.0).astype(BF16) for g in range(ws_ref.shape[0])]
    x1 = [_odd_mixer_rows(x_ref[b], r * tm, pool_ref.at[b], gpre_ref, win_ref, pw_ref, ps_ref,
                          lng_ref, lnb_ref, ws, bs_ref, wout_ref, gpost_ref) for b in range(nb)]
    out = _mlp_rows(jnp.concatenate(x1, axis=0), mgpre_ref, mgpost_ref, w1_ref, w2_ref)
    for b in range(nb):
        o_ref[b] = out[b * tm:(b + 1) * tm]


def _odd_layer_call(x, consts, w1s, w2s, layer):
    B, L, D = x.shape
    tm = min(POST_TILE, L)
    width = consts[3].shape[1]
    tok = lambda r: (0, r, 0)
    return pl.pallas_call(
        _odd_layer_kernel,
        grid=(L // tm,),
        in_specs=[pl.BlockSpec((B, tm, D), tok)] + [_const_spec(c.shape) for c in consts]
                 + [_layer_spec(w1s.shape, layer), _layer_spec(w2s.shape, layer)],
        out_specs=pl.BlockSpec((B, tm, D), tok),
        out_shape=jax.ShapeDtypeStruct((B, L, D), F32),
        scratch_shapes=[pltpu.VMEM((B, tm + max(POOL_WINDOWS), width), F32)],
        compiler_params=_params("arbitrary"),
        name="odd_layer",
    )(x, *consts, w1s, w2s)


def _even_layer(x, g_pre, g_post, w_in, lam_re, lam_im, log_dt, b_re, b_im, c_re, c_im, d, w_glu,
                b_f, w_out, mlp_g_pre, mlp_g_post, w1s, w2s, layer):
    B, L, D = x.shape
    n_heads = b_f.shape[0]
    s5w = d.shape[0]
    fw = n_heads * HEAD_DIM
    assert s5w == 4 * LANES and w_in.shape[1] == s5w + 3 * fw + n_heads
    wf = jnp.pad(w_in[:, s5w + 3 * fw:], ((0, 0), (0, LANES - n_heads))).astype(BF16)
    u0, u1, q, k, v, fl = _even_inproj(x, g_pre[None], w_in.astype(BF16), wf)

    feat, ones = _fox_prep(fl, b_f)
    stacks = (w1s, w2s)
    riders = tuple(w.reshape(-1, w.shape[-1]) for w in stacks if w.dtype != BF16)
    y_b, cast = _fox_attn(q, k, v, feat, ones, riders)
    cast = iter(cast)
    w1s, w2s = (w if w.dtype == BF16 else next(cast).reshape(w.shape) for w in stacks)

    ab_re, ab_im, bb_re, bb_im = _s5_discretize(lam_re, lam_im, log_dt, b_re, b_im)
    bst, cst, a_re, a_im, d2 = _s5_matrices(ab_re, ab_im, bb_re, bb_im, c_re, c_im, d, B)
    g0, g1 = _s5_scan(u0, u1, bst, cst, a_re, a_im, d2)

    x = _even_post(x, g0, g1, y_b, w_glu.astype(BF16), w_out.astype(BF16), g_post[None],
                   mlp_g_pre[None], mlp_g_post[None], w1s, w2s, layer)
    return x, w1s, w2s


def _odd_layer(x, g_pre, g_post, w_in, pool_w, pool_scale, ln_g, ln_b, w_s, b_s, w_out,
               mlp_g_pre, mlp_g_post, w1s, w2s, layer):
    bs_full = jnp.repeat(jnp.transpose(b_s), LANES, axis=1)
    consts = (g_pre[None], w_in.astype(BF16), pool_w.astype(BF16), pool_scale[None], ln_g[None],
              ln_b[None], w_s, bs_full, w_out.astype(BF16), g_post[None], mlp_g_pre[None],
              mlp_g_post[None])
    return _odd_layer_call(x, consts, w1s, w2s, layer)


def kernel(x, mix_pre_g, mix_post_g, mlp_pre_g, mlp_post_g, w_in_even, s5_lam_re, s5_lam_im, s5_log_dt, s5_b_re, s5_b_im, s5_c_re, s5_c_im, s5_d, s5_w_glu, fox_b_f, w_out_even, w_in_odd, pool_w, pool_scale, sgu_ln_g, sgu_ln_b, sgu_w_s, sgu_b_s, w_out_odd, mlp_w1, mlp_w2):
    depth = mix_pre_g.shape[0]
    w1s, w2s = mlp_w1, mlp_w2
    for l in range(depth):
        i = l // 2
        if l % 2 == 0:
            x, w1s, w2s = _even_layer(
                x, mix_pre_g[l], mix_post_g[l], w_in_even[i], s5_lam_re[i], s5_lam_im[i],
                s5_log_dt[i], s5_b_re[i], s5_b_im[i], s5_c_re[i], s5_c_im[i], s5_d[i],
                s5_w_glu[i], fox_b_f[i], w_out_even[i], mlp_pre_g[l], mlp_post_g[l], w1s, w2s, l)
        else:
            x = _odd_layer(x, mix_pre_g[l], mix_post_g[l], w_in_odd[i], pool_w[i], pool_scale[i],
                           sgu_ln_g[i], sgu_ln_b[i], sgu_w_s[i], sgu_b_s[i], w_out_odd[i],
                           mlp_pre_g[l], mlp_post_g[l], w1s, w2s, l)
    return x
```

```python
import functools

import numpy as np
import jax
import jax.numpy as jnp
from jax import lax
from jax.experimental import pallas as pl
from jax.experimental.pallas import tpu as pltpu

F32 = jnp.float32
BF16 = jnp.bfloat16

EPS = 1e-6
LOG2E = 1.4426950408889634
NEG_BIG = -1e30

LANES = 128
SUBLANES = 8
VMEM_LIMIT = 56 * 1024 * 1024

S5_GROUP = 16
S5_STATE = 64
HEAD_DIM = 64
POOL_WINDOWS = (2, 4, 8, 16)
CHUNK = 128

ROW_TILE = 512
POST_TILE = 256
MLP_FF_CHUNK = 1024
S5_TILE = 128
ATT_TQ = 1024
ATT_TK = 512
ATT_PAIRS = 2


def _params(*sem):
    return pltpu.CompilerParams(dimension_semantics=sem, vmem_limit_bytes=VMEM_LIMIT)


def _const_spec(shape):
    nd = len(shape)
    return pl.BlockSpec(shape, lambda *_: (0,) * nd, pipeline_mode=pl.Buffered(1))


def _layer_spec(shape, layer):
    nd = len(shape)
    return pl.BlockSpec((None,) + tuple(shape[1:]), lambda *_: (layer,) + (0,) * (nd - 1),
                        pipeline_mode=pl.Buffered(1))


def _rms(x, g):
    return x * lax.rsqrt(jnp.mean(x * x, axis=-1, keepdims=True) + EPS) * g


def _split3(x):
    hi = x.astype(BF16)
    r1 = x - hi.astype(F32)
    mid = r1.astype(BF16)
    lo = (r1 - mid.astype(F32)).astype(BF16)
    return hi, mid, lo


def _dot(a, b):
    return jnp.dot(a, b, preferred_element_type=F32)


def _even_inproj_kernel(x_ref, g_ref, w_ref, wf_ref, u0_ref, u1_ref, q_ref, k_ref, v_ref, f_ref):
    nb, tm, _ = x_ref.shape
    s5w = 4 * LANES
    fw = q_ref.shape[2]
    q_scale = HEAD_DIM ** -0.5 * LOG2E
    for b in range(nb):
        h = _rms(x_ref[b], g_ref[...]).astype(BF16)
        zu = _dot(h, w_ref[:, 0:s5w])
        for half in range(2):
            rows = pl.ds(2 * b + half, tm, stride=2 * nb)
            u0_ref[rows, :] = zu[:, 2 * half * LANES:(2 * half + 1) * LANES]
            u1_ref[rows, :] = zu[:, (2 * half + 1) * LANES:(2 * half + 2) * LANES]
        q_ref[b] = (_dot(h, w_ref[:, s5w:s5w + fw]) * q_scale).astype(BF16)
        k_ref[b] = _dot(h, w_ref[:, s5w + fw:s5w + 2 * fw]).astype(BF16)
        v_ref[b] = _dot(h, w_ref[:, s5w + 2 * fw:s5w + 3 * fw]).astype(BF16)
        f_ref[b] = _dot(h, wf_ref[...])


def _even_inproj(x, g, w, wf):
    B, L, D = x.shape
    tm = min(ROW_TILE, L)
    tok = lambda r: (0, r, 0)
    ublk = pl.BlockSpec((tm * 2 * B, LANES), lambda r: (r, 0))
    return pl.pallas_call(
        _even_inproj_kernel,
        grid=(L // tm,),
        in_specs=[pl.BlockSpec((B, tm, D), tok), _const_spec((1, D)), _const_spec(w.shape),
                  _const_spec(wf.shape)],
        out_specs=[
            ublk, ublk,
            pl.BlockSpec((B, tm, 512), tok),
            pl.BlockSpec((B, tm, 512), tok),
            pl.BlockSpec((B, tm, 512), tok),
            pl.BlockSpec((B, tm, LANES), tok),
        ],
        out_shape=[
            jax.ShapeDtypeStruct((L * 2 * B, LANES), F32),
            jax.ShapeDtypeStruct((L * 2 * B, LANES), F32),
            jax.ShapeDtypeStruct((B, L, 512), BF16),
            jax.ShapeDtypeStruct((B, L, 512), BF16),
            jax.ShapeDtypeStruct((B, L, 512), BF16),
            jax.ShapeDtypeStruct((B, L, LANES), F32),
        ],
        compiler_params=_params("parallel"),
        name="even_inproj",
    )(x, g, w, wf)


def _fox_prep_kernel(f_ref, bf_ref, tri_ref, place_ref, feat_ref, carry_ref):
    @pl.when(pl.program_id(1) == 0)
    def _():
        carry_ref[...] = jnp.zeros_like(carry_ref)

    z = f_ref[0] + bf_ref[...]
    logf = jnp.minimum(z, 0.0) - jnp.log1p(jnp.exp(-jnp.abs(z)))
    c3 = _dot(tri_ref[...], jnp.concatenate(_split3(logf), axis=1))
    csum = c3[:, 0:LANES] + c3[:, LANES:2 * LANES] + c3[:, 2 * LANES:3 * LANES] + carry_ref[...]
    tm = csum.shape[0]
    carry_ref[...] = csum[tm - 1:tm, :]
    pieces = jnp.concatenate(_split3(csum * LOG2E), axis=1)
    feat_ref[0] = _dot(pieces, place_ref[...]).astype(BF16)


def _fox_placement(n_heads):
    width = (n_heads // 2) * LANES
    place = np.zeros((3 * LANES, width), np.float32)
    ones = np.zeros((2, LANES), np.float32)
    for h in range(n_heads):
        base = HEAD_DIM if h % 2 == 0 else 0
        for j in range(3):
            place[j * LANES + h, (h // 2) * LANES + base + j] = 1.0
            place[j * LANES + h, (h // 2) * LANES + base + 3 + j] = -1.0
            ones[0, base + 3 + j] = 1.0
            ones[1, base + j] = 1.0
    return place, ones


def _fox_prep(fl, b_f):
    B, L, _ = fl.shape
    n_heads = b_f.shape[0]
    width = (n_heads // 2) * LANES
    tm = min(ROW_TILE, L)
    place, ones = _fox_placement(n_heads)
    tri = jnp.asarray(np.tril(np.ones((tm, tm), np.float32)), BF16)
    bf = jnp.zeros((1, LANES), F32).at[0, :n_heads].set(b_f)
    tok = lambda b, r: (b, r, 0)
    feat = pl.pallas_call(
        _fox_prep_kernel,
        grid=(B, L // tm),
        in_specs=[pl.BlockSpec((1, tm, LANES), tok), _const_spec((1, LANES)),
                  _const_spec((tm, tm)), _const_spec(place.shape)],
        out_specs=pl.BlockSpec((1, tm, width), tok),
        out_shape=jax.ShapeDtypeStruct((B, L, width), BF16),
        scratch_shapes=[pltpu.VMEM((1, LANES), F32)],
        compiler_params=_params("parallel", "arbitrary"),
        name="fox_prep",
    )(fl, bf, tri, jnp.asarray(place, BF16))
    return feat, jnp.asarray(ones)


def _fox_attn_kernel(q_ref, k_ref, v_ref, fq_ref, fk_ref, ones_ref, *refs):
    n_riders = (len(refs) - 6) // 2
    o_ref = refs[n_riders]
    kaug_ref, vaug_ref, qaug_ref, m_ref, acc_ref = refs[2 * n_riders + 1:]
    for src, dst in zip(refs[:n_riders], refs[n_riders + 1:2 * n_riders + 1]):
        dst[...] = src[...].astype(BF16)

    qi = pl.program_id(2)
    tq = q_ref.shape[1]
    L = k_ref.shape[1]
    tk = min(ATT_TK, L)
    n_heads = kaug_ref.shape[0]
    assert tq % tk == 0

    def merge(lanes, first, x_ref, f):
        x = x_ref[0, :, first * LANES:(first + 1) * LANES].astype(F32)
        return (jnp.where(lanes < HEAD_DIM, x, f(0)).astype(BF16),
                jnp.where(lanes >= HEAD_DIM, x, f(1)).astype(BF16))

    def features(f_ref, pr, side):
        one = ones_ref[side:side + 1, :]
        return f_ref[0, :, pr * LANES:(pr + 1) * LANES].astype(F32) * (1.0 - one) + one

    @pl.when(qi == 0)
    def _():
        lanes = lax.broadcasted_iota(jnp.int32, (L, LANES), 1)
        for pr in range(n_heads // 2):
            fk = features(fk_ref, pr, 1)
            kaug_ref[2 * pr], kaug_ref[2 * pr + 1] = merge(lanes, pr, k_ref, lambda e: fk)
            ones = ((lanes == HEAD_DIM).astype(F32), (lanes == 0).astype(F32))
            vaug_ref[2 * pr], vaug_ref[2 * pr + 1] = merge(lanes, pr, v_ref, lambda e: ones[e])

    lanes_q = lax.broadcasted_iota(jnp.int32, (tq, LANES), 1)
    for pr in range(n_heads // 2):
        fq = features(fq_ref, pr, 0)
        qaug_ref[2 * pr], qaug_ref[2 * pr + 1] = merge(lanes_q, pr, q_ref, lambda e: fq)

    def tile(j, r0, r1, masked, first=False):
        start = pl.multiple_of(j * tk, tk)
        for e in range(n_heads):
            kt = kaug_ref[e, pl.ds(start, tk), :]
            s = lax.dot_general(qaug_ref[e, r0:r1, :], kt, (((1,), (1,)), ((), ())),
                                preferred_element_type=F32)
            if masked:
                row = lax.broadcasted_iota(jnp.int32, s.shape, 0)
                col = lax.broadcasted_iota(jnp.int32, s.shape, 1)
                s = jnp.where(col <= row, s, NEG_BIG)
            m_new = jnp.broadcast_to(jnp.max(s, axis=1, keepdims=True), (r1 - r0, LANES))
            if not first:
                m = m_ref[e, r0:r1, :]
                m_new = jnp.maximum(m, m_new)
            p = jnp.exp2((s - jnp.concatenate([m_new] * (tk // LANES), axis=1)).astype(BF16))
            pv = _dot(p, vaug_ref[e, pl.ds(start, tk), :])
            acc_ref[e, r0:r1, :] = pv if first else jnp.exp2(m - m_new) * acc_ref[e, r0:r1, :] + pv
            m_ref[e, r0:r1, :] = m_new

    n_diag = tq // tk

    tile(qi * n_diag, 0, tk, True, first=True)
    if tk < tq:
        tile(qi * n_diag, tk, tq, False, first=True)

    def body(j, c):
        for d in range(n_diag):
            tile(j * n_diag + d, 0, tq, False)
        return c

    lax.fori_loop(0, qi, body, 0)
    for d in range(1, n_diag):
        tile(qi * n_diag + d, d * tk, (d + 1) * tk, True)
        if (d + 1) * tk < tq:
            tile(qi * n_diag + d, (d + 1) * tk, tq, False)

    for pr in range(n_heads // 2):
        acc0, acc1 = acc_ref[2 * pr], acc_ref[2 * pr + 1]
        out0 = acc0 / acc0[:, HEAD_DIM:HEAD_DIM + 1]
        out1 = acc1 / acc1[:, 0:1]
        o_ref[0, :, pr * LANES:(pr + 1) * LANES] = jnp.where(lanes_q < HEAD_DIM, out0, out1).astype(BF16)


def _fox_attn(q, k, v, feat, ones, riders):
    B, L, W = q.shape
    bw = ATT_PAIRS * LANES
    n_heads = 2 * ATT_PAIRS
    tq = min(ATT_TQ, L)
    grid = (B, W // bw, L // tq)
    n_steps = grid[0] * grid[1] * grid[2]
    qspec = pl.BlockSpec((1, tq, bw), lambda b, p, i: (b, i, p))
    kspec = pl.BlockSpec((1, L, bw), lambda b, p, i: (b, 0, p))
    step = lambda b, p, i: ((b * grid[1] + p) * grid[2] + i, 0)
    rspecs = [pl.BlockSpec((r.shape[0] // n_steps, r.shape[1]), step) for r in riders]
    out = pl.pallas_call(
        _fox_attn_kernel,
        grid=grid,
        in_specs=[qspec, kspec, kspec, qspec, kspec, _const_spec(ones.shape)] + rspecs,
        out_specs=[qspec] + rspecs,
        out_shape=[jax.ShapeDtypeStruct((B, L, W), BF16)]
                  + [jax.ShapeDtypeStruct(r.shape, BF16) for r in riders],
        scratch_shapes=[pltpu.VMEM((n_heads, L, LANES), BF16), pltpu.VMEM((n_heads, L, LANES), BF16),
                        pltpu.VMEM((n_heads, tq, LANES), BF16), pltpu.VMEM((n_heads, tq, LANES), F32),
                        pltpu.VMEM((n_heads, tq, LANES), F32)],
        compiler_params=_params("parallel", "parallel", "arbitrary"),
        name="fox_attn",
    )(q, k, v, feat, feat, ones, *riders)
    return out[0], out[1:]


def _s5_kernel(u0_ref, u1_ref, bst_ref, cst_ref, are_ref, aim_ref, d_ref, o0_ref, o1_ref,
               x_ref, st_ref):
    rows = u0_ref.shape[0]
    nb = x_ref.shape[0] // 2
    hrows = rows // 2

    @pl.when(pl.program_id(0) == 0)
    def _():
        st_ref[...] = jnp.zeros_like(st_ref)

    def half_rows(ref, h, *lead):
        return ref[(*lead, pl.ds(h, hrows, stride=2), slice(None))]

    u_h = []
    for h in range(2):
        u = jnp.concatenate([half_rows(u0_ref, h), half_rows(u1_ref, h)], axis=1)
        u_h.append(u)
        bu = _dot(u.astype(BF16), bst_ref[h])
        for c in range(2 * nb):
            x_ref[c, pl.ds(h, hrows, stride=2), :] = bu[:, c * LANES:(c + 1) * LANES]

    a_re = [are_ref[:, c * LANES:(c + 1) * LANES] for c in range(nb)]
    a_im = [aim_ref[:, c * LANES:(c + 1) * LANES] for c in range(nb)]

    def step(i, state):
        r = pl.ds(pl.multiple_of(i * SUBLANES, SUBLANES), SUBLANES)
        new = []
        for c in range(nb):
            s_re, s_im = state[2 * c], state[2 * c + 1]
            n_re = a_re[c] * s_re - a_im[c] * s_im + x_ref[c, r, :]
            n_im = a_re[c] * s_im + a_im[c] * s_re + x_ref[nb + c, r, :]
            x_ref[c, r, :] = n_re
            x_ref[nb + c, r, :] = n_im
            new += [n_re, n_im]
        return tuple(new)

    state = lax.fori_loop(0, rows // SUBLANES, step,
                          tuple(st_ref[c] for c in range(2 * nb)), unroll=4)
    for c in range(2 * nb):
        st_ref[c] = state[c]

    for h in range(2):
        xs = jnp.concatenate([half_rows(x_ref, h, c) for c in range(nb)]
                             + [half_rows(x_ref, h, nb + c) for c in range(nb)], axis=1)
        y = _dot(xs.astype(BF16), cst_ref[h])
        g = jax.nn.gelu(y + d_ref[h:h + 1, :] * u_h[h])
        o0_ref[pl.ds(h, hrows, stride=2), :] = g[:, 0:LANES]
        o1_ref[pl.ds(h, hrows, stride=2), :] = g[:, LANES:2 * LANES]


def _s5_discretize(lam_re, lam_im, log_dt, b_re, b_im):
    dt = jnp.exp(log_dt)[:, None]
    mag = jnp.exp(lam_re * dt)
    ab_re = mag * jnp.cos(lam_im * dt)
    ab_im = mag * jnp.sin(lam_im * dt)
    den = lam_re * lam_re + lam_im * lam_im
    nr = ab_re - 1.0
    ni = ab_im
    q_re = (nr * lam_re + ni * lam_im) / den
    q_im = (ni * lam_re - nr * lam_im) / den
    bb_re = q_re[..., None] * b_re - q_im[..., None] * b_im
    bb_im = q_re[..., None] * b_im + q_im[..., None] * b_re
    return ab_re, ab_im, bb_re, bb_im


def _s5_matrices(ab_re, ab_im, bb_re, bb_im, c_re, c_im, d, batch):
    G, P, H = bb_re.shape
    gh = G // 2
    hw, ns = gh * H, gh * P
    bmask = np.kron(np.eye(gh, dtype=np.float32), np.ones((H, P), np.float32))

    def bmat(bb):
        t = jnp.transpose(bb.reshape(2, gh, P, H), (0, 1, 3, 2)).reshape(2, hw, P)
        return jnp.tile(t, (1, 1, gh)) * bmask

    def cmat(cc):
        t = jnp.transpose(cc.reshape(2, gh, H, P), (0, 1, 3, 2)).reshape(2, ns, H)
        return jnp.tile(t, (1, 1, gh)) * bmask.T

    bst = jnp.concatenate([bmat(bb_re), bmat(bb_im)], axis=2).astype(BF16)
    cst = jnp.concatenate([cmat(c_re), -cmat(c_im)], axis=1).astype(BF16)
    a_re = jnp.tile(ab_re.reshape(2, ns), (batch, 1))
    a_im = jnp.tile(ab_im.reshape(2, ns), (batch, 1))
    return bst, cst, a_re, a_im, d.reshape(2, hw)


def _s5_scan(u0, u1, bst, cst, a_re, a_im, d2):
    n_rows = u0.shape[0]
    ns = a_re.shape[1]
    rows = min(S5_TILE * SUBLANES, n_rows)
    blk = pl.BlockSpec((rows, LANES), lambda t: (t, 0))
    return pl.pallas_call(
        _s5_kernel,
        grid=(n_rows // rows,),
        in_specs=[blk, blk, _const_spec(bst.shape), _const_spec(cst.shape),
                  _const_spec(a_re.shape), _const_spec(a_im.shape), _const_spec(d2.shape)],
        out_specs=[blk, blk],
        out_shape=[jax.ShapeDtypeStruct((n_rows, LANES), F32)] * 2,
        scratch_shapes=[pltpu.VMEM((2 * ns // LANES, rows, LANES), F32),
                        pltpu.VMEM((2 * ns // LANES, SUBLANES, LANES), F32)],
        compiler_params=_params("arbitrary"),
        name="s5_scan",
    )(u0, u1, bst, cst, a_re, a_im, d2)


def _mlp_rows(x, gpre_ref, gpost_ref, w1_ref, w2_ref):
    h = _rms(x, gpre_ref[...]).astype(BF16)
    d_ff = w1_ref.shape[1]
    acc = jnp.zeros(x.shape, F32)
    for f in range(0, d_ff, MLP_FF_CHUNK):
        a = jnp.maximum(_dot(h, w1_ref[:, f:f + MLP_FF_CHUNK]), 0.0)
        acc = acc + _dot((a * a).astype(BF16), w2_ref[f:f + MLP_FF_CHUNK, :])
    return x + _rms(acc, gpost_ref[...])


def _even_post_kernel(x_ref, g0_ref, g1_ref, yb_ref, wglu_ref, wout_ref, g_ref,
                      gpre_ref, gpost_ref, w1_ref, w2_ref, o_ref):
    nb, tm, _ = x_ref.shape
    half_w = wout_ref.shape[0] // 2
    x1 = []
    for b in range(nb):
        blocks = [ref[pl.ds(2 * b + half, tm, stride=2 * nb), :]
                  for half in range(2) for ref in (g0_ref, g1_ref)]
        ga = jnp.concatenate(blocks, axis=1)
        ya = ga * jax.nn.sigmoid(_dot(ga.astype(BF16), wglu_ref[...]))
        mix = (_dot(ya.astype(BF16), wout_ref[0:half_w, :])
               + _dot(yb_ref[b], wout_ref[half_w:2 * half_w, :]))
        x1.append(x_ref[b] + _rms(mix, g_ref[...]))
    out = _mlp_rows(jnp.concatenate(x1, axis=0), gpre_ref, gpost_ref, w1_ref, w2_ref)
    for b in range(nb):
        o_ref[b] = out[b * tm:(b + 1) * tm]


def _even_post(x, g0, g1, yb, w_glu, w_out, g, g_pre, g_post, w1s, w2s, layer):
    B, L, D = x.shape
    tm = min(POST_TILE, L)
    tok = lambda r: (0, r, 0)
    gblk = pl.BlockSpec((tm * 2 * B, LANES), lambda r: (r, 0))
    consts = (w_glu, w_out, g, g_pre, g_post)
    return pl.pallas_call(
        _even_post_kernel,
        grid=(L // tm,),
        in_specs=[pl.BlockSpec((B, tm, D), tok), gblk, gblk, pl.BlockSpec((B, tm, 512), tok)]
                 + [_const_spec(c.shape) for c in consts]
                 + [_layer_spec(w1s.shape, layer), _layer_spec(w2s.shape, layer)],
        out_specs=pl.BlockSpec((B, tm, D), tok),
        out_shape=jax.ShapeDtypeStruct((B, L, D), F32),
        compiler_params=_params("parallel"),
        name="even_post_mlp",
    )(x, g0, g1, yb, *consts, w1s, w2s)


def _odd_mixer_rows(x, t0, pool_ref, gpre_ref, win_ref, pw_ref, ps_ref, lng_ref, lnb_ref, ws,
                    bs_ref, wout_ref, gpost_ref):
    tm = x.shape[0]
    pad = max(POOL_WINDOWS)
    gd = LANES
    n_grp = len(POOL_WINDOWS)
    width = n_grp * gd

    h = _rms(x, gpre_ref[...]).astype(BF16)
    xc = _dot(h, win_ref[:, 0:width])
    u = _dot(h, win_ref[:, width:2 * width])
    v = _dot(h, win_ref[:, 2 * width:3 * width])

    pool_ref[pad:pad + tm, :] = xc
    t = t0 + lax.broadcasted_iota(jnp.int32, (tm, 1), 0)
    yc = []
    for g, w in enumerate(POOL_WINDOWS):
        cols = slice(g * gd, (g + 1) * gd)
        tot = xc[:, cols]
        for lag in range(1, w):
            tot = tot + pool_ref[pl.ds(pad - lag, tm), cols]
        cnt = jnp.minimum(t + 1, w).astype(F32)
        pooled = tot / cnt - xc[:, cols]
        yc.append(_dot(pooled.astype(BF16), pw_ref[g]))
    pool_ref[0:pad, :] = pool_ref[tm:tm + pad, :]
    y_c = jnp.concatenate(yc, axis=-1) * ps_ref[...]

    gv = jax.nn.gelu(v)
    mu = jnp.mean(gv, axis=-1, keepdims=True)
    vc = gv - mu
    vn = vc * lax.rsqrt(jnp.mean(vc * vc, axis=-1, keepdims=True) + EPS)
    vn = (vn * lng_ref[...] + lnb_ref[...]).astype(BF16)
    chunks = []
    for n in range(tm // CHUNK):
        parts = [_dot(ws[g], vn[n * CHUNK:(n + 1) * CHUNK, g * gd:(g + 1) * gd]) for g in range(n_grp)]
        chunks.append(jnp.concatenate(parts, axis=-1) + bs_ref[...])
    y_d = jax.nn.gelu(u) * jnp.concatenate(chunks, axis=0)

    mix = _dot(y_c.astype(BF16), wout_ref[0:width, :]) + _dot(y_d.astype(BF16), wout_ref[width:2 * width, :])
    return x + _rms(mix, gpost_ref[...])


def _odd_layer_kernel(x_ref, gpre_ref, win_ref, pw_ref, ps_ref, lng_ref, lnb_ref, ws_ref,
                      bs_ref, wout_ref, gpost_ref, mgpre_ref, mgpost_ref, w1_ref, w2_ref,
                      o_ref, pool_ref):
    r = pl.program_id(0)
    nb, tm, _ = x_ref.shape
    pad = max(POOL_WINDOWS)

    @pl.when(r == 0)
    def _():
        pool_ref[:, 0:pad, :] = jnp.zeros((nb, pad, pool_ref.shape[2]), F32)

    rows = lax.broadcasted_iota(jnp.int32, (CHUNK, CHUNK), 0)
    cols = lax.broadcasted_iota(jnp.int32, (CHUNK, CHUNK), 1)
    ws = [jnp.where(cols <= rows, ws_ref[g], 0.0).astype(BF16) for g in range(ws_ref.shape[0])]
    x1 = [_odd_mixer_rows(x_ref[b], r * tm, pool_ref.at[b], gpre_ref, win_ref, pw_ref, ps_ref,
                          lng_ref, lnb_ref, ws, bs_ref, wout_ref, gpost_ref) for b in range(nb)]
    out = _mlp_rows(jnp.concatenate(x1, axis=0), mgpre_ref, mgpost_ref, w1_ref, w2_ref)
    for b in range(nb):
        o_ref[b] = out[b * tm:(b + 1) * tm]


def _odd_layer_call(x, consts, w1s, w2s, layer):
    B, L, D = x.shape
    tm = min(POST_TILE, L)
    width = consts[3].shape[1]
    tok = lambda r: (0, r, 0)
    return pl.pallas_call(
        _odd_layer_kernel,
        grid=(L // tm,),
        in_specs=[pl.BlockSpec((B, tm, D), tok)] + [_const_spec(c.shape) for c in consts]
                 + [_layer_spec(w1s.shape, layer), _layer_spec(w2s.shape, layer)],
        out_specs=pl.BlockSpec((B, tm, D), tok),
        out_shape=jax.ShapeDtypeStruct((B, L, D), F32),
        scratch_shapes=[pltpu.VMEM((B, tm + max(POOL_WINDOWS), width), F32)],
        compiler_params=_params("arbitrary"),
        name="odd_layer",
    )(x, *consts, w1s, w2s)


def _even_layer(x, g_pre, g_post, w_in, lam_re, lam_im, log_dt, b_re, b_im, c_re, c_im, d,
                b_f, mlp_g_pre, mlp_g_post, late, i, layer):
    B, L, D = x.shape
    n_heads = b_f.shape[0]
    s5w = d.shape[0]
    fw = n_heads * HEAD_DIM
    assert s5w == 4 * LANES and w_in.shape[1] == s5w + 3 * fw + n_heads
    wf = jnp.pad(w_in[:, s5w + 3 * fw:], ((0, 0), (0, LANES - n_heads))).astype(BF16)
    u0, u1, q, k, v, fl = _even_inproj(x, g_pre[None], w_in.astype(BF16), wf)

    feat, ones = _fox_prep(fl, b_f)
    todo = [n for n, w in late.items() if w.dtype != BF16]
    y_b, cast = _fox_attn(q, k, v, feat, ones,
                          tuple(late[n].reshape(-1, late[n].shape[-1]) for n in todo))
    late = {**late, **{n: c.reshape(late[n].shape) for n, c in zip(todo, cast)}}

    ab_re, ab_im, bb_re, bb_im = _s5_discretize(lam_re, lam_im, log_dt, b_re, b_im)
    bst, cst, a_re, a_im, d2 = _s5_matrices(ab_re, ab_im, bb_re, bb_im, c_re, c_im, d, B)
    g0, g1 = _s5_scan(u0, u1, bst, cst, a_re, a_im, d2)

    x = _even_post(x, g0, g1, y_b, late["w_glu"][i], late["w_out_even"][i], g_post[None],
                   mlp_g_pre[None], mlp_g_post[None], late["mlp_w1"], late["mlp_w2"], layer)
    return x, late


def _odd_layer(x, g_pre, g_post, pool_scale, ln_g, ln_b, w_s, b_s, mlp_g_pre, mlp_g_post,
               late, i, layer):
    bs_full = jnp.repeat(jnp.transpose(b_s), LANES, axis=1)
    consts = (g_pre[None], late["w_in_odd"][i], late["pool_w"][i], pool_scale[None], ln_g[None],
              ln_b[None], w_s, bs_full, late["w_out_odd"][i], g_post[None], mlp_g_pre[None],
              mlp_g_post[None])
    return _odd_layer_call(x, consts, late["mlp_w1"], late["mlp_w2"], layer)


def kernel(x, mix_pre_g, mix_post_g, mlp_pre_g, mlp_post_g, w_in_even, s5_lam_re, s5_lam_im, s5_log_dt, s5_b_re, s5_b_im, s5_c_re, s5_c_im, s5_d, s5_w_glu, fox_b_f, w_out_even, w_in_odd, pool_w, pool_scale, sgu_ln_g, sgu_ln_b, sgu_w_s, sgu_b_s, w_out_odd, mlp_w1, mlp_w2):
    depth = mix_pre_g.shape[0]
    late = dict(mlp_w1=mlp_w1, mlp_w2=mlp_w2, w_glu=s5_w_glu, w_out_even=w_out_even,
                w_in_odd=w_in_odd, pool_w=pool_w, w_out_odd=w_out_odd)
    for l in range(depth):
        i = l // 2
        if l % 2 == 0:
            x, late = _even_layer(
                x, mix_pre_g[l], mix_post_g[l], w_in_even[i], s5_lam_re[i], s5_lam_im[i],
                s5_log_dt[i], s5_b_re[i], s5_b_im[i], s5_c_re[i], s5_c_im[i], s5_d[i],
                fox_b_f[i], mlp_pre_g[l], mlp_post_g[l], late, i, l)
        else:
            x = _odd_layer(x, mix_pre_g[l], mix_post_g[l], pool_scale[i], sgu_ln_g[i], sgu_ln_b[i],
                           sgu_w_s[i], sgu_b_s[i], mlp_pre_g[l], mlp_post_g[l], late, i, l)
    return x
```

```python
import functools

import numpy as np
import jax
import jax.numpy as jnp
from jax import lax
from jax.experimental import pallas as pl
from jax.experimental.pallas import tpu as pltpu

F32 = jnp.float32
BF16 = jnp.bfloat16

EPS = 1e-6
LOG2E = 1.4426950408889634
NEG_BIG = -1e30

LANES = 128
SUBLANES = 8
VMEM_LIMIT = 56 * 1024 * 1024

S5_GROUP = 16
S5_STATE = 64
HEAD_DIM = 64
POOL_WINDOWS = (2, 4, 8, 16)
CHUNK = 128

ROW_TILE = 512
POST_TILE = 256
MLP_FF_CHUNK = 1024
MLP_GROUP = 2
S5_TILE = 128
ATT_TQ = 1024
ATT_TK = 512
ATT_PAIRS = 2


def _params(*sem):
    return pltpu.CompilerParams(dimension_semantics=sem, vmem_limit_bytes=VMEM_LIMIT)


def _const_spec(shape):
    nd = len(shape)
    return pl.BlockSpec(shape, lambda *_: (0,) * nd, pipeline_mode=pl.Buffered(1))


def _layer_spec(shape, layer):
    nd = len(shape)
    return pl.BlockSpec((None,) + tuple(shape[1:]), lambda *_: (layer,) + (0,) * (nd - 1),
                        pipeline_mode=pl.Buffered(1))


def _rms(x, g):
    return x * lax.rsqrt(jnp.mean(x * x, axis=-1, keepdims=True) + EPS) * g


def _split3(x):
    hi = x.astype(BF16)
    r1 = x - hi.astype(F32)
    mid = r1.astype(BF16)
    lo = (r1 - mid.astype(F32)).astype(BF16)
    return hi, mid, lo


def _dot(a, b):
    return jnp.dot(a, b, preferred_element_type=F32)


def _even_inproj_kernel(x_ref, g_ref, w_ref, wf_ref, u0_ref, u1_ref, q_ref, k_ref, v_ref, f_ref):
    nb, tm, _ = x_ref.shape
    s5w = 4 * LANES
    fw = q_ref.shape[2]
    q_scale = HEAD_DIM ** -0.5 * LOG2E
    for b in range(nb):
        h = _rms(x_ref[b], g_ref[...]).astype(BF16)
        zu = _dot(h, w_ref[:, 0:s5w])
        for half in range(2):
            rows = pl.ds(2 * b + half, tm, stride=2 * nb)
            u0_ref[rows, :] = zu[:, 2 * half * LANES:(2 * half + 1) * LANES]
            u1_ref[rows, :] = zu[:, (2 * half + 1) * LANES:(2 * half + 2) * LANES]
        q_ref[b] = (_dot(h, w_ref[:, s5w:s5w + fw]) * q_scale).astype(BF16)
        k_ref[b] = _dot(h, w_ref[:, s5w + fw:s5w + 2 * fw]).astype(BF16)
        v_ref[b] = _dot(h, w_ref[:, s5w + 2 * fw:s5w + 3 * fw]).astype(BF16)
        f_ref[b] = _dot(h, wf_ref[...])


def _even_inproj(x, g, w, wf):
    B, L, D = x.shape
    tm = min(ROW_TILE, L)
    tok = lambda r: (0, r, 0)
    ublk = pl.BlockSpec((tm * 2 * B, LANES), lambda r: (r, 0))
    return pl.pallas_call(
        _even_inproj_kernel,
        grid=(L // tm,),
        in_specs=[pl.BlockSpec((B, tm, D), tok), _const_spec((1, D)), _const_spec(w.shape),
                  _const_spec(wf.shape)],
        out_specs=[
            ublk, ublk,
            pl.BlockSpec((B, tm, 512), tok),
            pl.BlockSpec((B, tm, 512), tok),
            pl.BlockSpec((B, tm, 512), tok),
            pl.BlockSpec((B, tm, LANES), tok),
        ],
        out_shape=[
            jax.ShapeDtypeStruct((L * 2 * B, LANES), F32),
            jax.ShapeDtypeStruct((L * 2 * B, LANES), F32),
            jax.ShapeDtypeStruct((B, L, 512), BF16),
            jax.ShapeDtypeStruct((B, L, 512), BF16),
            jax.ShapeDtypeStruct((B, L, 512), BF16),
            jax.ShapeDtypeStruct((B, L, LANES), F32),
        ],
        compiler_params=_params("parallel"),
        name="even_inproj",
    )(x, g, w, wf)


def _fox_prep_kernel(f_ref, bf_ref, tri_ref, place_ref, feat_ref, carry_ref):
    @pl.when(pl.program_id(1) == 0)
    def _():
        carry_ref[...] = jnp.zeros_like(carry_ref)

    z = f_ref[0] + bf_ref[...]
    logf = jnp.minimum(z, 0.0) - jnp.log1p(jnp.exp(-jnp.abs(z)))
    c3 = _dot(tri_ref[...], jnp.concatenate(_split3(logf), axis=1))
    csum = c3[:, 0:LANES] + c3[:, LANES:2 * LANES] + c3[:, 2 * LANES:3 * LANES] + carry_ref[...]
    tm = csum.shape[0]
    carry_ref[...] = csum[tm - 1:tm, :]
    pieces = jnp.concatenate(_split3(csum * LOG2E), axis=1)
    feat_ref[0] = _dot(pieces, place_ref[...]).astype(BF16)


def _fox_placement(n_heads):
    width = (n_heads // 2) * LANES
    place = np.zeros((3 * LANES, width), np.float32)
    ones = np.zeros((2, LANES), np.float32)
    for h in range(n_heads):
        base = HEAD_DIM if h % 2 == 0 else 0
        for j in range(3):
            place[j * LANES + h, (h // 2) * LANES + base + j] = 1.0
            place[j * LANES + h, (h // 2) * LANES + base + 3 + j] = -1.0
            ones[0, base + 3 + j] = 1.0
            ones[1, base + j] = 1.0
    return place, ones


def _fox_prep(fl, b_f):
    B, L, _ = fl.shape
    n_heads = b_f.shape[0]
    width = (n_heads // 2) * LANES
    tm = min(ROW_TILE, L)
    place, ones = _fox_placement(n_heads)
    tri = jnp.asarray(np.tril(np.ones((tm, tm), np.float32)), BF16)
    bf = jnp.zeros((1, LANES), F32).at[0, :n_heads].set(b_f)
    tok = lambda b, r: (b, r, 0)
    feat = pl.pallas_call(
        _fox_prep_kernel,
        grid=(B, L // tm),
        in_specs=[pl.BlockSpec((1, tm, LANES), tok), _const_spec((1, LANES)),
                  _const_spec((tm, tm)), _const_spec(place.shape)],
        out_specs=pl.BlockSpec((1, tm, width), tok),
        out_shape=jax.ShapeDtypeStruct((B, L, width), BF16),
        scratch_shapes=[pltpu.VMEM((1, LANES), F32)],
        compiler_params=_params("parallel", "arbitrary"),
        name="fox_prep",
    )(fl, bf, tri, jnp.asarray(place, BF16))
    return feat, jnp.asarray(ones)


def _fox_attn_kernel(q_ref, k_ref, v_ref, fq_ref, fk_ref, ones_ref, *refs):
    n_riders = (len(refs) - 6) // 2
    o_ref = refs[n_riders]
    kaug_ref, vaug_ref, qaug_ref, m_ref, acc_ref = refs[2 * n_riders + 1:]
    for src, dst in zip(refs[:n_riders], refs[n_riders + 1:2 * n_riders + 1]):
        dst[...] = src[...].astype(BF16)

    qi = pl.program_id(2)
    tq = q_ref.shape[1]
    L = k_ref.shape[1]
    tk = min(ATT_TK, L)
    n_heads = kaug_ref.shape[0]
    assert tq % tk == 0

    def merge(lanes, first, x_ref, f):
        x = x_ref[0, :, first * LANES:(first + 1) * LANES].astype(F32)
        return (jnp.where(lanes < HEAD_DIM, x, f(0)).astype(BF16),
                jnp.where(lanes >= HEAD_DIM, x, f(1)).astype(BF16))

    def features(f_ref, pr, side):
        one = ones_ref[side:side + 1, :]
        return f_ref[0, :, pr * LANES:(pr + 1) * LANES].astype(F32) * (1.0 - one) + one

    @pl.when(qi == 0)
    def _():
        lanes = lax.broadcasted_iota(jnp.int32, (L, LANES), 1)
        for pr in range(n_heads // 2):
            fk = features(fk_ref, pr, 1)
            kaug_ref[2 * pr], kaug_ref[2 * pr + 1] = merge(lanes, pr, k_ref, lambda e: fk)
            ones = ((lanes == HEAD_DIM).astype(F32), (lanes == 0).astype(F32))
            vaug_ref[2 * pr], vaug_ref[2 * pr + 1] = merge(lanes, pr, v_ref, lambda e: ones[e])

    lanes_q = lax.broadcasted_iota(jnp.int32, (tq, LANES), 1)
    for pr in range(n_heads // 2):
        fq = features(fq_ref, pr, 0)
        qaug_ref[2 * pr], qaug_ref[2 * pr + 1] = merge(lanes_q, pr, q_ref, lambda e: fq)

    def tile(j, r0, r1, masked, first=False):
        start = pl.multiple_of(j * tk, tk)
        for e in range(n_heads):
            kt = kaug_ref[e, pl.ds(start, tk), :]
            s = lax.dot_general(qaug_ref[e, r0:r1, :], kt, (((1,), (1,)), ((), ())),
                                preferred_element_type=F32)
            if masked:
                row = lax.broadcasted_iota(jnp.int32, s.shape, 0)
                col = lax.broadcasted_iota(jnp.int32, s.shape, 1)
                s = jnp.where(col <= row, s, NEG_BIG)
            m_new = jnp.broadcast_to(jnp.max(s, axis=1, keepdims=True), (r1 - r0, LANES))
            if not first:
                m = m_ref[e, r0:r1, :]
                m_new = jnp.maximum(m, m_new)
            p = jnp.exp2((s - jnp.concatenate([m_new] * (tk // LANES), axis=1)).astype(BF16))
            pv = _dot(p, vaug_ref[e, pl.ds(start, tk), :])
            acc_ref[e, r0:r1, :] = pv if first else jnp.exp2(m - m_new) * acc_ref[e, r0:r1, :] + pv
            m_ref[e, r0:r1, :] = m_new

    n_diag = tq // tk

    tile(qi * n_diag, 0, tk, True, first=True)
    if tk < tq:
        tile(qi * n_diag, tk, tq, False, first=True)

    def body(j, c):
        for d in range(n_diag):
            tile(j * n_diag + d, 0, tq, False)
        return c

    lax.fori_loop(0, qi, body, 0)
    for d in range(1, n_diag):
        tile(qi * n_diag + d, d * tk, (d + 1) * tk, True)
        if (d + 1) * tk < tq:
            tile(qi * n_diag + d, (d + 1) * tk, tq, False)

    for pr in range(n_heads // 2):
        acc0, acc1 = acc_ref[2 * pr], acc_ref[2 * pr + 1]
        out0 = acc0 / acc0[:, HEAD_DIM:HEAD_DIM + 1]
        out1 = acc1 / acc1[:, 0:1]
        o_ref[0, :, pr * LANES:(pr + 1) * LANES] = jnp.where(lanes_q < HEAD_DIM, out0, out1).astype(BF16)


def _fox_attn(q, k, v, feat, ones, riders):
    B, L, W = q.shape
    bw = ATT_PAIRS * LANES
    n_heads = 2 * ATT_PAIRS
    tq = min(ATT_TQ, L)
    grid = (B, W // bw, L // tq)
    n_steps = grid[0] * grid[1] * grid[2]
    qspec = pl.BlockSpec((1, tq, bw), lambda b, p, i: (b, i, p))
    kspec = pl.BlockSpec((1, L, bw), lambda b, p, i: (b, 0, p))
    step = lambda b, p, i: ((b * grid[1] + p) * grid[2] + i, 0)
    rspecs = [pl.BlockSpec((r.shape[0] // n_steps, r.shape[1]), step) for r in riders]
    out = pl.pallas_call(
        _fox_attn_kernel,
        grid=grid,
        in_specs=[qspec, kspec, kspec, qspec, kspec, _const_spec(ones.shape)] + rspecs,
        out_specs=[qspec] + rspecs,
        out_shape=[jax.ShapeDtypeStruct((B, L, W), BF16)]
                  + [jax.ShapeDtypeStruct(r.shape, BF16) for r in riders],
        scratch_shapes=[pltpu.VMEM((n_heads, L, LANES), BF16), pltpu.VMEM((n_heads, L, LANES), BF16),
                        pltpu.VMEM((n_heads, tq, LANES), BF16), pltpu.VMEM((n_heads, tq, LANES), F32),
                        pltpu.VMEM((n_heads, tq, LANES), F32)],
        compiler_params=_params("parallel", "parallel", "arbitrary"),
        name="fox_attn",
    )(q, k, v, feat, feat, ones, *riders)
    return out[0], out[1:]


def _s5_kernel(u0_ref, u1_ref, bst_ref, cst_ref, are_ref, aim_ref, d_ref, o0_ref, o1_ref,
               x_ref, st_ref):
    rows = u0_ref.shape[0]
    nb = x_ref.shape[0] // 2
    hrows = rows // 2

    @pl.when(pl.program_id(0) == 0)
    def _():
        st_ref[...] = jnp.zeros_like(st_ref)

    def half_rows(ref, h, *lead):
        return ref[(*lead, pl.ds(h, hrows, stride=2), slice(None))]

    u_h = []
    for h in range(2):
        u = jnp.concatenate([half_rows(u0_ref, h), half_rows(u1_ref, h)], axis=1)
        u_h.append(u)
        bu = _dot(u.astype(BF16), bst_ref[h])
        for c in range(2 * nb):
            x_ref[c, pl.ds(h, hrows, stride=2), :] = bu[:, c * LANES:(c + 1) * LANES]

    a_re = [are_ref[:, c * LANES:(c + 1) * LANES] for c in range(nb)]
    a_im = [aim_ref[:, c * LANES:(c + 1) * LANES] for c in range(nb)]

    def step(i, state):
        r = pl.ds(pl.multiple_of(i * SUBLANES, SUBLANES), SUBLANES)
        new = []
        for c in range(nb):
            s_re, s_im = state[2 * c], state[2 * c + 1]
            n_re = a_re[c] * s_re - a_im[c] * s_im + x_ref[c, r, :]
            n_im = a_re[c] * s_im + a_im[c] * s_re + x_ref[nb + c, r, :]
            x_ref[c, r, :] = n_re
            x_ref[nb + c, r, :] = n_im
            new += [n_re, n_im]
        return tuple(new)

    state = lax.fori_loop(0, rows // SUBLANES, step,
                          tuple(st_ref[c] for c in range(2 * nb)), unroll=4)
    for c in range(2 * nb):
        st_ref[c] = state[c]

    for h in range(2):
        xs = jnp.concatenate([half_rows(x_ref, h, c) for c in range(nb)]
                             + [half_rows(x_ref, h, nb + c) for c in range(nb)], axis=1)
        y = _dot(xs.astype(BF16), cst_ref[h])
        g = jax.nn.gelu(y + d_ref[h:h + 1, :] * u_h[h])
        o0_ref[pl.ds(h, hrows, stride=2), :] = g[:, 0:LANES]
        o1_ref[pl.ds(h, hrows, stride=2), :] = g[:, LANES:2 * LANES]


def _s5_discretize(lam_re, lam_im, log_dt, b_re, b_im):
    dt = jnp.exp(log_dt)[:, None]
    mag = jnp.exp(lam_re * dt)
    ab_re = mag * jnp.cos(lam_im * dt)
    ab_im = mag * jnp.sin(lam_im * dt)
    den = lam_re * lam_re + lam_im * lam_im
    nr = ab_re - 1.0
    ni = ab_im
    q_re = (nr * lam_re + ni * lam_im) / den
    q_im = (ni * lam_re - nr * lam_im) / den
    bb_re = q_re[..., None] * b_re - q_im[..., None] * b_im
    bb_im = q_re[..., None] * b_im + q_im[..., None] * b_re
    return ab_re, ab_im, bb_re, bb_im


def _s5_matrices(ab_re, ab_im, bb_re, bb_im, c_re, c_im, d, batch):
    G, P, H = bb_re.shape
    gh = G // 2
    hw, ns = gh * H, gh * P
    bmask = np.kron(np.eye(gh, dtype=np.float32), np.ones((H, P), np.float32))

    def bmat(bb):
        t = jnp.transpose(bb.reshape(2, gh, P, H), (0, 1, 3, 2)).reshape(2, hw, P)
        return jnp.tile(t, (1, 1, gh)) * bmask

    def cmat(cc):
        t = jnp.transpose(cc.reshape(2, gh, H, P), (0, 1, 3, 2)).reshape(2, ns, H)
        return jnp.tile(t, (1, 1, gh)) * bmask.T

    bst = jnp.concatenate([bmat(bb_re), bmat(bb_im)], axis=2).astype(BF16)
    cst = jnp.concatenate([cmat(c_re), -cmat(c_im)], axis=1).astype(BF16)
    a_re = jnp.tile(ab_re.reshape(2, ns), (batch, 1))
    a_im = jnp.tile(ab_im.reshape(2, ns), (batch, 1))
    return bst, cst, a_re, a_im, d.reshape(2, hw)


def _s5_scan(u0, u1, bst, cst, a_re, a_im, d2):
    n_rows = u0.shape[0]
    ns = a_re.shape[1]
    rows = min(S5_TILE * SUBLANES, n_rows)
    blk = pl.BlockSpec((rows, LANES), lambda t: (t, 0))
    return pl.pallas_call(
        _s5_kernel,
        grid=(n_rows // rows,),
        in_specs=[blk, blk, _const_spec(bst.shape), _const_spec(cst.shape),
                  _const_spec(a_re.shape), _const_spec(a_im.shape), _const_spec(d2.shape)],
        out_specs=[blk, blk],
        out_shape=[jax.ShapeDtypeStruct((n_rows, LANES), F32)] * 2,
        scratch_shapes=[pltpu.VMEM((2 * ns // LANES, rows, LANES), F32),
                        pltpu.VMEM((2 * ns // LANES, SUBLANES, LANES), F32)],
        compiler_params=_params("arbitrary"),
        name="s5_scan",
    )(u0, u1, bst, cst, a_re, a_im, d2)


def _mlp_rows(x, gpre_ref, gpost_ref, w1_ref, w2_ref):
    h = _rms(x, gpre_ref[...]).astype(BF16)
    d_ff = w1_ref.shape[1]
    acc = jnp.zeros(x.shape, F32)
    for f in range(0, d_ff, MLP_FF_CHUNK):
        a = jnp.maximum(_dot(h, w1_ref[:, f:f + MLP_FF_CHUNK]), 0.0)
        acc = acc + _dot((a * a).astype(BF16), w2_ref[f:f + MLP_FF_CHUNK, :])
    return x + _rms(acc, gpost_ref[...])


def _even_post_kernel(x_ref, g0_ref, g1_ref, yb_ref, wglu_ref, wout_ref, g_ref,
                      gpre_ref, gpost_ref, w1_ref, w2_ref, o_ref):
    nb, tm, _ = x_ref.shape
    half_w = wout_ref.shape[0] // 2

    def mixer_tail(b):
        blocks = [ref[pl.ds(2 * b + half, tm, stride=2 * nb), :]
                  for half in range(2) for ref in (g0_ref, g1_ref)]
        ga = jnp.concatenate(blocks, axis=1)
        ya = ga * jax.nn.sigmoid(_dot(ga.astype(BF16), wglu_ref[...]))
        mix = (_dot(ya.astype(BF16), wout_ref[0:half_w, :])
               + _dot(yb_ref[b], wout_ref[half_w:2 * half_w, :]))
        return x_ref[b] + _rms(mix, g_ref[...])

    x1 = jnp.concatenate([mixer_tail(b) for b in range(nb)], axis=0)
    out = _mlp_rows(x1, gpre_ref, gpost_ref, w1_ref, w2_ref)
    for b in range(nb):
        o_ref[b] = out[b * tm:(b + 1) * tm]


def _even_post(x, g0, g1, yb, w_glu, w_out, g, g_pre, g_post, w1s, w2s, layer):
    B, L, D = x.shape
    tm = min(POST_TILE, L)
    tok = lambda r: (0, r, 0)
    gblk = pl.BlockSpec((tm * 2 * B, LANES), lambda r: (r, 0))
    consts = (w_glu, w_out, g, g_pre, g_post)
    return pl.pallas_call(
        _even_post_kernel,
        grid=(L // tm,),
        in_specs=[pl.BlockSpec((B, tm, D), tok), gblk, gblk, pl.BlockSpec((B, tm, 512), tok)]
                 + [_const_spec(c.shape) for c in consts]
                 + [_layer_spec(w1s.shape, layer), _layer_spec(w2s.shape, layer)],
        out_specs=pl.BlockSpec((B, tm, D), tok),
        out_shape=jax.ShapeDtypeStruct((B, L, D), F32),
        compiler_params=_params("parallel"),
        name="even_post_mlp",
    )(x, g0, g1, yb, *consts, w1s, w2s)


def _odd_inproj_rows(x, gpre_ref, win_ref):
    width = win_ref.shape[1] // 3
    h = _rms(x, gpre_ref[...]).astype(BF16)
    return tuple(_dot(h, win_ref[:, n * width:(n + 1) * width]) for n in range(3))


def _odd_mix_rows(x, xc, u, v, t0, pool_ref, pw_ref, ps_ref, lng_ref, lnb_ref, ws, bs_ref,
                  wout_ref, gpost_ref):
    tm = x.shape[0]
    pad = max(POOL_WINDOWS)
    gd = LANES
    n_grp = len(POOL_WINDOWS)
    width = n_grp * gd

    pool_ref[pad:pad + tm, :] = xc
    t = t0 + lax.broadcasted_iota(jnp.int32, (tm, 1), 0)
    yc = []
    for g, w in enumerate(POOL_WINDOWS):
        cols = slice(g * gd, (g + 1) * gd)
        tot = xc[:, cols]
        for lag in range(1, w):
            tot = tot + pool_ref[pl.ds(pad - lag, tm), cols]
        cnt = jnp.minimum(t + 1, w).astype(F32)
        pooled = tot / cnt - xc[:, cols]
        yc.append(_dot(pooled.astype(BF16), pw_ref[g]))
    pool_ref[0:pad, :] = pool_ref[tm:tm + pad, :]
    y_c = jnp.concatenate(yc, axis=-1) * ps_ref[...]

    gv = jax.nn.gelu(v)
    mu = jnp.mean(gv, axis=-1, keepdims=True)
    vc = gv - mu
    vn = vc * lax.rsqrt(jnp.mean(vc * vc, axis=-1, keepdims=True) + EPS)
    vn = (vn * lng_ref[...] + lnb_ref[...]).astype(BF16)
    chunks = []
    for n in range(tm // CHUNK):
        parts = [_dot(ws[g], vn[n * CHUNK:(n + 1) * CHUNK, g * gd:(g + 1) * gd]) for g in range(n_grp)]
        chunks.append(jnp.concatenate(parts, axis=-1) + bs_ref[...])
    y_d = jax.nn.gelu(u) * jnp.concatenate(chunks, axis=0)

    mix = _dot(y_c.astype(BF16), wout_ref[0:width, :]) + _dot(y_d.astype(BF16), wout_ref[width:2 * width, :])
    return x + _rms(mix, gpost_ref[...])


def _odd_layer_kernel(x_ref, gpre_ref, win_ref, pw_ref, ps_ref, lng_ref, lnb_ref, ws_ref,
                      bs_ref, wout_ref, gpost_ref, mgpre_ref, mgpost_ref, w1_ref, w2_ref,
                      o_ref, pool_ref):
    r = pl.program_id(0)
    nb, tm, _ = x_ref.shape
    pad = max(POOL_WINDOWS)

    @pl.when(r == 0)
    def _():
        pool_ref[:, 0:pad, :] = jnp.zeros((nb, pad, pool_ref.shape[2]), F32)

    rows = lax.broadcasted_iota(jnp.int32, (CHUNK, CHUNK), 0)
    cols = lax.broadcasted_iota(jnp.int32, (CHUNK, CHUNK), 1)
    ws = [jnp.where(cols <= rows, ws_ref[g], 0.0).astype(BF16) for g in range(ws_ref.shape[0])]
    proj = {0: _odd_inproj_rows(x_ref[0], gpre_ref, win_ref)}
    x1 = []
    for b in range(nb):
        if b + 1 < nb:
            proj[b + 1] = _odd_inproj_rows(x_ref[b + 1], gpre_ref, win_ref)
        if len(x1) == MLP_GROUP:
            out = _mlp_rows(jnp.concatenate(x1, axis=0), mgpre_ref, mgpost_ref, w1_ref, w2_ref)
            for n in range(MLP_GROUP):
                o_ref[b - MLP_GROUP + n] = out[n * tm:(n + 1) * tm]
            x1 = []
        x1.append(_odd_mix_rows(x_ref[b], *proj.pop(b), r * tm, pool_ref.at[b], pw_ref, ps_ref,
                                lng_ref, lnb_ref, ws, bs_ref, wout_ref, gpost_ref))
    out = _mlp_rows(jnp.concatenate(x1, axis=0), mgpre_ref, mgpost_ref, w1_ref, w2_ref)
    for n in range(len(x1)):
        o_ref[nb - len(x1) + n] = out[n * tm:(n + 1) * tm]


def _odd_layer_call(x, consts, w1s, w2s, layer):
    B, L, D = x.shape
    tm = min(POST_TILE, L)
    width = consts[3].shape[1]
    tok = lambda r: (0, r, 0)
    return pl.pallas_call(
        _odd_layer_kernel,
        grid=(L // tm,),
        in_specs=[pl.BlockSpec((B, tm, D), tok)] + [_const_spec(c.shape) for c in consts]
                 + [_layer_spec(w1s.shape, layer), _layer_spec(w2s.shape, layer)],
        out_specs=pl.BlockSpec((B, tm, D), tok),
        out_shape=jax.ShapeDtypeStruct((B, L, D), F32),
        scratch_shapes=[pltpu.VMEM((B, tm + max(POOL_WINDOWS), width), F32)],
        compiler_params=_params("arbitrary"),
        name="odd_layer",
    )(x, *consts, w1s, w2s)


def _even_layer(x, g_pre, g_post, w_in, lam_re, lam_im, log_dt, b_re, b_im, c_re, c_im, d,
                b_f, mlp_g_pre, mlp_g_post, late, i, layer):
    B, L, D = x.shape
    n_heads = b_f.shape[0]
    s5w = d.shape[0]
    fw = n_heads * HEAD_DIM
    assert s5w == 4 * LANES and w_in.shape[1] == s5w + 3 * fw + n_heads
    wf = jnp.pad(w_in[:, s5w + 3 * fw:], ((0, 0), (0, LANES - n_heads))).astype(BF16)
    u0, u1, q, k, v, fl = _even_inproj(x, g_pre[None], w_in.astype(BF16), wf)

    feat, ones = _fox_prep(fl, b_f)
    todo = [n for n, w in late.items() if w.dtype != BF16]
    y_b, cast = _fox_attn(q, k, v, feat, ones,
                          tuple(late[n].reshape(-1, late[n].shape[-1]) for n in todo))
    late = {**late, **{n: c.reshape(late[n].shape) for n, c in zip(todo, cast)}}

    ab_re, ab_im, bb_re, bb_im = _s5_discretize(lam_re, lam_im, log_dt, b_re, b_im)
    bst, cst, a_re, a_im, d2 = _s5_matrices(ab_re, ab_im, bb_re, bb_im, c_re, c_im, d, B)
    g0, g1 = _s5_scan(u0, u1, bst, cst, a_re, a_im, d2)

    x = _even_post(x, g0, g1, y_b, late["w_glu"][i], late["w_out_even"][i], g_post[None],
                   mlp_g_pre[None], mlp_g_post[None], late["mlp_w1"], late["mlp_w2"], layer)
    return x, late


def _odd_layer(x, g_pre, g_post, pool_scale, ln_g, ln_b, w_s, b_s, mlp_g_pre, mlp_g_post,
               late, i, layer):
    bs_full = jnp.repeat(jnp.transpose(b_s), LANES, axis=1)
    consts = (g_pre[None], late["w_in_odd"][i], late["pool_w"][i], pool_scale[None], ln_g[None],
              ln_b[None], w_s, bs_full, late["w_out_odd"][i], g_post[None], mlp_g_pre[None],
              mlp_g_post[None])
    return _odd_layer_call(x, consts, late["mlp_w1"], late["mlp_w2"], layer)


def kernel(x, mix_pre_g, mix_post_g, mlp_pre_g, mlp_post_g, w_in_even, s5_lam_re, s5_lam_im, s5_log_dt, s5_b_re, s5_b_im, s5_c_re, s5_c_im, s5_d, s5_w_glu, fox_b_f, w_out_even, w_in_odd, pool_w, pool_scale, sgu_ln_g, sgu_ln_b, sgu_w_s, sgu_b_s, w_out_odd, mlp_w1, mlp_w2):
    depth = mix_pre_g.shape[0]
    late = dict(mlp_w1=mlp_w1, mlp_w2=mlp_w2, w_glu=s5_w_glu, w_out_even=w_out_even,
                w_in_odd=w_in_odd, pool_w=pool_w, w_out_odd=w_out_odd)
    for l in range(depth):
        i = l // 2
        if l % 2 == 0:
            x, late = _even_layer(
                x, mix_pre_g[l], mix_post_g[l], w_in_even[i], s5_lam_re[i], s5_lam_im[i],
                s5_log_dt[i], s5_b_re[i], s5_b_im[i], s5_c_re[i], s5_c_im[i], s5_d[i],
                fox_b_f[i], mlp_pre_g[l], mlp_post_g[l], late, i, l)
        else:
            x = _odd_layer(x, mix_pre_g[l], mix_post_g[l], pool_scale[i], sgu_ln_g[i], sgu_ln_b[i],
                           sgu_w_s[i], sgu_b_s[i], mlp_pre_g[l], mlp_post_g[l], late, i, l)
    return x
```

```python
import functools

import numpy as np
import jax
import jax.numpy as jnp
from jax import lax
from jax.experimental import pallas as pl
from jax.experimental.pallas import tpu as pltpu

F32 = jnp.float32
BF16 = jnp.bfloat16

EPS = 1e-6
LOG2E = 1.4426950408889634
NEG_BIG = -1e30

LANES = 128
SUBLANES = 8
VMEM_LIMIT = 56 * 1024 * 1024

S5_GROUP = 16
S5_STATE = 64
HEAD_DIM = 64
POOL_WINDOWS = (2, 4, 8, 16)
CHUNK = 128

ROW_TILE = 512
POST_TILE = 256
MLP_FF_CHUNK = 1024
MLP_GROUP = 2
S5_TILE = 256
ATT_TQ = 1024
ATT_TK = 512
ATT_PAIRS = 2


def _params(*sem):
    return pltpu.CompilerParams(dimension_semantics=sem, vmem_limit_bytes=VMEM_LIMIT)


def _const_spec(shape):
    nd = len(shape)
    return pl.BlockSpec(shape, lambda *_: (0,) * nd, pipeline_mode=pl.Buffered(1))


def _layer_spec(shape, layer):
    nd = len(shape)
    return pl.BlockSpec((None,) + tuple(shape[1:]), lambda *_: (layer,) + (0,) * (nd - 1),
                        pipeline_mode=pl.Buffered(1))


def _rms(x, g):
    return x * lax.rsqrt(jnp.mean(x * x, axis=-1, keepdims=True) + EPS) * g


def _split3(x):
    hi = x.astype(BF16)
    r1 = x - hi.astype(F32)
    mid = r1.astype(BF16)
    lo = (r1 - mid.astype(F32)).astype(BF16)
    return hi, mid, lo


def _dot(a, b):
    return jnp.dot(a, b, preferred_element_type=F32)


def _even_inproj_kernel(x_ref, g_ref, w_ref, wf_ref, bf_ref, tri_ref, place_ref,
                        u0_ref, u1_ref, q_ref, k_ref, v_ref, feat_ref, carry_ref):
    nb, tm, _ = x_ref.shape
    s5w = 4 * LANES
    fw = q_ref.shape[2]
    q_scale = HEAD_DIM ** -0.5 * LOG2E

    @pl.when(pl.program_id(0) == 0)
    def _():
        carry_ref[...] = jnp.zeros_like(carry_ref)

    def running_sum(b, z):
        logf = jnp.minimum(z, 0.0) - jnp.log1p(jnp.exp(-jnp.abs(z)))
        c3 = _dot(tri_ref[...], jnp.concatenate(_split3(logf), axis=1))
        csum = (c3[:, 0:LANES] + c3[:, LANES:2 * LANES] + c3[:, 2 * LANES:3 * LANES]
                + carry_ref[b:b + 1, :])
        carry_ref[b:b + 1, :] = csum[tm - 1:tm, :]
        return csum

    def place(b, csum):
        pieces = jnp.concatenate(_split3(csum * LOG2E), axis=1)
        feat_ref[b] = _dot(pieces, place_ref[...]).astype(BF16)

    z_prev = None
    for b in range(nb):
        h = _rms(x_ref[b], g_ref[...]).astype(BF16)
        z = _dot(h, wf_ref[...]) + bf_ref[...]
        zu = _dot(h, w_ref[:, 0:s5w])
        for half in range(2):
            rows = pl.ds(2 * b + half, tm, stride=2 * nb)
            u0_ref[rows, :] = zu[:, 2 * half * LANES:(2 * half + 1) * LANES]
            u1_ref[rows, :] = zu[:, (2 * half + 1) * LANES:(2 * half + 2) * LANES]
        if z_prev is not None:
            csum = running_sum(b - 1, z_prev)
        q_ref[b] = (_dot(h, w_ref[:, s5w:s5w + fw]) * q_scale).astype(BF16)
        k_ref[b] = _dot(h, w_ref[:, s5w + fw:s5w + 2 * fw]).astype(BF16)
        if z_prev is not None:
            place(b - 1, csum)
        v_ref[b] = _dot(h, w_ref[:, s5w + 2 * fw:s5w + 3 * fw]).astype(BF16)
        z_prev = z
    place(nb - 1, running_sum(nb - 1, z_prev))


def _even_inproj(x, g, w, wf, b_f):
    B, L, D = x.shape
    n_heads = b_f.shape[0]
    tm = min(ROW_TILE, L)
    tok = lambda r: (0, r, 0)
    ublk = pl.BlockSpec((tm * 2 * B, LANES), lambda r: (r, 0))
    aspec = pl.BlockSpec((B, tm, 512), tok)
    place, ones = _fox_placement(n_heads)
    tri = jnp.asarray(np.tril(np.ones((tm, tm), np.float32)), BF16)
    bf = jnp.zeros((1, LANES), F32).at[0, :n_heads].set(b_f)
    consts = (g, w, wf, bf, tri, jnp.asarray(place, BF16))
    out = pl.pallas_call(
        _even_inproj_kernel,
        grid=(L // tm,),
        in_specs=[pl.BlockSpec((B, tm, D), tok)] + [_const_spec(c.shape) for c in consts],
        out_specs=[ublk, ublk, aspec, aspec, aspec, pl.BlockSpec((B, tm, place.shape[1]), tok)],
        out_shape=[
            jax.ShapeDtypeStruct((L * 2 * B, LANES), F32),
            jax.ShapeDtypeStruct((L * 2 * B, LANES), F32),
            jax.ShapeDtypeStruct((B, L, 512), BF16),
            jax.ShapeDtypeStruct((B, L, 512), BF16),
            jax.ShapeDtypeStruct((B, L, 512), BF16),
            jax.ShapeDtypeStruct((B, L, place.shape[1]), BF16),
        ],
        scratch_shapes=[pltpu.VMEM((SUBLANES, LANES), F32)],
        compiler_params=_params("arbitrary"),
        name="even_inproj",
    )(x, *consts)
    return (*out, jnp.asarray(ones))


def _fox_placement(n_heads):
    width = (n_heads // 2) * LANES
    place = np.zeros((3 * LANES, width), np.float32)
    ones = np.zeros((2, LANES), np.float32)
    for h in range(n_heads):
        base = HEAD_DIM if h % 2 == 0 else 0
        for j in range(3):
            place[j * LANES + h, (h // 2) * LANES + base + j] = 1.0
            place[j * LANES + h, (h // 2) * LANES + base + 3 + j] = -1.0
            ones[0, base + 3 + j] = 1.0
            ones[1, base + j] = 1.0
    return place, ones


def _fox_attn_kernel(q_ref, k_ref, v_ref, fq_ref, fk_ref, ones_ref, *refs):
    n_riders = (len(refs) - 6) // 2
    o_ref = refs[n_riders]
    kaug_ref, vaug_ref, qaug_ref, m_ref, acc_ref = refs[2 * n_riders + 1:]
    for src, dst in zip(refs[:n_riders], refs[n_riders + 1:2 * n_riders + 1]):
        dst[...] = src[...].astype(BF16)

    qi = pl.program_id(2)
    tq = q_ref.shape[1]
    L = k_ref.shape[1]
    tk = min(ATT_TK, L)
    n_heads = kaug_ref.shape[0]
    assert tq % tk == 0

    def merge(lanes, first, x_ref, f):
        x = x_ref[0, :, first * LANES:(first + 1) * LANES].astype(F32)
        return (jnp.where(lanes < HEAD_DIM, x, f(0)).astype(BF16),
                jnp.where(lanes >= HEAD_DIM, x, f(1)).astype(BF16))

    def features(f_ref, pr, side):
        one = ones_ref[side:side + 1, :]
        return f_ref[0, :, pr * LANES:(pr + 1) * LANES].astype(F32) * (1.0 - one) + one

    @pl.when(qi == 0)
    def _():
        lanes = lax.broadcasted_iota(jnp.int32, (L, LANES), 1)
        for pr in range(n_heads // 2):
            fk = features(fk_ref, pr, 1)
            kaug_ref[2 * pr], kaug_ref[2 * pr + 1] = merge(lanes, pr, k_ref, lambda e: fk)
            ones = ((lanes == HEAD_DIM).astype(F32), (lanes == 0).astype(F32))
            vaug_ref[2 * pr], vaug_ref[2 * pr + 1] = merge(lanes, pr, v_ref, lambda e: ones[e])

    lanes_q = lax.broadcasted_iota(jnp.int32, (tq, LANES), 1)
    for pr in range(n_heads // 2):
        fq = features(fq_ref, pr, 0)
        qaug_ref[2 * pr], qaug_ref[2 * pr + 1] = merge(lanes_q, pr, q_ref, lambda e: fq)

    def tile(j, r0, r1, masked, first=False):
        start = pl.multiple_of(j * tk, tk)
        for e in range(n_heads):
            kt = kaug_ref[e, pl.ds(start, tk), :]
            s = lax.dot_general(qaug_ref[e, r0:r1, :], kt, (((1,), (1,)), ((), ())),
                                preferred_element_type=F32)
            if masked:
                row = lax.broadcasted_iota(jnp.int32, s.shape, 0)
                col = lax.broadcasted_iota(jnp.int32, s.shape, 1)
                s = jnp.where(col <= row, s, NEG_BIG)
            m_new = jnp.broadcast_to(jnp.max(s, axis=1, keepdims=True), (r1 - r0, LANES))
            if not first:
                m = m_ref[e, r0:r1, :]
                m_new = jnp.maximum(m, m_new)
            p = jnp.exp2((s - jnp.concatenate([m_new] * (tk // LANES), axis=1)).astype(BF16))
            pv = _dot(p, vaug_ref[e, pl.ds(start, tk), :])
            acc_ref[e, r0:r1, :] = pv if first else jnp.exp2(m - m_new) * acc_ref[e, r0:r1, :] + pv
            m_ref[e, r0:r1, :] = m_new

    n_diag = tq // tk

    tile(qi * n_diag, 0, tk, True, first=True)
    if tk < tq:
        tile(qi * n_diag, tk, tq, False, first=True)

    def body(j, c):
        for d in range(n_diag):
            tile(j * n_diag + d, 0, tq, False)
        return c

    lax.fori_loop(0, qi, body, 0)
    for d in range(1, n_diag):
        tile(qi * n_diag + d, d * tk, (d + 1) * tk, True)
        if (d + 1) * tk < tq:
            tile(qi * n_diag + d, (d + 1) * tk, tq, False)

    for pr in range(n_heads // 2):
        acc0, acc1 = acc_ref[2 * pr], acc_ref[2 * pr + 1]
        out0 = acc0 / acc0[:, HEAD_DIM:HEAD_DIM + 1]
        out1 = acc1 / acc1[:, 0:1]
        o_ref[0, :, pr * LANES:(pr + 1) * LANES] = jnp.where(lanes_q < HEAD_DIM, out0, out1).astype(BF16)


def _fox_attn(q, k, v, feat, ones, riders):
    B, L, W = q.shape
    bw = ATT_PAIRS * LANES
    n_heads = 2 * ATT_PAIRS
    tq = min(ATT_TQ, L)
    grid = (B, W // bw, L // tq)
    n_steps = grid[0] * grid[1] * grid[2]
    qspec = pl.BlockSpec((1, tq, bw), lambda b, p, i: (b, i, p))
    kspec = pl.BlockSpec((1, L, bw), lambda b, p, i: (b, 0, p))
    step = lambda b, p, i: ((b * grid[1] + p) * grid[2] + i, 0)
    rspecs = [pl.BlockSpec((r.shape[0] // n_steps, r.shape[1]), step) for r in riders]
    out = pl.pallas_call(
        _fox_attn_kernel,
        grid=grid,
        in_specs=[qspec, kspec, kspec, qspec, kspec, _const_spec(ones.shape)] + rspecs,
        out_specs=[qspec] + rspecs,
        out_shape=[jax.ShapeDtypeStruct((B, L, W), BF16)]
                  + [jax.ShapeDtypeStruct(r.shape, BF16) for r in riders],
        scratch_shapes=[pltpu.VMEM((n_heads, L, LANES), BF16), pltpu.VMEM((n_heads, L, LANES), BF16),
                        pltpu.VMEM((n_heads, tq, LANES), BF16), pltpu.VMEM((n_heads, tq, LANES), F32),
                        pltpu.VMEM((n_heads, tq, LANES), F32)],
        compiler_params=_params("parallel", "parallel", "arbitrary"),
        name="fox_attn",
    )(q, k, v, feat, feat, ones, *riders)
    return out[0], out[1:]


def _s5_kernel(u0_ref, u1_ref, bst_ref, cst_ref, are_ref, aim_ref, d_ref, o0_ref, o1_ref,
               x_ref, st_ref):
    rows = u0_ref.shape[0]
    nb = x_ref.shape[0] // 2
    hrows = rows // 2

    @pl.when(pl.program_id(0) == 0)
    def _():
        st_ref[...] = jnp.zeros_like(st_ref)

    def half_rows(ref, h, *lead):
        return ref[(*lead, pl.ds(h, hrows, stride=2), slice(None))]

    u_h = []
    for h in range(2):
        u = jnp.concatenate([half_rows(u0_ref, h), half_rows(u1_ref, h)], axis=1)
        u_h.append(u)
        bu = _dot(u.astype(BF16), bst_ref[h])
        for c in range(2 * nb):
            x_ref[c, pl.ds(h, hrows, stride=2), :] = bu[:, c * LANES:(c + 1) * LANES]

    a_re = [are_ref[:, c * LANES:(c + 1) * LANES] for c in range(nb)]
    a_im = [aim_ref[:, c * LANES:(c + 1) * LANES] for c in range(nb)]

    def step(i, state):
        r = pl.ds(pl.multiple_of(i * SUBLANES, SUBLANES), SUBLANES)
        new = []
        for c in range(nb):
            s_re, s_im = state[2 * c], state[2 * c + 1]
            n_re = a_re[c] * s_re - a_im[c] * s_im + x_ref[c, r, :]
            n_im = a_re[c] * s_im + a_im[c] * s_re + x_ref[nb + c, r, :]
            x_ref[c, r, :] = n_re
            x_ref[nb + c, r, :] = n_im
            new += [n_re, n_im]
        return tuple(new)

    state = lax.fori_loop(0, rows // SUBLANES, step,
                          tuple(st_ref[c] for c in range(2 * nb)), unroll=4)
    for c in range(2 * nb):
        st_ref[c] = state[c]

    for h in range(2):
        xs = jnp.concatenate([half_rows(x_ref, h, c) for c in range(nb)]
                             + [half_rows(x_ref, h, nb + c) for c in range(nb)], axis=1)
        y = _dot(xs.astype(BF16), cst_ref[h])
        g = jax.nn.gelu(y + d_ref[h:h + 1, :] * u_h[h])
        o0_ref[pl.ds(h, hrows, stride=2), :] = g[:, 0:LANES]
        o1_ref[pl.ds(h, hrows, stride=2), :] = g[:, LANES:2 * LANES]


def _s5_discretize(lam_re, lam_im, log_dt, b_re, b_im):
    dt = jnp.exp(log_dt)[:, None]
    mag = jnp.exp(lam_re * dt)
    ab_re = mag * jnp.cos(lam_im * dt)
    ab_im = mag * jnp.sin(lam_im * dt)
    den = lam_re * lam_re + lam_im * lam_im
    nr = ab_re - 1.0
    ni = ab_im
    q_re = (nr * lam_re + ni * lam_im) / den
    q_im = (ni * lam_re - nr * lam_im) / den
    bb_re = q_re[..., None] * b_re - q_im[..., None] * b_im
    bb_im = q_re[..., None] * b_im + q_im[..., None] * b_re
    return ab_re, ab_im, bb_re, bb_im


def _s5_matrices(ab_re, ab_im, bb_re, bb_im, c_re, c_im, d, batch):
    G, P, H = bb_re.shape
    gh = G // 2
    hw, ns = gh * H, gh * P
    bmask = np.kron(np.eye(gh, dtype=np.float32), np.ones((H, P), np.float32))

    def bmat(bb):
        t = jnp.transpose(bb.reshape(2, gh, P, H), (0, 1, 3, 2)).reshape(2, hw, P)
        return jnp.tile(t, (1, 1, gh)) * bmask

    def cmat(cc):
        t = jnp.transpose(cc.reshape(2, gh, H, P), (0, 1, 3, 2)).reshape(2, ns, H)
        return jnp.tile(t, (1, 1, gh)) * bmask.T

    bst = jnp.concatenate([bmat(bb_re), bmat(bb_im)], axis=2).astype(BF16)
    cst = jnp.concatenate([cmat(c_re), -cmat(c_im)], axis=1).astype(BF16)
    a_re = jnp.tile(ab_re.reshape(2, ns), (batch, 1))
    a_im = jnp.tile(ab_im.reshape(2, ns), (batch, 1))
    return bst, cst, a_re, a_im, d.reshape(2, hw)


def _s5_scan(u0, u1, bst, cst, a_re, a_im, d2):
    n_rows = u0.shape[0]
    ns = a_re.shape[1]
    rows = min(S5_TILE * SUBLANES, n_rows)
    blk = pl.BlockSpec((rows, LANES), lambda t: (t, 0))
    return pl.pallas_call(
        _s5_kernel,
        grid=(n_rows // rows,),
        in_specs=[blk, blk, _const_spec(bst.shape), _const_spec(cst.shape),
                  _const_spec(a_re.shape), _const_spec(a_im.shape), _const_spec(d2.shape)],
        out_specs=[blk, blk],
        out_shape=[jax.ShapeDtypeStruct((n_rows, LANES), F32)] * 2,
        scratch_shapes=[pltpu.VMEM((2 * ns // LANES, rows, LANES), F32),
                        pltpu.VMEM((2 * ns // LANES, SUBLANES, LANES), F32)],
        compiler_params=_params("arbitrary"),
        name="s5_scan",
    )(u0, u1, bst, cst, a_re, a_im, d2)


def _mlp_rows(x, gpre_ref, gpost_ref, w1_ref, w2_ref):
    h = _rms(x, gpre_ref[...]).astype(BF16)
    d_ff = w1_ref.shape[1]
    acc = jnp.zeros(x.shape, F32)
    for f in range(0, d_ff, MLP_FF_CHUNK):
        a = jnp.maximum(_dot(h, w1_ref[:, f:f + MLP_FF_CHUNK]), 0.0)
        acc = acc + _dot((a * a).astype(BF16), w2_ref[f:f + MLP_FF_CHUNK, :])
    return x + _rms(acc, gpost_ref[...])


def _even_post_kernel(x_ref, g0_ref, g1_ref, yb_ref, wglu_ref, wout_ref, g_ref,
                      gpre_ref, gpost_ref, w1_ref, w2_ref, o_ref):
    nb, tm, _ = x_ref.shape
    half_w = wout_ref.shape[0] // 2

    def mixer_tail(b):
        blocks = [ref[pl.ds(2 * b + half, tm, stride=2 * nb), :]
                  for half in range(2) for ref in (g0_ref, g1_ref)]
        ga = jnp.concatenate(blocks, axis=1)
        ya = ga * jax.nn.sigmoid(_dot(ga.astype(BF16), wglu_ref[...]))
        mix = (_dot(ya.astype(BF16), wout_ref[0:half_w, :])
               + _dot(yb_ref[b], wout_ref[half_w:2 * half_w, :]))
        return x_ref[b] + _rms(mix, g_ref[...])

    x1 = jnp.concatenate([mixer_tail(b) for b in range(nb)], axis=0)
    out = _mlp_rows(x1, gpre_ref, gpost_ref, w1_ref, w2_ref)
    for b in range(nb):
        o_ref[b] = out[b * tm:(b + 1) * tm]


def _even_post(x, g0, g1, yb, w_glu, w_out, g, g_pre, g_post, w1s, w2s, layer):
    B, L, D = x.shape
    tm = min(POST_TILE, L)
    tok = lambda r: (0, r, 0)
    gblk = pl.BlockSpec((tm * 2 * B, LANES), lambda r: (r, 0))
    consts = (w_glu, w_out, g, g_pre, g_post)
    return pl.pallas_call(
        _even_post_kernel,
        grid=(L // tm,),
        in_specs=[pl.BlockSpec((B, tm, D), tok), gblk, gblk, pl.BlockSpec((B, tm, 512), tok)]
                 + [_const_spec(c.shape) for c in consts]
                 + [_layer_spec(w1s.shape, layer), _layer_spec(w2s.shape, layer)],
        out_specs=pl.BlockSpec((B, tm, D), tok),
        out_shape=jax.ShapeDtypeStruct((B, L, D), F32),
        compiler_params=_params("parallel"),
        name="even_post_mlp",
    )(x, g0, g1, yb, *consts, w1s, w2s)


def _odd_inproj_rows(x, gpre_ref, win_ref):
    width = win_ref.shape[1] // 3
    h = _rms(x, gpre_ref[...]).astype(BF16)
    return tuple(_dot(h, win_ref[:, n * width:(n + 1) * width]) for n in range(3))


def _odd_mix_rows(x, xc, u, v, t0, pool_ref, pw_ref, ps_ref, lng_ref, lnb_ref, ws, bs_ref,
                  wout_ref, gpost_ref):
    tm = x.shape[0]
    pad = max(POOL_WINDOWS)
    gd = LANES
    n_grp = len(POOL_WINDOWS)
    width = n_grp * gd

    pool_ref[pad:pad + tm, :] = xc
    t = t0 + lax.broadcasted_iota(jnp.int32, (tm, 1), 0)
    yc = []
    for g, w in enumerate(POOL_WINDOWS):
        cols = slice(g * gd, (g + 1) * gd)
        tot = xc[:, cols]
        for lag in range(1, w):
            tot = tot + pool_ref[pl.ds(pad - lag, tm), cols]
        cnt = jnp.minimum(t + 1, w).astype(F32)
        pooled = tot / cnt - xc[:, cols]
        yc.append(_dot(pooled.astype(BF16), pw_ref[g]))
    pool_ref[0:pad, :] = pool_ref[tm:tm + pad, :]
    y_c = jnp.concatenate(yc, axis=-1) * ps_ref[...]

    gv = jax.nn.gelu(v)
    mu = jnp.mean(gv, axis=-1, keepdims=True)
    vc = gv - mu
    vn = vc * lax.rsqrt(jnp.mean(vc * vc, axis=-1, keepdims=True) + EPS)
    vn = (vn * lng_ref[...] + lnb_ref[...]).astype(BF16)
    chunks = []
    for n in range(tm // CHUNK):
        parts = [_dot(ws[g], vn[n * CHUNK:(n + 1) * CHUNK, g * gd:(g + 1) * gd]) for g in range(n_grp)]
        chunks.append(jnp.concatenate(parts, axis=-1) + bs_ref[...])
    y_d = jax.nn.gelu(u) * jnp.concatenate(chunks, axis=0)

    mix = _dot(y_c.astype(BF16), wout_ref[0:width, :]) + _dot(y_d.astype(BF16), wout_ref[width:2 * width, :])
    return x + _rms(mix, gpost_ref[...])


def _odd_layer_kernel(x_ref, gpre_ref, win_ref, pw_ref, ps_ref, lng_ref, lnb_ref, ws_ref,
                      bs_ref, wout_ref, gpost_ref, mgpre_ref, mgpost_ref, w1_ref, w2_ref,
                      o_ref, pool_ref):
    r = pl.program_id(0)
    nb, tm, _ = x_ref.shape
    pad = max(POOL_WINDOWS)

    @pl.when(r == 0)
    def _():
        pool_ref[:, 0:pad, :] = jnp.zeros((nb, pad, pool_ref.shape[2]), F32)

    rows = lax.broadcasted_iota(jnp.int32, (CHUNK, CHUNK), 0)
    cols = lax.broadcasted_iota(jnp.int32, (CHUNK, CHUNK), 1)
    ws = [jnp.where(cols <= rows, ws_ref[g], 0.0).astype(BF16) for g in range(ws_ref.shape[0])]
    proj = {0: _odd_inproj_rows(x_ref[0], gpre_ref, win_ref)}
    x1 = []
    for b in range(nb):
        if b + 1 < nb:
            proj[b + 1] = _odd_inproj_rows(x_ref[b + 1], gpre_ref, win_ref)
        if len(x1) == MLP_GROUP:
            out = _mlp_rows(jnp.concatenate(x1, axis=0), mgpre_ref, mgpost_ref, w1_ref, w2_ref)
            for n in range(MLP_GROUP):
                o_ref[b - MLP_GROUP + n] = out[n * tm:(n + 1) * tm]
            x1 = []
        x1.append(_odd_mix_rows(x_ref[b], *proj.pop(b), r * tm, pool_ref.at[b], pw_ref, ps_ref,
                                lng_ref, lnb_ref, ws, bs_ref, wout_ref, gpost_ref))
    out = _mlp_rows(jnp.concatenate(x1, axis=0), mgpre_ref, mgpost_ref, w1_ref, w2_ref)
    for n in range(len(x1)):
        o_ref[nb - len(x1) + n] = out[n * tm:(n + 1) * tm]


def _odd_layer_call(x, consts, w1s, w2s, layer):
    B, L, D = x.shape
    tm = min(POST_TILE, L)
    width = consts[3].shape[1]
    tok = lambda r: (0, r, 0)
    return pl.pallas_call(
        _odd_layer_kernel,
        grid=(L // tm,),
        in_specs=[pl.BlockSpec((B, tm, D), tok)] + [_const_spec(c.shape) for c in consts]
                 + [_layer_spec(w1s.shape, layer), _layer_spec(w2s.shape, layer)],
        out_specs=pl.BlockSpec((B, tm, D), tok),
        out_shape=jax.ShapeDtypeStruct((B, L, D), F32),
        scratch_shapes=[pltpu.VMEM((B, tm + max(POOL_WINDOWS), width), F32)],
        compiler_params=_params("arbitrary"),
        name="odd_layer",
    )(x, *consts, w1s, w2s)


def _even_layer(x, g_pre, g_post, w_in, lam_re, lam_im, log_dt, b_re, b_im, c_re, c_im, d,
                b_f, mlp_g_pre, mlp_g_post, late, i, layer):
    B, L, D = x.shape
    n_heads = b_f.shape[0]
    s5w = d.shape[0]
    fw = n_heads * HEAD_DIM
    assert s5w == 4 * LANES and w_in.shape[1] == s5w + 3 * fw + n_heads
    wf = jnp.pad(w_in[:, s5w + 3 * fw:], ((0, 0), (0, LANES - n_heads))).astype(BF16)
    assert B <= SUBLANES
    u0, u1, q, k, v, feat, ones = _even_inproj(x, g_pre[None], w_in.astype(BF16), wf, b_f)

    todo = [n for n, w in late.items() if w.dtype != BF16]
    y_b, cast = _fox_attn(q, k, v, feat, ones,
                          tuple(late[n].reshape(-1, late[n].shape[-1]) for n in todo))
    late = {**late, **{n: c.reshape(late[n].shape) for n, c in zip(todo, cast)}}

    ab_re, ab_im, bb_re, bb_im = _s5_discretize(lam_re, lam_im, log_dt, b_re, b_im)
    bst, cst, a_re, a_im, d2 = _s5_matrices(ab_re, ab_im, bb_re, bb_im, c_re, c_im, d, B)
    g0, g1 = _s5_scan(u0, u1, bst, cst, a_re, a_im, d2)

    x = _even_post(x, g0, g1, y_b, late["w_glu"][i], late["w_out_even"][i], g_post[None],
                   mlp_g_pre[None], mlp_g_post[None], late["mlp_w1"], late["mlp_w2"], layer)
    return x, late


def _odd_layer(x, g_pre, g_post, pool_scale, ln_g, ln_b, w_s, b_s, mlp_g_pre, mlp_g_post,
               late, i, layer):
    bs_full = jnp.repeat(jnp.transpose(b_s), LANES, axis=1)
    consts = (g_pre[None], late["w_in_odd"][i], late["pool_w"][i], pool_scale[None], ln_g[None],
              ln_b[None], w_s, bs_full, late["w_out_odd"][i], g_post[None], mlp_g_pre[None],
              mlp_g_post[None])
    return _odd_layer_call(x, consts, late["mlp_w1"], late["mlp_w2"], layer)


def kernel(x, mix_pre_g, mix_post_g, mlp_pre_g, mlp_post_g, w_in_even, s5_lam_re, s5_lam_im, s5_log_dt, s5_b_re, s5_b_im, s5_c_re, s5_c_im, s5_d, s5_w_glu, fox_b_f, w_out_even, w_in_odd, pool_w, pool_scale, sgu_ln_g, sgu_ln_b, sgu_w_s, sgu_b_s, w_out_odd, mlp_w1, mlp_w2):
    depth = mix_pre_g.shape[0]
    late = dict(mlp_w1=mlp_w1, mlp_w2=mlp_w2, w_glu=s5_w_glu, w_out_even=w_out_even,
                w_in_odd=w_in_odd, pool_w=pool_w, w_out_odd=w_out_odd)
    for l in range(depth):
        i = l // 2
        if l % 2 == 0:
            x, late = _even_layer(
                x, mix_pre_g[l], mix_post_g[l], w_in_even[i], s5_lam_re[i], s5_lam_im[i],
                s5_log_dt[i], s5_b_re[i], s5_b_im[i], s5_c_re[i], s5_c_im[i], s5_d[i],
                fox_b_f[i], mlp_pre_g[l], mlp_post_g[l], late, i, l)
        else:
            x = _odd_layer(x, mix_pre_g[l], mix_post_g[l], pool_scale[i], sgu_ln_g[i], sgu_ln_b[i],
                           sgu_w_s[i], sgu_b_s[i], mlp_pre_g[l], mlp_post_g[l], late, i, l)
    return x
```

```python
import functools

import numpy as np
import jax
import jax.numpy as jnp
from jax import lax
from jax.experimental import pallas as pl
from jax.experimental.pallas import tpu as pltpu

F32 = jnp.float32
BF16 = jnp.bfloat16

EPS = 1e-6
LOG2E = 1.4426950408889634
NEG_BIG = -1e30

LANES = 128
SUBLANES = 8
VMEM_LIMIT = 56 * 1024 * 1024

S5_GROUP = 16
S5_STATE = 64
HEAD_DIM = 64
POOL_WINDOWS = (2, 4, 8, 16)
CHUNK = 128

ROW_TILE = 512
POST_TILE = 256
MLP_FF_CHUNK = 1024
MLP_GROUP = 2
S5_TILE = 256
S5_SUBTILES = 2
ATT_TQ = 1024
ATT_TK = 512
ATT_PAIRS = 2


def _params(*sem):
    return pltpu.CompilerParams(dimension_semantics=sem, vmem_limit_bytes=VMEM_LIMIT)


def _const_spec(shape):
    nd = len(shape)
    return pl.BlockSpec(shape, lambda *_: (0,) * nd, pipeline_mode=pl.Buffered(1))


def _layer_spec(shape, layer):
    nd = len(shape)
    return pl.BlockSpec((None,) + tuple(shape[1:]), lambda *_: (layer,) + (0,) * (nd - 1),
                        pipeline_mode=pl.Buffered(1))


def _rms(x, g):
    return x * lax.rsqrt(jnp.mean(x * x, axis=-1, keepdims=True) + EPS) * g


def _split3(x):
    hi = x.astype(BF16)
    r1 = x - hi.astype(F32)
    mid = r1.astype(BF16)
    lo = (r1 - mid.astype(F32)).astype(BF16)
    return hi, mid, lo


def _dot(a, b):
    return jnp.dot(a, b, preferred_element_type=F32)


def _even_inproj_kernel(x_ref, g_ref, w_ref, wf_ref, bf_ref, tri_ref, place_ref,
                        u0_ref, u1_ref, q_ref, k_ref, v_ref, feat_ref, carry_ref):
    nb, tm, _ = x_ref.shape
    s5w = 4 * LANES
    fw = q_ref.shape[2]
    q_scale = HEAD_DIM ** -0.5 * LOG2E

    @pl.when(pl.program_id(0) == 0)
    def _():
        carry_ref[...] = jnp.zeros_like(carry_ref)

    def running_sum(b, z):
        logf = jnp.minimum(z, 0.0) - jnp.log1p(jnp.exp(-jnp.abs(z)))
        c3 = _dot(tri_ref[...], jnp.concatenate(_split3(logf), axis=1))
        csum = (c3[:, 0:LANES] + c3[:, LANES:2 * LANES] + c3[:, 2 * LANES:3 * LANES]
                + carry_ref[b:b + 1, :])
        carry_ref[b:b + 1, :] = csum[tm - 1:tm, :]
        return csum

    def place(b, csum):
        pieces = jnp.concatenate(_split3(csum * LOG2E), axis=1)
        feat_ref[b] = _dot(pieces, place_ref[...]).astype(BF16)

    z_prev = None
    for b in range(nb):
        h = _rms(x_ref[b], g_ref[...]).astype(BF16)
        z = _dot(h, wf_ref[...]) + bf_ref[...]
        zu = _dot(h, w_ref[:, 0:s5w])
        for half in range(2):
            rows = pl.ds(2 * b + half, tm, stride=2 * nb)
            u0_ref[rows, :] = zu[:, 2 * half * LANES:(2 * half + 1) * LANES]
            u1_ref[rows, :] = zu[:, (2 * half + 1) * LANES:(2 * half + 2) * LANES]
        if z_prev is not None:
            csum = running_sum(b - 1, z_prev)
        q_ref[b] = (_dot(h, w_ref[:, s5w:s5w + fw]) * q_scale).astype(BF16)
        k_ref[b] = _dot(h, w_ref[:, s5w + fw:s5w + 2 * fw]).astype(BF16)
        if z_prev is not None:
            place(b - 1, csum)
        v_ref[b] = _dot(h, w_ref[:, s5w + 2 * fw:s5w + 3 * fw]).astype(BF16)
        z_prev = z
    place(nb - 1, running_sum(nb - 1, z_prev))


def _even_inproj(x, g, w, wf, b_f):
    B, L, D = x.shape
    n_heads = b_f.shape[0]
    tm = min(ROW_TILE, L)
    tok = lambda r: (0, r, 0)
    ublk = pl.BlockSpec((tm * 2 * B, LANES), lambda r: (r, 0))
    aspec = pl.BlockSpec((B, tm, 512), tok)
    place, ones = _fox_placement(n_heads)
    tri = jnp.asarray(np.tril(np.ones((tm, tm), np.float32)), BF16)
    bf = jnp.zeros((1, LANES), F32).at[0, :n_heads].set(b_f)
    consts = (g, w, wf, bf, tri, jnp.asarray(place, BF16))
    out = pl.pallas_call(
        _even_inproj_kernel,
        grid=(L // tm,),
        in_specs=[pl.BlockSpec((B, tm, D), tok)] + [_const_spec(c.shape) for c in consts],
        out_specs=[ublk, ublk, aspec, aspec, aspec, pl.BlockSpec((B, tm, place.shape[1]), tok)],
        out_shape=[
            jax.ShapeDtypeStruct((L * 2 * B, LANES), F32),
            jax.ShapeDtypeStruct((L * 2 * B, LANES), F32),
            jax.ShapeDtypeStruct((B, L, 512), BF16),
            jax.ShapeDtypeStruct((B, L, 512), BF16),
            jax.ShapeDtypeStruct((B, L, 512), BF16),
            jax.ShapeDtypeStruct((B, L, place.shape[1]), BF16),
        ],
        scratch_shapes=[pltpu.VMEM((SUBLANES, LANES), F32)],
        compiler_params=_params("arbitrary"),
        name="even_inproj",
    )(x, *consts)
    return (*out, jnp.asarray(ones))


def _fox_placement(n_heads):
    width = (n_heads // 2) * LANES
    place = np.zeros((3 * LANES, width), np.float32)
    ones = np.zeros((2, LANES), np.float32)
    for h in range(n_heads):
        base = HEAD_DIM if h % 2 == 0 else 0
        for j in range(3):
            place[j * LANES + h, (h // 2) * LANES + base + j] = 1.0
            place[j * LANES + h, (h // 2) * LANES + base + 3 + j] = -1.0
            ones[0, base + 3 + j] = 1.0
            ones[1, base + j] = 1.0
    return place, ones


def _fox_attn_kernel(q_ref, k_ref, v_ref, fq_ref, fk_ref, ones_ref, *refs):
    n_riders = (len(refs) - 6) // 2
    o_ref = refs[n_riders]
    kaug_ref, vaug_ref, qaug_ref, m_ref, acc_ref = refs[2 * n_riders + 1:]
    for src, dst in zip(refs[:n_riders], refs[n_riders + 1:2 * n_riders + 1]):
        dst[...] = src[...].astype(BF16)

    qi = pl.program_id(2)
    tq = q_ref.shape[1]
    L = k_ref.shape[1]
    tk = min(ATT_TK, L)
    n_heads = kaug_ref.shape[0]
    assert tq % tk == 0

    def merge(lanes, first, x_ref, f):
        x = x_ref[0, :, first * LANES:(first + 1) * LANES].astype(F32)
        return (jnp.where(lanes < HEAD_DIM, x, f(0)).astype(BF16),
                jnp.where(lanes >= HEAD_DIM, x, f(1)).astype(BF16))

    def features(f_ref, pr, side):
        one = ones_ref[side:side + 1, :]
        return f_ref[0, :, pr * LANES:(pr + 1) * LANES].astype(F32) * (1.0 - one) + one

    @pl.when(qi == 0)
    def _():
        lanes = lax.broadcasted_iota(jnp.int32, (L, LANES), 1)
        for pr in range(n_heads // 2):
            fk = features(fk_ref, pr, 1)
            kaug_ref[2 * pr], kaug_ref[2 * pr + 1] = merge(lanes, pr, k_ref, lambda e: fk)
            ones = ((lanes == HEAD_DIM).astype(F32), (lanes == 0).astype(F32))
            vaug_ref[2 * pr], vaug_ref[2 * pr + 1] = merge(lanes, pr, v_ref, lambda e: ones[e])

    lanes_q = lax.broadcasted_iota(jnp.int32, (tq, LANES), 1)
    for pr in range(n_heads // 2):
        fq = features(fq_ref, pr, 0)
        qaug_ref[2 * pr], qaug_ref[2 * pr + 1] = merge(lanes_q, pr, q_ref, lambda e: fq)

    def tile(j, r0, r1, masked, first=False):
        start = pl.multiple_of(j * tk, tk)
        for e in range(n_heads):
            kt = kaug_ref[e, pl.ds(start, tk), :]
            s = lax.dot_general(qaug_ref[e, r0:r1, :], kt, (((1,), (1,)), ((), ())),
                                preferred_element_type=F32)
            if masked:
                row = lax.broadcasted_iota(jnp.int32, s.shape, 0)
                col = lax.broadcasted_iota(jnp.int32, s.shape, 1)
                s = jnp.where(col <= row, s, NEG_BIG)
            m_new = jnp.broadcast_to(jnp.max(s, axis=1, keepdims=True), (r1 - r0, LANES))
            if not first:
                m = m_ref[e, r0:r1, :]
                m_new = jnp.maximum(m, m_new)
            p = jnp.exp2((s - jnp.concatenate([m_new] * (tk // LANES), axis=1)).astype(BF16))
            pv = _dot(p, vaug_ref[e, pl.ds(start, tk), :])
            acc_ref[e, r0:r1, :] = pv if first else jnp.exp2(m - m_new) * acc_ref[e, r0:r1, :] + pv
            m_ref[e, r0:r1, :] = m_new

    n_diag = tq // tk

    tile(qi * n_diag, 0, tk, True, first=True)
    if tk < tq:
        tile(qi * n_diag, tk, tq, False, first=True)

    def body(j, c):
        for d in range(n_diag):
            tile(j * n_diag + d, 0, tq, False)
        return c

    lax.fori_loop(0, qi, body, 0)
    for d in range(1, n_diag):
        tile(qi * n_diag + d, d * tk, (d + 1) * tk, True)
        if (d + 1) * tk < tq:
            tile(qi * n_diag + d, (d + 1) * tk, tq, False)

    for pr in range(n_heads // 2):
        acc0, acc1 = acc_ref[2 * pr], acc_ref[2 * pr + 1]
        out0 = acc0 / acc0[:, HEAD_DIM:HEAD_DIM + 1]
        out1 = acc1 / acc1[:, 0:1]
        o_ref[0, :, pr * LANES:(pr + 1) * LANES] = jnp.where(lanes_q < HEAD_DIM, out0, out1).astype(BF16)


def _fox_attn(q, k, v, feat, ones, riders):
    B, L, W = q.shape
    bw = ATT_PAIRS * LANES
    n_heads = 2 * ATT_PAIRS
    tq = min(ATT_TQ, L)
    grid = (B, W // bw, L // tq)
    n_steps = grid[0] * grid[1] * grid[2]
    qspec = pl.BlockSpec((1, tq, bw), lambda b, p, i: (b, i, p))
    kspec = pl.BlockSpec((1, L, bw), lambda b, p, i: (b, 0, p))
    step = lambda b, p, i: ((b * grid[1] + p) * grid[2] + i, 0)
    rspecs = [pl.BlockSpec((r.shape[0] // n_steps, r.shape[1]), step) for r in riders]
    out = pl.pallas_call(
        _fox_attn_kernel,
        grid=grid,
        in_specs=[qspec, kspec, kspec, qspec, kspec, _const_spec(ones.shape)] + rspecs,
        out_specs=[qspec] + rspecs,
        out_shape=[jax.ShapeDtypeStruct((B, L, W), BF16)]
                  + [jax.ShapeDtypeStruct(r.shape, BF16) for r in riders],
        scratch_shapes=[pltpu.VMEM((n_heads, L, LANES), BF16), pltpu.VMEM((n_heads, L, LANES), BF16),
                        pltpu.VMEM((n_heads, tq, LANES), BF16), pltpu.VMEM((n_heads, tq, LANES), F32),
                        pltpu.VMEM((n_heads, tq, LANES), F32)],
        compiler_params=_params("parallel", "parallel", "arbitrary"),
        name="fox_attn",
    )(q, k, v, feat, feat, ones, *riders)
    return out[0], out[1:]


def _s5_kernel(u0_ref, u1_ref, bst_ref, cst_ref, are_ref, aim_ref, d_ref, o0_ref, o1_ref,
               st_ref, *x_refs):
    rows = u0_ref.shape[0]
    nb = st_ref.shape[0] // 2
    sub = rows // len(x_refs)
    hrows = sub // 2

    @pl.when(pl.program_id(0) == 0)
    def _():
        st_ref[...] = jnp.zeros_like(st_ref)

    def half_rows(ref, base, h, *lead):
        return ref[(*lead, pl.ds(base + h, hrows, stride=2), slice(None))]

    def project(k):
        u_h = []
        for h in range(2):
            u = jnp.concatenate([half_rows(u0_ref, k * sub, h), half_rows(u1_ref, k * sub, h)],
                                axis=1)
            u_h.append(u)
            bu = _dot(u.astype(BF16), bst_ref[h])
            for c in range(2 * nb):
                x_refs[k][c, pl.ds(h, hrows, stride=2), :] = bu[:, c * LANES:(c + 1) * LANES]
        return u_h

    a_re = [are_ref[:, c * LANES:(c + 1) * LANES] for c in range(nb)]
    a_im = [aim_ref[:, c * LANES:(c + 1) * LANES] for c in range(nb)]

    def scan(k, state):
        state = list(state)
        x_ref = x_refs[k]
        for i in range(sub // SUBLANES):
            r = pl.ds(i * SUBLANES, SUBLANES)
            for c in range(nb):
                s_re, s_im = state[2 * c], state[2 * c + 1]
                n_re = a_re[c] * s_re - a_im[c] * s_im + x_ref[c, r, :]
                n_im = a_re[c] * s_im + a_im[c] * s_re + x_ref[nb + c, r, :]
                x_ref[c, r, :] = n_re
                x_ref[nb + c, r, :] = n_im
                state[2 * c], state[2 * c + 1] = n_re, n_im
        return state

    def emit(k, u_h):
        for h in range(2):
            xs = jnp.concatenate([half_rows(x_refs[k], 0, h, c) for c in range(2 * nb)], axis=1)
            y = _dot(xs.astype(BF16), cst_ref[h])
            g = jax.nn.gelu(y + d_ref[h:h + 1, :] * u_h[h])
            o0_ref[pl.ds(k * sub + h, hrows, stride=2), :] = g[:, 0:LANES]
            o1_ref[pl.ds(k * sub + h, hrows, stride=2), :] = g[:, LANES:2 * LANES]

    state = [st_ref[c] for c in range(2 * nb)]
    u_next = project(0)
    for k in range(len(x_refs)):
        u_k = u_next
        if k + 1 < len(x_refs):
            u_next = project(k + 1)
        state = scan(k, state)
        emit(k, u_k)
    for c in range(2 * nb):
        st_ref[c] = state[c]


def _s5_discretize(lam_re, lam_im, log_dt, b_re, b_im):
    dt = jnp.exp(log_dt)[:, None]
    mag = jnp.exp(lam_re * dt)
    ab_re = mag * jnp.cos(lam_im * dt)
    ab_im = mag * jnp.sin(lam_im * dt)
    den = lam_re * lam_re + lam_im * lam_im
    nr = ab_re - 1.0
    ni = ab_im
    q_re = (nr * lam_re + ni * lam_im) / den
    q_im = (ni * lam_re - nr * lam_im) / den
    bb_re = q_re[..., None] * b_re - q_im[..., None] * b_im
    bb_im = q_re[..., None] * b_im + q_im[..., None] * b_re
    return ab_re, ab_im, bb_re, bb_im


def _s5_matrices(ab_re, ab_im, bb_re, bb_im, c_re, c_im, d, batch):
    G, P, H = bb_re.shape
    gh = G // 2
    hw, ns = gh * H, gh * P
    bmask = np.kron(np.eye(gh, dtype=np.float32), np.ones((H, P), np.float32))

    def bmat(bb):
        t = jnp.transpose(bb.reshape(2, gh, P, H), (0, 1, 3, 2)).reshape(2, hw, P)
        return jnp.tile(t, (1, 1, gh)) * bmask

    def cmat(cc):
        t = jnp.transpose(cc.reshape(2, gh, H, P), (0, 1, 3, 2)).reshape(2, ns, H)
        return jnp.tile(t, (1, 1, gh)) * bmask.T

    bst = jnp.concatenate([bmat(bb_re), bmat(bb_im)], axis=2).astype(BF16)
    cst = jnp.concatenate([cmat(c_re), -cmat(c_im)], axis=1).astype(BF16)
    a_re = jnp.tile(ab_re.reshape(2, ns), (batch, 1))
    a_im = jnp.tile(ab_im.reshape(2, ns), (batch, 1))
    return bst, cst, a_re, a_im, d.reshape(2, hw)


def _s5_scan(u0, u1, bst, cst, a_re, a_im, d2):
    n_rows = u0.shape[0]
    ns = a_re.shape[1]
    rows = min(S5_TILE * SUBLANES, n_rows)
    blk = pl.BlockSpec((rows, LANES), lambda t: (t, 0))
    return pl.pallas_call(
        _s5_kernel,
        grid=(n_rows // rows,),
        in_specs=[blk, blk, _const_spec(bst.shape), _const_spec(cst.shape),
                  _const_spec(a_re.shape), _const_spec(a_im.shape), _const_spec(d2.shape)],
        out_specs=[blk, blk],
        out_shape=[jax.ShapeDtypeStruct((n_rows, LANES), F32)] * 2,
        scratch_shapes=[pltpu.VMEM((2 * ns // LANES, SUBLANES, LANES), F32)]
                       + [pltpu.VMEM((2 * ns // LANES, rows // S5_SUBTILES, LANES), F32)] * S5_SUBTILES,
        compiler_params=_params("arbitrary"),
        name="s5_scan",
    )(u0, u1, bst, cst, a_re, a_im, d2)


def _mlp_rows(x, gpre_ref, gpost_ref, w1_ref, w2_ref):
    h = _rms(x, gpre_ref[...]).astype(BF16)
    d_ff = w1_ref.shape[1]
    acc = jnp.zeros(x.shape, F32)
    for f in range(0, d_ff, MLP_FF_CHUNK):
        a = jnp.maximum(_dot(h, w1_ref[:, f:f + MLP_FF_CHUNK]), 0.0)
        acc = acc + _dot((a * a).astype(BF16), w2_ref[f:f + MLP_FF_CHUNK, :])
    return x + _rms(acc, gpost_ref[...])


def _even_post_kernel(x_ref, g0_ref, g1_ref, yb_ref, wglu_ref, wout_ref, g_ref,
                      gpre_ref, gpost_ref, w1_ref, w2_ref, o_ref):
    nb, tm, _ = x_ref.shape
    half_w = wout_ref.shape[0] // 2

    def mixer_tail(b):
        blocks = [ref[pl.ds(2 * b + half, tm, stride=2 * nb), :]
                  for half in range(2) for ref in (g0_ref, g1_ref)]
        ga = jnp.concatenate(blocks, axis=1)
        ya = ga * jax.nn.sigmoid(_dot(ga.astype(BF16), wglu_ref[...]))
        mix = (_dot(ya.astype(BF16), wout_ref[0:half_w, :])
               + _dot(yb_ref[b], wout_ref[half_w:2 * half_w, :]))
        return x_ref[b] + _rms(mix, g_ref[...])

    x1 = jnp.concatenate([mixer_tail(b) for b in range(nb)], axis=0)
    out = _mlp_rows(x1, gpre_ref, gpost_ref, w1_ref, w2_ref)
    for b in range(nb):
        o_ref[b] = out[b * tm:(b + 1) * tm]


def _even_post(x, g0, g1, yb, w_glu, w_out, g, g_pre, g_post, w1s, w2s, layer):
    B, L, D = x.shape
    tm = min(POST_TILE, L)
    tok = lambda r: (0, r, 0)
    gblk = pl.BlockSpec((tm * 2 * B, LANES), lambda r: (r, 0))
    consts = (w_glu, w_out, g, g_pre, g_post)
    return pl.pallas_call(
        _even_post_kernel,
        grid=(L // tm,),
        in_specs=[pl.BlockSpec((B, tm, D), tok), gblk, gblk, pl.BlockSpec((B, tm, 512), tok)]
                 + [_const_spec(c.shape) for c in consts]
                 + [_layer_spec(w1s.shape, layer), _layer_spec(w2s.shape, layer)],
        out_specs=pl.BlockSpec((B, tm, D), tok),
        out_shape=jax.ShapeDtypeStruct((B, L, D), F32),
        compiler_params=_params("parallel"),
        name="even_post_mlp",
    )(x, g0, g1, yb, *consts, w1s, w2s)


def _odd_inproj_rows(x, gpre_ref, win_ref):
    width = win_ref.shape[1] // 3
    h = _rms(x, gpre_ref[...]).astype(BF16)
    return tuple(_dot(h, win_ref[:, n * width:(n + 1) * width]) for n in range(3))


def _odd_mix_rows(x, xc, u, v, t0, pool_ref, pw_ref, ps_ref, lng_ref, lnb_ref, ws, bs_ref,
                  wout_ref, gpost_ref):
    tm = x.shape[0]
    pad = max(POOL_WINDOWS)
    gd = LANES
    n_grp = len(POOL_WINDOWS)
    width = n_grp * gd

    pool_ref[pad:pad + tm, :] = xc
    t = t0 + lax.broadcasted_iota(jnp.int32, (tm, 1), 0)
    yc = []
    for g, w in enumerate(POOL_WINDOWS):
        cols = slice(g * gd, (g + 1) * gd)
        tot = xc[:, cols]
        for lag in range(1, w):
            tot = tot + pool_ref[pl.ds(pad - lag, tm), cols]
        cnt = jnp.minimum(t + 1, w).astype(F32)
        pooled = tot / cnt - xc[:, cols]
        yc.append(_dot(pooled.astype(BF16), pw_ref[g]))
    pool_ref[0:pad, :] = pool_ref[tm:tm + pad, :]
    y_c = jnp.concatenate(yc, axis=-1) * ps_ref[...]

    gv = jax.nn.gelu(v)
    mu = jnp.mean(gv, axis=-1, keepdims=True)
    vc = gv - mu
    vn = vc * lax.rsqrt(jnp.mean(vc * vc, axis=-1, keepdims=True) + EPS)
    vn = (vn * lng_ref[...] + lnb_ref[...]).astype(BF16)
    chunks = []
    for n in range(tm // CHUNK):
        parts = [_dot(ws[g], vn[n * CHUNK:(n + 1) * CHUNK, g * gd:(g + 1) * gd]) for g in range(n_grp)]
        chunks.append(jnp.concatenate(parts, axis=-1) + bs_ref[...])
    y_d = jax.nn.gelu(u) * jnp.concatenate(chunks, axis=0)

    mix = _dot(y_c.astype(BF16), wout_ref[0:width, :]) + _dot(y_d.astype(BF16), wout_ref[width:2 * width, :])
    return x + _rms(mix, gpost_ref[...])


def _odd_layer_kernel(x_ref, gpre_ref, win_ref, pw_ref, ps_ref, lng_ref, lnb_ref, ws_ref,
                      bs_ref, wout_ref, gpost_ref, mgpre_ref, mgpost_ref, w1_ref, w2_ref,
                      o_ref, pool_ref):
    r = pl.program_id(0)
    nb, tm, _ = x_ref.shape
    pad = max(POOL_WINDOWS)

    @pl.when(r == 0)
    def _():
        pool_ref[:, 0:pad, :] = jnp.zeros((nb, pad, pool_ref.shape[2]), F32)

    rows = lax.broadcasted_iota(jnp.int32, (CHUNK, CHUNK), 0)
    cols = lax.broadcasted_iota(jnp.int32, (CHUNK, CHUNK), 1)
    ws = [jnp.where(cols <= rows, ws_ref[g], 0.0).astype(BF16) for g in range(ws_ref.shape[0])]
    proj = {0: _odd_inproj_rows(x_ref[0], gpre_ref, win_ref)}
    x1 = []
    for b in range(nb):
        if b + 1 < nb:
            proj[b + 1] = _odd_inproj_rows(x_ref[b + 1], gpre_ref, win_ref)
        if len(x1) == MLP_GROUP:
            out = _mlp_rows(jnp.concatenate(x1, axis=0), mgpre_ref, mgpost_ref, w1_ref, w2_ref)
            for n in range(MLP_GROUP):
                o_ref[b - MLP_GROUP + n] = out[n * tm:(n + 1) * tm]
            x1 = []
        x1.append(_odd_mix_rows(x_ref[b], *proj.pop(b), r * tm, pool_ref.at[b], pw_ref, ps_ref,
                                lng_ref, lnb_ref, ws, bs_ref, wout_ref, gpost_ref))
    out = _mlp_rows(jnp.concatenate(x1, axis=0), mgpre_ref, mgpost_ref, w1_ref, w2_ref)
    for n in range(len(x1)):
        o_ref[nb - len(x1) + n] = out[n * tm:(n + 1) * tm]


def _odd_layer_call(x, consts, w1s, w2s, layer):
    B, L, D = x.shape
    tm = min(POST_TILE, L)
    width = consts[3].shape[1]
    tok = lambda r: (0, r, 0)
    return pl.pallas_call(
        _odd_layer_kernel,
        grid=(L // tm,),
        in_specs=[pl.BlockSpec((B, tm, D), tok)] + [_const_spec(c.shape) for c in consts]
                 + [_layer_spec(w1s.shape, layer), _layer_spec(w2s.shape, layer)],
        out_specs=pl.BlockSpec((B, tm, D), tok),
        out_shape=jax.ShapeDtypeStruct((B, L, D), F32),
        scratch_shapes=[pltpu.VMEM((B, tm + max(POOL_WINDOWS), width), F32)],
        compiler_params=_params("arbitrary"),
        name="odd_layer",
    )(x, *consts, w1s, w2s)


def _even_layer(x, g_pre, g_post, w_in, lam_re, lam_im, log_dt, b_re, b_im, c_re, c_im, d,
                b_f, mlp_g_pre, mlp_g_post, late, i, layer):
    B, L, D = x.shape
    n_heads = b_f.shape[0]
    s5w = d.shape[0]
    fw = n_heads * HEAD_DIM
    assert s5w == 4 * LANES and w_in.shape[1] == s5w + 3 * fw + n_heads
    wf = jnp.pad(w_in[:, s5w + 3 * fw:], ((0, 0), (0, LANES - n_heads))).astype(BF16)
    assert B <= SUBLANES
    u0, u1, q, k, v, feat, ones = _even_inproj(x, g_pre[None], w_in.astype(BF16), wf, b_f)

    todo = [n for n, w in late.items() if w.dtype != BF16]
    y_b, cast = _fox_attn(q, k, v, feat, ones,
                          tuple(late[n].reshape(-1, late[n].shape[-1]) for n in todo))
    late = {**late, **{n: c.reshape(late[n].shape) for n, c in zip(todo, cast)}}

    ab_re, ab_im, bb_re, bb_im = _s5_discretize(lam_re, lam_im, log_dt, b_re, b_im)
    bst, cst, a_re, a_im, d2 = _s5_matrices(ab_re, ab_im, bb_re, bb_im, c_re, c_im, d, B)
    g0, g1 = _s5_scan(u0, u1, bst, cst, a_re, a_im, d2)

    x = _even_post(x, g0, g1, y_b, late["w_glu"][i], late["w_out_even"][i], g_post[None],
                   mlp_g_pre[None], mlp_g_post[None], late["mlp_w1"], late["mlp_w2"], layer)
    return x, late


def _odd_layer(x, g_pre, g_post, pool_scale, ln_g, ln_b, w_s, b_s, mlp_g_pre, mlp_g_post,
               late, i, layer):
    bs_full = jnp.repeat(jnp.transpose(b_s), LANES, axis=1)
    consts = (g_pre[None], late["w_in_odd"][i], late["pool_w"][i], pool_scale[None], ln_g[None],
              ln_b[None], w_s, bs_full, late["w_out_odd"][i], g_post[None], mlp_g_pre[None],
              mlp_g_post[None])
    return _odd_layer_call(x, consts, late["mlp_w1"], late["mlp_w2"], layer)


def kernel(x, mix_pre_g, mix_post_g, mlp_pre_g, mlp_post_g, w_in_even, s5_lam_re, s5_lam_im, s5_log_dt, s5_b_re, s5_b_im, s5_c_re, s5_c_im, s5_d, s5_w_glu, fox_b_f, w_out_even, w_in_odd, pool_w, pool_scale, sgu_ln_g, sgu_ln_b, sgu_w_s, sgu_b_s, w_out_odd, mlp_w1, mlp_w2):
    depth = mix_pre_g.shape[0]
    late = dict(mlp_w1=mlp_w1, mlp_w2=mlp_w2, w_glu=s5_w_glu, w_out_even=w_out_even,
                w_in_odd=w_in_odd, pool_w=pool_w, w_out_odd=w_out_odd)
    for l in range(depth):
        i = l // 2
        if l % 2 == 0:
            x, late = _even_layer(
                x, mix_pre_g[l], mix_post_g[l], w_in_even[i], s5_lam_re[i], s5_lam_im[i],
                s5_log_dt[i], s5_b_re[i], s5_b_im[i], s5_c_re[i], s5_c_im[i], s5_d[i],
                fox_b_f[i], mlp_pre_g[l], mlp_post_g[l], late, i, l)
        else:
            x = _odd_layer(x, mix_pre_g[l], mix_post_g[l], pool_scale[i], sgu_ln_g[i], sgu_ln_b[i],
                           sgu_w_s[i], sgu_b_s[i], mlp_pre_g[l], mlp_post_g[l], late, i, l)
    return x
```

```python
import numpy as np
import jax
import jax.numpy as jnp
from jax import lax
from jax.experimental import pallas as pl
from jax.experimental.pallas import tpu as pltpu

F32 = jnp.float32
BF16 = jnp.bfloat16

EPS = 1e-6
LOG2E = 1.4426950408889634
NEG_BIG = -1e30

LANES = 128
SUBLANES = 8
VMEM_LIMIT = 56 * 1024 * 1024

HEAD_DIM = 64
POOL_WINDOWS = (2, 4, 8, 16)
CHUNK = 128

ROW_TILE = 512
POST_TILE = 256
MLP_FF_CHUNK = 1024
MLP_GROUP = 2
S5_TILE = 256
S5_SUBTILES = 2
ATT_TQ = 1024
ATT_TK = 512
ATT_PAIRS = 2


def _params(*sem):
    return pltpu.CompilerParams(dimension_semantics=sem, vmem_limit_bytes=VMEM_LIMIT)


def _const_spec(shape):
    nd = len(shape)
    return pl.BlockSpec(shape, lambda *_: (0,) * nd, pipeline_mode=pl.Buffered(1))


def _layer_spec(shape, layer):
    nd = len(shape)
    return pl.BlockSpec((None,) + tuple(shape[1:]), lambda *_: (layer,) + (0,) * (nd - 1),
                        pipeline_mode=pl.Buffered(1))


def _rms(x, g):
    return x * lax.rsqrt(jnp.mean(x * x, axis=-1, keepdims=True) + EPS) * g


def _split3(x):
    hi = x.astype(BF16)
    r1 = x - hi.astype(F32)
    mid = r1.astype(BF16)
    lo = (r1 - mid.astype(F32)).astype(BF16)
    return hi, mid, lo


def _dot(a, b):
    return jnp.dot(a, b, preferred_element_type=F32)


def _even_inproj_kernel(x_ref, g_ref, w_ref, wf_ref, bf_ref, tri_ref, place_ref,
                        u0_ref, u1_ref, q_ref, k_ref, v_ref, feat_ref, carry_ref):
    nb, tm, _ = x_ref.shape
    s5w = 4 * LANES
    fw = q_ref.shape[2]
    q_scale = HEAD_DIM ** -0.5 * LOG2E

    @pl.when(pl.program_id(0) == 0)
    def _():
        carry_ref[...] = jnp.zeros_like(carry_ref)

    def running_sum(b, z):
        logf = jnp.minimum(z, 0.0) - jnp.log1p(jnp.exp(-jnp.abs(z)))
        c3 = _dot(tri_ref[...], jnp.concatenate(_split3(logf), axis=1))
        csum = (c3[:, 0:LANES] + c3[:, LANES:2 * LANES] + c3[:, 2 * LANES:3 * LANES]
                + carry_ref[b:b + 1, :])
        carry_ref[b:b + 1, :] = csum[tm - 1:tm, :]
        return csum

    def place(b, csum):
        pieces = jnp.concatenate(_split3(csum * LOG2E), axis=1)
        feat_ref[b] = _dot(pieces, place_ref[...]).astype(BF16)

    z_prev = None
    for b in range(nb):
        h = _rms(x_ref[b], g_ref[...]).astype(BF16)
        z = _dot(h, wf_ref[...]) + bf_ref[...]
        zu = _dot(h, w_ref[:, 0:s5w])
        for half in range(2):
            rows = pl.ds(2 * b + half, tm, stride=2 * nb)
            u0_ref[rows, :] = zu[:, 2 * half * LANES:(2 * half + 1) * LANES]
            u1_ref[rows, :] = zu[:, (2 * half + 1) * LANES:(2 * half + 2) * LANES]
        if z_prev is not None:
            csum = running_sum(b - 1, z_prev)
        q_ref[b] = (_dot(h, w_ref[:, s5w:s5w + fw]) * q_scale).astype(BF16)
        k_ref[b] = _dot(h, w_ref[:, s5w + fw:s5w + 2 * fw]).astype(BF16)
        if z_prev is not None:
            place(b - 1, csum)
        v_ref[b] = _dot(h, w_ref[:, s5w + 2 * fw:s5w + 3 * fw]).astype(BF16)
        z_prev = z
    place(nb - 1, running_sum(nb - 1, z_prev))


def _even_inproj(x, g, w, wf, b_f):
    B, L, D = x.shape
    n_heads = b_f.shape[0]
    tm = min(ROW_TILE, L)
    tok = lambda r: (0, r, 0)
    ublk = pl.BlockSpec((tm * 2 * B, LANES), lambda r: (r, 0))
    aspec = pl.BlockSpec((B, tm, 512), tok)
    place, ones = _fox_placement(n_heads)
    tri = jnp.asarray(np.tril(np.ones((tm, tm), np.float32)), BF16)
    bf = jnp.zeros((1, LANES), F32).at[0, :n_heads].set(b_f)
    consts = (g, w, wf, bf, tri, jnp.asarray(place, BF16))
    out = pl.pallas_call(
        _even_inproj_kernel,
        grid=(L // tm,),
        in_specs=[pl.BlockSpec((B, tm, D), tok)] + [_const_spec(c.shape) for c in consts],
        out_specs=[ublk, ublk, aspec, aspec, aspec, pl.BlockSpec((B, tm, place.shape[1]), tok)],
        out_shape=[
            jax.ShapeDtypeStruct((L * 2 * B, LANES), F32),
            jax.ShapeDtypeStruct((L * 2 * B, LANES), F32),
            jax.ShapeDtypeStruct((B, L, 512), BF16),
            jax.ShapeDtypeStruct((B, L, 512), BF16),
            jax.ShapeDtypeStruct((B, L, 512), BF16),
            jax.ShapeDtypeStruct((B, L, place.shape[1]), BF16),
        ],
        scratch_shapes=[pltpu.VMEM((SUBLANES, LANES), F32)],
        compiler_params=_params("arbitrary"),
        name="even_inproj",
    )(x, *consts)
    return (*out, jnp.asarray(ones))


def _fox_placement(n_heads):
    width = (n_heads // 2) * LANES
    place = np.zeros((3 * LANES, width), np.float32)
    ones = np.zeros((2, LANES), np.float32)
    for h in range(n_heads):
        base = HEAD_DIM if h % 2 == 0 else 0
        for j in range(3):
            place[j * LANES + h, (h // 2) * LANES + base + j] = 1.0
            place[j * LANES + h, (h // 2) * LANES + base + 3 + j] = -1.0
            ones[0, base + 3 + j] = 1.0
            ones[1, base + j] = 1.0
    return place, ones


def _fox_attn_kernel(q_ref, k_ref, v_ref, fq_ref, fk_ref, ones_ref, *refs):
    n_riders = (len(refs) - 6) // 2
    o_ref = refs[n_riders]
    kaug_ref, vaug_ref, qaug_ref, m_ref, acc_ref = refs[2 * n_riders + 1:]
    for src, dst in zip(refs[:n_riders], refs[n_riders + 1:2 * n_riders + 1]):
        dst[...] = src[...].astype(BF16)

    qi = pl.program_id(2)
    tq = q_ref.shape[1]
    L = k_ref.shape[1]
    tk = min(ATT_TK, L)
    n_heads = kaug_ref.shape[0]
    assert tq % tk == 0

    def merge(lanes, first, x_ref, f):
        x = x_ref[0, :, first * LANES:(first + 1) * LANES].astype(F32)
        return (jnp.where(lanes < HEAD_DIM, x, f(0)).astype(BF16),
                jnp.where(lanes >= HEAD_DIM, x, f(1)).astype(BF16))

    def features(f_ref, pr, side):
        one = ones_ref[side:side + 1, :]
        return f_ref[0, :, pr * LANES:(pr + 1) * LANES].astype(F32) * (1.0 - one) + one

    @pl.when(qi == 0)
    def _():
        lanes = lax.broadcasted_iota(jnp.int32, (L, LANES), 1)
        for pr in range(n_heads // 2):
            fk = features(fk_ref, pr, 1)
            kaug_ref[2 * pr], kaug_ref[2 * pr + 1] = merge(lanes, pr, k_ref, lambda e: fk)
            ones = ((lanes == HEAD_DIM).astype(F32), (lanes == 0).astype(F32))
            vaug_ref[2 * pr], vaug_ref[2 * pr + 1] = merge(lanes, pr, v_ref, lambda e: ones[e])

    lanes_q = lax.broadcasted_iota(jnp.int32, (tq, LANES), 1)
    for pr in range(n_heads // 2):
        fq = features(fq_ref, pr, 0)
        qaug_ref[2 * pr], qaug_ref[2 * pr + 1] = merge(lanes_q, pr, q_ref, lambda e: fq)

    def tile(j, r0, r1, masked, first=False):
        start = pl.multiple_of(j * tk, tk)
        for e in range(n_heads):
            kt = kaug_ref[e, pl.ds(start, tk), :]
            s = lax.dot_general(qaug_ref[e, r0:r1, :], kt, (((1,), (1,)), ((), ())),
                                preferred_element_type=F32)
            if masked:
                row = lax.broadcasted_iota(jnp.int32, s.shape, 0)
                col = lax.broadcasted_iota(jnp.int32, s.shape, 1)
                s = jnp.where(col <= row, s, NEG_BIG)
            m_new = jnp.broadcast_to(jnp.max(s, axis=1, keepdims=True), (r1 - r0, LANES))
            if not first:
                m = m_ref[e, r0:r1, :]
                m_new = jnp.maximum(m, m_new)
            p = jnp.exp2((s - jnp.concatenate([m_new] * (tk // LANES), axis=1)).astype(BF16))
            pv = _dot(p, vaug_ref[e, pl.ds(start, tk), :])
            acc_ref[e, r0:r1, :] = pv if first else jnp.exp2(m - m_new) * acc_ref[e, r0:r1, :] + pv
            m_ref[e, r0:r1, :] = m_new

    n_diag = tq // tk

    tile(qi * n_diag, 0, tk, True, first=True)
    if tk < tq:
        tile(qi * n_diag, tk, tq, False, first=True)

    def body(j, c):
        for d in range(n_diag):
            tile(j * n_diag + d, 0, tq, False)
        return c

    lax.fori_loop(0, qi, body, 0)
    for d in range(1, n_diag):
        tile(qi * n_diag + d, d * tk, (d + 1) * tk, True)
        if (d + 1) * tk < tq:
            tile(qi * n_diag + d, (d + 1) * tk, tq, False)

    for pr in range(n_heads // 2):
        acc0, acc1 = acc_ref[2 * pr], acc_ref[2 * pr + 1]
        out0 = acc0 / acc0[:, HEAD_DIM:HEAD_DIM + 1]
        out1 = acc1 / acc1[:, 0:1]
        o_ref[0, :, pr * LANES:(pr + 1) * LANES] = jnp.where(lanes_q < HEAD_DIM, out0, out1).astype(BF16)


def _fox_attn(q, k, v, feat, ones, riders):
    B, L, W = q.shape
    bw = ATT_PAIRS * LANES
    n_heads = 2 * ATT_PAIRS
    tq = min(ATT_TQ, L)
    grid = (B, W // bw, L // tq)
    n_steps = grid[0] * grid[1] * grid[2]
    qspec = pl.BlockSpec((1, tq, bw), lambda b, p, i: (b, i, p))
    kspec = pl.BlockSpec((1, L, bw), lambda b, p, i: (b, 0, p))
    step = lambda b, p, i: ((b * grid[1] + p) * grid[2] + i, 0)
    rspecs = [pl.BlockSpec((r.shape[0] // n_steps, r.shape[1]), step) for r in riders]
    out = pl.pallas_call(
        _fox_attn_kernel,
        grid=grid,
        in_specs=[qspec, kspec, kspec, qspec, kspec, _const_spec(ones.shape)] + rspecs,
        out_specs=[qspec] + rspecs,
        out_shape=[jax.ShapeDtypeStruct((B, L, W), BF16)]
                  + [jax.ShapeDtypeStruct(r.shape, BF16) for r in riders],
        scratch_shapes=[pltpu.VMEM((n_heads, L, LANES), BF16), pltpu.VMEM((n_heads, L, LANES), BF16),
                        pltpu.VMEM((n_heads, tq, LANES), BF16), pltpu.VMEM((n_heads, tq, LANES), F32),
                        pltpu.VMEM((n_heads, tq, LANES), F32)],
        compiler_params=_params("parallel", "parallel", "arbitrary"),
        name="fox_attn",
    )(q, k, v, feat, feat, ones, *riders)
    return out[0], out[1:]


def _s5_kernel(u0_ref, u1_ref, bst_ref, cst_ref, are_ref, aim_ref, d_ref, o0_ref, o1_ref,
               st_ref, *x_refs):
    rows = u0_ref.shape[0]
    nb = st_ref.shape[0] // 2
    sub = rows // len(x_refs)
    hrows = sub // 2

    @pl.when(pl.program_id(0) == 0)
    def _():
        st_ref[...] = jnp.zeros_like(st_ref)

    def half_rows(ref, base, h, *lead):
        return ref[(*lead, pl.ds(base + h, hrows, stride=2), slice(None))]

    def project(k):
        u_h = []
        for h in range(2):
            u = jnp.concatenate([half_rows(u0_ref, k * sub, h), half_rows(u1_ref, k * sub, h)],
                                axis=1)
            u_h.append(u)
            bu = _dot(u.astype(BF16), bst_ref[h])
            for c in range(2 * nb):
                x_refs[k][c, pl.ds(h, hrows, stride=2), :] = bu[:, c * LANES:(c + 1) * LANES]
        return u_h

    a_re = [are_ref[:, c * LANES:(c + 1) * LANES] for c in range(nb)]
    a_im = [aim_ref[:, c * LANES:(c + 1) * LANES] for c in range(nb)]

    def scan(k, state):
        state = list(state)
        x_ref = x_refs[k]
        for i in range(sub // SUBLANES):
            r = pl.ds(i * SUBLANES, SUBLANES)
            for c in range(nb):
                s_re, s_im = state[2 * c], state[2 * c + 1]
                n_re = a_re[c] * s_re - a_im[c] * s_im + x_ref[c, r, :]
                n_im = a_re[c] * s_im + a_im[c] * s_re + x_ref[nb + c, r, :]
                x_ref[c, r, :] = n_re
                x_ref[nb + c, r, :] = n_im
                state[2 * c], state[2 * c + 1] = n_re, n_im
        return state

    def emit(k, u_h):
        for h in range(2):
            xs = jnp.concatenate([half_rows(x_refs[k], 0, h, c) for c in range(2 * nb)], axis=1)
            y = _dot(xs.astype(BF16), cst_ref[h])
            g = jax.nn.gelu(y + d_ref[h:h + 1, :] * u_h[h])
            o0_ref[pl.ds(k * sub + h, hrows, stride=2), :] = g[:, 0:LANES]
            o1_ref[pl.ds(k * sub + h, hrows, stride=2), :] = g[:, LANES:2 * LANES]

    state = [st_ref[c] for c in range(2 * nb)]
    u_next = project(0)
    for k in range(len(x_refs)):
        u_k = u_next
        if k + 1 < len(x_refs):
            u_next = project(k + 1)
        state = scan(k, state)
        emit(k, u_k)
    for c in range(2 * nb):
        st_ref[c] = state[c]


def _s5_zoh_kernel(lre_ref, lim_ref, ldt_ref, bre_ref, bim_ref, are_ref, aim_ref, bbre_ref, bbim_ref):
    lr, li = lre_ref[...], lim_ref[...]
    dt = jnp.exp(ldt_ref[...])
    mag = jnp.exp(lr * dt)
    ab_re = mag * jnp.cos(li * dt)
    ab_im = mag * jnp.sin(li * dt)
    den = lr * lr + li * li
    nr = ab_re - 1.0
    ni = ab_im
    q_re = (nr * lr + ni * li) / den
    q_im = (ni * lr - nr * li) / den
    b_re, b_im = bre_ref[...], bim_ref[...]
    are_ref[...] = ab_re
    aim_ref[...] = ab_im
    bbre_ref[...] = q_re * b_re - q_im * b_im
    bbim_ref[...] = q_re * b_im + q_im * b_re


def _s5_discretize(lam_re, lam_im, log_dt, b_re, b_im):
    G, P, H = b_re.shape
    rep = lambda a: jnp.repeat(a, H, axis=0)
    to_rows = lambda b: jnp.transpose(b, (0, 2, 1)).reshape(G * H, P)
    args = (rep(lam_re), rep(lam_im), rep(jnp.broadcast_to(log_dt[:, None], (G, P))),
            to_rows(b_re), to_rows(b_im))
    ab_re, ab_im, bbt_re, bbt_im = pl.pallas_call(
        _s5_zoh_kernel,
        out_shape=[jax.ShapeDtypeStruct((G * H, P), F32)] * 4,
        name="s5_zoh",
    )(*args)
    return ab_re[::H], ab_im[::H], bbt_re, bbt_im


def _s5_matrices(ab_re, ab_im, bbt_re, bbt_im, c_re, c_im, d, batch):
    G, H, P = c_re.shape
    gh = G // 2
    hw, ns = gh * H, gh * P
    bmask = np.kron(np.eye(gh, dtype=np.float32), np.ones((H, P), np.float32))

    def bmat(bbt):
        return jnp.tile(bbt.reshape(2, hw, P), (1, 1, gh)) * bmask

    def cmat(cc):
        t = jnp.transpose(cc.reshape(2, gh, H, P), (0, 1, 3, 2)).reshape(2, ns, H)
        return jnp.tile(t, (1, 1, gh)) * bmask.T

    bst = jnp.concatenate([bmat(bbt_re), bmat(bbt_im)], axis=2).astype(BF16)
    cst = jnp.concatenate([cmat(c_re), -cmat(c_im)], axis=1).astype(BF16)
    a_re = jnp.tile(ab_re.reshape(2, ns), (batch, 1))
    a_im = jnp.tile(ab_im.reshape(2, ns), (batch, 1))
    return bst, cst, a_re, a_im, d.reshape(2, hw)


def _s5_scan(u0, u1, bst, cst, a_re, a_im, d2):
    n_rows = u0.shape[0]
    ns = a_re.shape[1]
    rows = min(S5_TILE * SUBLANES, n_rows)
    blk = pl.BlockSpec((rows, LANES), lambda t: (t, 0))
    return pl.pallas_call(
        _s5_kernel,
        grid=(n_rows // rows,),
        in_specs=[blk, blk, _const_spec(bst.shape), _const_spec(cst.shape),
                  _const_spec(a_re.shape), _const_spec(a_im.shape), _const_spec(d2.shape)],
        out_specs=[blk, blk],
        out_shape=[jax.ShapeDtypeStruct((n_rows, LANES), F32)] * 2,
        scratch_shapes=[pltpu.VMEM((2 * ns // LANES, SUBLANES, LANES), F32)]
                       + [pltpu.VMEM((2 * ns // LANES, rows // S5_SUBTILES, LANES), F32)] * S5_SUBTILES,
        compiler_params=_params("arbitrary"),
        name="s5_scan",
    )(u0, u1, bst, cst, a_re, a_im, d2)


def _mlp_rows(x, gpre_ref, gpost_ref, w1_ref, w2_ref):
    h = _rms(x, gpre_ref[...]).astype(BF16)
    d_ff = w1_ref.shape[1]
    acc = jnp.zeros(x.shape, F32)
    for f in range(0, d_ff, MLP_FF_CHUNK):
        a = jnp.maximum(_dot(h, w1_ref[:, f:f + MLP_FF_CHUNK]), 0.0)
        acc = acc + _dot((a * a).astype(BF16), w2_ref[f:f + MLP_FF_CHUNK, :])
    return x + _rms(acc, gpost_ref[...])


def _even_post_kernel(x_ref, g0_ref, g1_ref, yb_ref, wglu_ref, wout_ref, g_ref,
                      gpre_ref, gpost_ref, w1_ref, w2_ref, o_ref):
    nb, tm, _ = x_ref.shape
    half_w = wout_ref.shape[0] // 2

    def mixer_tail(b):
        blocks = [ref[pl.ds(2 * b + half, tm, stride=2 * nb), :]
                  for half in range(2) for ref in (g0_ref, g1_ref)]
        ga = jnp.concatenate(blocks, axis=1)
        ya = ga * jax.nn.sigmoid(_dot(ga.astype(BF16), wglu_ref[...]))
        mix = (_dot(ya.astype(BF16), wout_ref[0:half_w, :])
               + _dot(yb_ref[b], wout_ref[half_w:2 * half_w, :]))
        return x_ref[b] + _rms(mix, g_ref[...])

    x1 = jnp.concatenate([mixer_tail(b) for b in range(nb)], axis=0)
    out = _mlp_rows(x1, gpre_ref, gpost_ref, w1_ref, w2_ref)
    for b in range(nb):
        o_ref[b] = out[b * tm:(b + 1) * tm]


def _even_post(x, g0, g1, yb, w_glu, w_out, g, g_pre, g_post, w1s, w2s, layer):
    B, L, D = x.shape
    tm = min(POST_TILE, L)
    tok = lambda r: (0, r, 0)
    gblk = pl.BlockSpec((tm * 2 * B, LANES), lambda r: (r, 0))
    consts = (w_glu, w_out, g, g_pre, g_post)
    return pl.pallas_call(
        _even_post_kernel,
        grid=(L // tm,),
        in_specs=[pl.BlockSpec((B, tm, D), tok), gblk, gblk, pl.BlockSpec((B, tm, 512), tok)]
                 + [_const_spec(c.shape) for c in consts]
                 + [_layer_spec(w1s.shape, layer), _layer_spec(w2s.shape, layer)],
        out_specs=pl.BlockSpec((B, tm, D), tok),
        out_shape=jax.ShapeDtypeStruct((B, L, D), F32),
        compiler_params=_params("parallel"),
        name="even_post_mlp",
    )(x, g0, g1, yb, *consts, w1s, w2s)


def _odd_inproj_rows(x, gpre_ref, win_ref):
    width = win_ref.shape[1] // 3
    h = _rms(x, gpre_ref[...]).astype(BF16)
    return tuple(_dot(h, win_ref[:, n * width:(n + 1) * width]) for n in range(3))


def _odd_mix_rows(x, xc, u, v, t0, pool_ref, pw_ref, ps_ref, lng_ref, lnb_ref, ws, bs_ref,
                  wout_ref, gpost_ref):
    tm = x.shape[0]
    pad = max(POOL_WINDOWS)
    gd = LANES
    n_grp = len(POOL_WINDOWS)
    width = n_grp * gd

    pool_ref[pad:pad + tm, :] = xc
    t = t0 + lax.broadcasted_iota(jnp.int32, (tm, 1), 0)
    yc = []
    for g, w in enumerate(POOL_WINDOWS):
        cols = slice(g * gd, (g + 1) * gd)
        tot = xc[:, cols]
        for lag in range(1, w):
            tot = tot + pool_ref[pl.ds(pad - lag, tm), cols]
        cnt = jnp.minimum(t + 1, w).astype(F32)
        pooled = tot / cnt - xc[:, cols]
        yc.append(_dot(pooled.astype(BF16), pw_ref[g]))
    pool_ref[0:pad, :] = pool_ref[tm:tm + pad, :]
    y_c = jnp.concatenate(yc, axis=-1) * ps_ref[...]

    gv = jax.nn.gelu(v)
    mu = jnp.mean(gv, axis=-1, keepdims=True)
    vc = gv - mu
    vn = vc * lax.rsqrt(jnp.mean(vc * vc, axis=-1, keepdims=True) + EPS)
    vn = (vn * lng_ref[...] + lnb_ref[...]).astype(BF16)
    chunks = []
    for n in range(tm // CHUNK):
        parts = [_dot(ws[g], vn[n * CHUNK:(n + 1) * CHUNK, g * gd:(g + 1) * gd]) for g in range(n_grp)]
        chunks.append(jnp.concatenate(parts, axis=-1) + bs_ref[...])
    y_d = jax.nn.gelu(u) * jnp.concatenate(chunks, axis=0)

    mix = _dot(y_c.astype(BF16), wout_ref[0:width, :]) + _dot(y_d.astype(BF16), wout_ref[width:2 * width, :])
    return x + _rms(mix, gpost_ref[...])


def _odd_layer_kernel(x_ref, gpre_ref, win_ref, pw_ref, ps_ref, lng_ref, lnb_ref, ws_ref,
                      bs_ref, wout_ref, gpost_ref, mgpre_ref, mgpost_ref, w1_ref, w2_ref,
                      o_ref, pool_ref):
    r = pl.program_id(0)
    nb, tm, _ = x_ref.shape
    pad = max(POOL_WINDOWS)

    @pl.when(r == 0)
    def _():
        pool_ref[:, 0:pad, :] = jnp.zeros((nb, pad, pool_ref.shape[2]), F32)

    rows = lax.broadcasted_iota(jnp.int32, (CHUNK, CHUNK), 0)
    cols = lax.broadcasted_iota(jnp.int32, (CHUNK, CHUNK), 1)
    ws = [jnp.where(cols <= rows, ws_ref[g], 0.0).astype(BF16) for g in range(ws_ref.shape[0])]
    proj = {0: _odd_inproj_rows(x_ref[0], gpre_ref, win_ref)}
    x1 = []
    for b in range(nb):
        if b + 1 < nb:
            proj[b + 1] = _odd_inproj_rows(x_ref[b + 1], gpre_ref, win_ref)
        if len(x1) == MLP_GROUP:
            out = _mlp_rows(jnp.concatenate(x1, axis=0), mgpre_ref, mgpost_ref, w1_ref, w2_ref)
            for n in range(MLP_GROUP):
                o_ref[b - MLP_GROUP + n] = out[n * tm:(n + 1) * tm]
            x1 = []
        x1.append(_odd_mix_rows(x_ref[b], *proj.pop(b), r * tm, pool_ref.at[b], pw_ref, ps_ref,
                                lng_ref, lnb_ref, ws, bs_ref, wout_ref, gpost_ref))
    out = _mlp_rows(jnp.concatenate(x1, axis=0), mgpre_ref, mgpost_ref, w1_ref, w2_ref)
    for n in range(len(x1)):
        o_ref[nb - len(x1) + n] = out[n * tm:(n + 1) * tm]


def _odd_layer_call(x, consts, w1s, w2s, layer):
    B, L, D = x.shape
    tm = min(POST_TILE, L)
    width = consts[3].shape[1]
    tok = lambda r: (0, r, 0)
    return pl.pallas_call(
        _odd_layer_kernel,
        grid=(L // tm,),
        in_specs=[pl.BlockSpec((B, tm, D), tok)] + [_const_spec(c.shape) for c in consts]
                 + [_layer_spec(w1s.shape, layer), _layer_spec(w2s.shape, layer)],
        out_specs=pl.BlockSpec((B, tm, D), tok),
        out_shape=jax.ShapeDtypeStruct((B, L, D), F32),
        scratch_shapes=[pltpu.VMEM((B, tm + max(POOL_WINDOWS), width), F32)],
        compiler_params=_params("arbitrary"),
        name="odd_layer",
    )(x, *consts, w1s, w2s)


def _even_layer(x, g_pre, g_post, w_in, lam_re, lam_im, log_dt, b_re, b_im, c_re, c_im, d,
                b_f, mlp_g_pre, mlp_g_post, late, i, layer):
    B, L, D = x.shape
    n_heads = b_f.shape[0]
    s5w = d.shape[0]
    fw = n_heads * HEAD_DIM
    assert s5w == 4 * LANES and w_in.shape[1] == s5w + 3 * fw + n_heads
    wf = jnp.pad(w_in[:, s5w + 3 * fw:], ((0, 0), (0, LANES - n_heads))).astype(BF16)
    assert B <= SUBLANES
    u0, u1, q, k, v, feat, ones = _even_inproj(x, g_pre[None], w_in.astype(BF16), wf, b_f)

    todo = [n for n, w in late.items() if w.dtype != BF16]
    y_b, cast = _fox_attn(q, k, v, feat, ones,
                          tuple(late[n].reshape(-1, late[n].shape[-1]) for n in todo))
    late = {**late, **{n: c.reshape(late[n].shape) for n, c in zip(todo, cast)}}

    ab_re, ab_im, bbt_re, bbt_im = _s5_discretize(lam_re, lam_im, log_dt, b_re, b_im)
    bst, cst, a_re, a_im, d2 = _s5_matrices(ab_re, ab_im, bbt_re, bbt_im, c_re, c_im, d, B)
    g0, g1 = _s5_scan(u0, u1, bst, cst, a_re, a_im, d2)

    x = _even_post(x, g0, g1, y_b, late["w_glu"][i], late["w_out_even"][i], g_post[None],
                   mlp_g_pre[None], mlp_g_post[None], late["mlp_w1"], late["mlp_w2"], layer)
    return x, late


def _odd_layer(x, g_pre, g_post, pool_scale, ln_g, ln_b, w_s, b_s, mlp_g_pre, mlp_g_post,
               late, i, layer):
    bs_full = jnp.repeat(jnp.transpose(b_s), LANES, axis=1)
    consts = (g_pre[None], late["w_in_odd"][i], late["pool_w"][i], pool_scale[None], ln_g[None],
              ln_b[None], w_s, bs_full, late["w_out_odd"][i], g_post[None], mlp_g_pre[None],
              mlp_g_post[None])
    return _odd_layer_call(x, consts, late["mlp_w1"], late["mlp_w2"], layer)


def kernel(x, mix_pre_g, mix_post_g, mlp_pre_g, mlp_post_g, w_in_even, s5_lam_re, s5_lam_im, s5_log_dt, s5_b_re, s5_b_im, s5_c_re, s5_c_im, s5_d, s5_w_glu, fox_b_f, w_out_even, w_in_odd, pool_w, pool_scale, sgu_ln_g, sgu_ln_b, sgu_w_s, sgu_b_s, w_out_odd, mlp_w1, mlp_w2):
    depth = mix_pre_g.shape[0]
    late = dict(mlp_w1=mlp_w1, mlp_w2=mlp_w2, w_glu=s5_w_glu, w_out_even=w_out_even,
                w_in_odd=w_in_odd, pool_w=pool_w, w_out_odd=w_out_odd)
    for l in range(depth):
        i = l // 2
        if l % 2 == 0:
            x, late = _even_layer(
                x, mix_pre_g[l], mix_post_g[l], w_in_even[i], s5_lam_re[i], s5_lam_im[i],
                s5_log_dt[i], s5_b_re[i], s5_b_im[i], s5_c_re[i], s5_c_im[i], s5_d[i],
                fox_b_f[i], mlp_pre_g[l], mlp_post_g[l], late, i, l)
        else:
            x = _odd_layer(x, mix_pre_g[l], mix_post_g[l], pool_scale[i], sgu_ln_g[i], sgu_ln_b[i],
                           sgu_w_s[i], sgu_b_s[i], mlp_pre_g[l], mlp_post_g[l], late, i, l)
    return x
```

```python
import numpy as np
import jax
import jax.numpy as jnp
from jax import lax
from jax.experimental import pallas as pl
from jax.experimental.pallas import tpu as pltpu

F32 = jnp.float32
BF16 = jnp.bfloat16

EPS = 1e-6
LOG2E = 1.4426950408889634
NEG_BIG = -1e30

LANES = 128
SUBLANES = 8
VMEM_LIMIT = 56 * 1024 * 1024

HEAD_DIM = 64
POOL_WINDOWS = (2, 4, 8, 16)
CHUNK = 128

ROW_TILE = 512
POST_TILE = 256
MLP_FF_CHUNK = 1024
MLP_GROUP = 2
S5_TILE = 256
S5_SUBTILES = 2
ATT_TQ = 1024
ATT_TK = 512
ATT_PAIRS = 2


def _params(*sem):
    return pltpu.CompilerParams(dimension_semantics=sem, vmem_limit_bytes=VMEM_LIMIT)


def _const_spec(shape):
    nd = len(shape)
    return pl.BlockSpec(shape, lambda *_: (0,) * nd, pipeline_mode=pl.Buffered(1))


def _layer_spec(shape, layer):
    nd = len(shape)
    return pl.BlockSpec((None,) + tuple(shape[1:]), lambda *_: (layer,) + (0,) * (nd - 1),
                        pipeline_mode=pl.Buffered(1))


def _rms(x, g):
    return x * lax.rsqrt(jnp.mean(x * x, axis=-1, keepdims=True) + EPS) * g


def _split3(x):
    hi = x.astype(BF16)
    r1 = x - hi.astype(F32)
    mid = r1.astype(BF16)
    lo = (r1 - mid.astype(F32)).astype(BF16)
    return hi, mid, lo


def _dot(a, b):
    return jnp.dot(a, b, preferred_element_type=F32)


def _even_inproj_kernel(x_ref, g_ref, w_ref, wf_ref, bf_ref, tri_ref, place_ref,
                        u0_ref, u1_ref, q_ref, k_ref, v_ref, feat_ref, carry_ref):
    nb, tm, _ = x_ref.shape
    s5w = 4 * LANES
    fw = q_ref.shape[2]
    q_scale = HEAD_DIM ** -0.5 * LOG2E

    @pl.when(pl.program_id(0) == 0)
    def _():
        carry_ref[...] = jnp.zeros_like(carry_ref)

    def running_sum(b, z):
        logf = jnp.minimum(z, 0.0) - jnp.log1p(jnp.exp(-jnp.abs(z)))
        c3 = _dot(tri_ref[...], jnp.concatenate(_split3(logf), axis=1))
        csum = (c3[:, 0:LANES] + c3[:, LANES:2 * LANES] + c3[:, 2 * LANES:3 * LANES]
                + carry_ref[b:b + 1, :])
        carry_ref[b:b + 1, :] = csum[tm - 1:tm, :]
        return csum

    def place(b, csum):
        pieces = jnp.concatenate(_split3(csum * LOG2E), axis=1)
        feat_ref[b] = _dot(pieces, place_ref[...]).astype(BF16)

    z_prev = None
    for b in range(nb):
        h = _rms(x_ref[b], g_ref[...]).astype(BF16)
        z = _dot(h, wf_ref[...]) + bf_ref[...]
        zu = _dot(h, w_ref[:, 0:s5w])
        for half in range(2):
            rows = pl.ds(2 * b + half, tm, stride=2 * nb)
            u0_ref[rows, :] = zu[:, 2 * half * LANES:(2 * half + 1) * LANES]
            u1_ref[rows, :] = zu[:, (2 * half + 1) * LANES:(2 * half + 2) * LANES]
        if z_prev is not None:
            csum = running_sum(b - 1, z_prev)
        q_ref[b] = (_dot(h, w_ref[:, s5w:s5w + fw]) * q_scale).astype(BF16)
        k_ref[b] = _dot(h, w_ref[:, s5w + fw:s5w + 2 * fw]).astype(BF16)
        if z_prev is not None:
            place(b - 1, csum)
        v_ref[b] = _dot(h, w_ref[:, s5w + 2 * fw:s5w + 3 * fw]).astype(BF16)
        z_prev = z
    place(nb - 1, running_sum(nb - 1, z_prev))


def _even_inproj(x, g, w, wf, b_f):
    B, L, D = x.shape
    n_heads = b_f.shape[0]
    tm = min(ROW_TILE, L)
    tok = lambda r: (0, r, 0)
    ublk = pl.BlockSpec((tm * 2 * B, LANES), lambda r: (r, 0))
    aspec = pl.BlockSpec((B, tm, 512), tok)
    place, ones = _fox_placement(n_heads)
    tri = jnp.asarray(np.tril(np.ones((tm, tm), np.float32)), BF16)
    bf = jnp.zeros((1, LANES), F32).at[0, :n_heads].set(b_f)
    consts = (g, w, wf, bf, tri, jnp.asarray(place, BF16))
    out = pl.pallas_call(
        _even_inproj_kernel,
        grid=(L // tm,),
        in_specs=[pl.BlockSpec((B, tm, D), tok)] + [_const_spec(c.shape) for c in consts],
        out_specs=[ublk, ublk, aspec, aspec, aspec, pl.BlockSpec((B, tm, place.shape[1]), tok)],
        out_shape=[
            jax.ShapeDtypeStruct((L * 2 * B, LANES), F32),
            jax.ShapeDtypeStruct((L * 2 * B, LANES), F32),
            jax.ShapeDtypeStruct((B, L, 512), BF16),
            jax.ShapeDtypeStruct((B, L, 512), BF16),
            jax.ShapeDtypeStruct((B, L, 512), BF16),
            jax.ShapeDtypeStruct((B, L, place.shape[1]), BF16),
        ],
        scratch_shapes=[pltpu.VMEM((SUBLANES, LANES), F32)],
        compiler_params=_params("arbitrary"),
        name="even_inproj",
    )(x, *consts)
    return (*out, jnp.asarray(ones))


def _fox_placement(n_heads):
    width = (n_heads // 2) * LANES
    place = np.zeros((3 * LANES, width), np.float32)
    ones = np.zeros((2, LANES), np.float32)
    for h in range(n_heads):
        base = HEAD_DIM if h % 2 == 0 else 0
        for j in range(3):
            place[j * LANES + h, (h // 2) * LANES + base + j] = 1.0
            place[j * LANES + h, (h // 2) * LANES + base + 3 + j] = -1.0
            ones[0, base + 3 + j] = 1.0
            ones[1, base + j] = 1.0
    return place, ones


def _fox_attn_kernel(q_ref, k_ref, v_ref, fq_ref, fk_ref, ones_ref, *refs):
    n_riders = (len(refs) - 6) // 2
    o_ref = refs[n_riders]
    kaug_ref, vaug_ref, qaug_ref, m_ref, acc_ref = refs[2 * n_riders + 1:]
    for src, dst in zip(refs[:n_riders], refs[n_riders + 1:2 * n_riders + 1]):
        dst[...] = src[...].astype(BF16)

    qi = pl.program_id(2)
    tq = q_ref.shape[1]
    L = k_ref.shape[1]
    tk = min(ATT_TK, L)
    n_heads = kaug_ref.shape[0]
    assert tq % tk == 0

    def merge(lanes, first, x_ref, f):
        x = x_ref[0, :, first * LANES:(first + 1) * LANES].astype(F32)
        return (jnp.where(lanes < HEAD_DIM, x, f(0)).astype(BF16),
                jnp.where(lanes >= HEAD_DIM, x, f(1)).astype(BF16))

    def features(f_ref, pr, side):
        one = ones_ref[side:side + 1, :]
        return f_ref[0, :, pr * LANES:(pr + 1) * LANES].astype(F32) * (1.0 - one) + one

    @pl.when(qi == 0)
    def _():
        lanes = lax.broadcasted_iota(jnp.int32, (L, LANES), 1)
        for pr in range(n_heads // 2):
            fk = features(fk_ref, pr, 1)
            kaug_ref[2 * pr], kaug_ref[2 * pr + 1] = merge(lanes, pr, k_ref, lambda e: fk)
            ones = ((lanes == HEAD_DIM).astype(F32), (lanes == 0).astype(F32))
            vaug_ref[2 * pr], vaug_ref[2 * pr + 1] = merge(lanes, pr, v_ref, lambda e: ones[e])

    lanes_q = lax.broadcasted_iota(jnp.int32, (tq, LANES), 1)
    for pr in range(n_heads // 2):
        fq = features(fq_ref, pr, 0)
        qaug_ref[2 * pr], qaug_ref[2 * pr + 1] = merge(lanes_q, pr, q_ref, lambda e: fq)

    def tile(j, r0, r1, masked, first=False):
        start = pl.multiple_of(j * tk, tk)
        for e in range(n_heads):
            kt = kaug_ref[e, pl.ds(start, tk), :]
            s = lax.dot_general(qaug_ref[e, r0:r1, :], kt, (((1,), (1,)), ((), ())),
                                preferred_element_type=F32)
            if masked:
                row = lax.broadcasted_iota(jnp.int32, s.shape, 0)
                col = lax.broadcasted_iota(jnp.int32, s.shape, 1)
                s = jnp.where(col <= row, s, NEG_BIG)
            m_new = jnp.broadcast_to(jnp.max(s, axis=1, keepdims=True), (r1 - r0, LANES))
            if not first:
                m = m_ref[e, r0:r1, :]
                m_new = jnp.maximum(m, m_new)
            p = jnp.exp2((s - jnp.concatenate([m_new] * (tk // LANES), axis=1)).astype(BF16))
            pv = _dot(p, vaug_ref[e, pl.ds(start, tk), :])
            acc_ref[e, r0:r1, :] = pv if first else jnp.exp2(m - m_new) * acc_ref[e, r0:r1, :] + pv
            m_ref[e, r0:r1, :] = m_new

    n_diag = tq // tk

    tile(qi * n_diag, 0, tk, True, first=True)
    if tk < tq:
        tile(qi * n_diag, tk, tq, False, first=True)

    def body(j, c):
        for d in range(n_diag):
            tile(j * n_diag + d, 0, tq, False)
        return c

    lax.fori_loop(0, qi, body, 0)
    for d in range(1, n_diag):
        tile(qi * n_diag + d, d * tk, (d + 1) * tk, True)
        if (d + 1) * tk < tq:
            tile(qi * n_diag + d, (d + 1) * tk, tq, False)

    for pr in range(n_heads // 2):
        acc0, acc1 = acc_ref[2 * pr], acc_ref[2 * pr + 1]
        out0 = acc0 / acc0[:, HEAD_DIM:HEAD_DIM + 1]
        out1 = acc1 / acc1[:, 0:1]
        o_ref[0, :, pr * LANES:(pr + 1) * LANES] = jnp.where(lanes_q < HEAD_DIM, out0, out1).astype(BF16)


def _fox_attn(q, k, v, feat, ones, riders):
    B, L, W = q.shape
    bw = ATT_PAIRS * LANES
    n_heads = 2 * ATT_PAIRS
    tq = min(ATT_TQ, L)
    grid = (B, W // bw, L // tq)
    n_steps = grid[0] * grid[1] * grid[2]
    qspec = pl.BlockSpec((1, tq, bw), lambda b, p, i: (b, i, p))
    kspec = pl.BlockSpec((1, L, bw), lambda b, p, i: (b, 0, p))
    step = lambda b, p, i: ((b * grid[1] + p) * grid[2] + i, 0)
    rspecs = [pl.BlockSpec((r.shape[0] // n_steps, r.shape[1]), step) for r in riders]
    out = pl.pallas_call(
        _fox_attn_kernel,
        grid=grid,
        in_specs=[qspec, kspec, kspec, qspec, kspec, _const_spec(ones.shape)] + rspecs,
        out_specs=[qspec] + rspecs,
        out_shape=[jax.ShapeDtypeStruct((B, L, W), BF16)]
                  + [jax.ShapeDtypeStruct(r.shape, BF16) for r in riders],
        scratch_shapes=[pltpu.VMEM((n_heads, L, LANES), BF16), pltpu.VMEM((n_heads, L, LANES), BF16),
                        pltpu.VMEM((n_heads, tq, LANES), BF16), pltpu.VMEM((n_heads, tq, LANES), F32),
                        pltpu.VMEM((n_heads, tq, LANES), F32)],
        compiler_params=_params("parallel", "parallel", "arbitrary"),
        name="fox_attn",
    )(q, k, v, feat, feat, ones, *riders)
    return out[0], out[1:]


def _s5_kernel(u0_ref, u1_ref, bst_ref, cst_ref, are_ref, aim_ref, d_ref, o0_ref, o1_ref,
               st_ref, *x_refs):
    rows = u0_ref.shape[0]
    nb = st_ref.shape[0] // 2
    sub = rows // len(x_refs)
    hrows = sub // 2

    @pl.when(pl.program_id(0) == 0)
    def _():
        st_ref[...] = jnp.zeros_like(st_ref)

    def half_rows(ref, base, h, *lead):
        return ref[(*lead, pl.ds(base + h, hrows, stride=2), slice(None))]

    def project(k):
        u_h = []
        for h in range(2):
            u = jnp.concatenate([half_rows(u0_ref, k * sub, h), half_rows(u1_ref, k * sub, h)],
                                axis=1)
            u_h.append(u)
            bu = _dot(u.astype(BF16), bst_ref[h])
            for c in range(2 * nb):
                x_refs[k][c, pl.ds(h, hrows, stride=2), :] = bu[:, c * LANES:(c + 1) * LANES]
        return u_h

    a_re = [are_ref[:, c * LANES:(c + 1) * LANES] for c in range(nb)]
    a_im = [aim_ref[:, c * LANES:(c + 1) * LANES] for c in range(nb)]

    def scan(k, state):
        state = list(state)
        x_ref = x_refs[k]
        for i in range(sub // SUBLANES):
            r = pl.ds(i * SUBLANES, SUBLANES)
            for c in range(nb):
                s_re, s_im = state[2 * c], state[2 * c + 1]
                n_re = a_re[c] * s_re - a_im[c] * s_im + x_ref[c, r, :]
                n_im = a_re[c] * s_im + a_im[c] * s_re + x_ref[nb + c, r, :]
                x_ref[c, r, :] = n_re
                x_ref[nb + c, r, :] = n_im
                state[2 * c], state[2 * c + 1] = n_re, n_im
        return state

    def emit(k, u_h):
        for h in range(2):
            xs = jnp.concatenate([half_rows(x_refs[k], 0, h, c) for c in range(2 * nb)], axis=1)
            y = _dot(xs.astype(BF16), cst_ref[h])
            g = jax.nn.gelu(y + d_ref[h:h + 1, :] * u_h[h])
            o0_ref[pl.ds(k * sub + h, hrows, stride=2), :] = g[:, 0:LANES]
            o1_ref[pl.ds(k * sub + h, hrows, stride=2), :] = g[:, LANES:2 * LANES]

    state = [st_ref[c] for c in range(2 * nb)]
    u_next = project(0)
    for k in range(len(x_refs)):
        u_k = u_next
        if k + 1 < len(x_refs):
            u_next = project(k + 1)
        state = scan(k, state)
        emit(k, u_k)
    for c in range(2 * nb):
        st_ref[c] = state[c]


def _s5_zoh_kernel(lre_ref, lim_ref, ldt_ref, bre_ref, bim_ref, are_ref, aim_ref, bbre_ref, bbim_ref):
    lr, li = lre_ref[...], lim_ref[...]
    dt = jnp.exp(ldt_ref[...])
    mag = jnp.exp(lr * dt)
    ab_re = mag * jnp.cos(li * dt)
    ab_im = mag * jnp.sin(li * dt)
    den = lr * lr + li * li
    nr = ab_re - 1.0
    ni = ab_im
    q_re = (nr * lr + ni * li) / den
    q_im = (ni * lr - nr * li) / den
    b_re, b_im = bre_ref[...], bim_ref[...]
    q_re, q_im = q_re[:, None, :], q_im[:, None, :]
    are_ref[...] = ab_re
    aim_ref[...] = ab_im
    bbre_ref[...] = q_re * b_re - q_im * b_im
    bbim_ref[...] = q_re * b_im + q_im * b_re


def _s5_discretize(lam_re, lam_im, log_dt, b_re, b_im):
    G, P, H = b_re.shape
    ab_re, ab_im, bbt_re, bbt_im = pl.pallas_call(
        _s5_zoh_kernel,
        out_shape=[jax.ShapeDtypeStruct((G, P), F32)] * 2 + [jax.ShapeDtypeStruct((G, H, P), F32)] * 2,
        name="s5_zoh",
    )(lam_re, lam_im, log_dt[:, None], jnp.transpose(b_re, (0, 2, 1)), jnp.transpose(b_im, (0, 2, 1)))
    return ab_re, ab_im, bbt_re.reshape(G * H, P), bbt_im.reshape(G * H, P)


def _s5_matrices(ab_re, ab_im, bbt_re, bbt_im, c_re, c_im, d, batch):
    G, H, P = c_re.shape
    gh = G // 2
    hw, ns = gh * H, gh * P
    bmask = np.kron(np.eye(gh, dtype=np.float32), np.ones((H, P), np.float32))

    def bmat(bbt):
        return jnp.tile(bbt.reshape(2, hw, P), (1, 1, gh)) * bmask

    def cmat(cc):
        t = jnp.transpose(cc.reshape(2, gh, H, P), (0, 1, 3, 2)).reshape(2, ns, H)
        return jnp.tile(t, (1, 1, gh)) * bmask.T

    bst = jnp.concatenate([bmat(bbt_re), bmat(bbt_im)], axis=2).astype(BF16)
    cst = jnp.concatenate([cmat(c_re), -cmat(c_im)], axis=1).astype(BF16)
    a_re = jnp.tile(ab_re.reshape(2, ns), (batch, 1))
    a_im = jnp.tile(ab_im.reshape(2, ns), (batch, 1))
    return bst, cst, a_re, a_im, d.reshape(2, hw)


def _s5_scan(u0, u1, bst, cst, a_re, a_im, d2):
    n_rows = u0.shape[0]
    ns = a_re.shape[1]
    rows = min(S5_TILE * SUBLANES, n_rows)
    blk = pl.BlockSpec((rows, LANES), lambda t: (t, 0))
    return pl.pallas_call(
        _s5_kernel,
        grid=(n_rows // rows,),
        in_specs=[blk, blk, _const_spec(bst.shape), _const_spec(cst.shape),
                  _const_spec(a_re.shape), _const_spec(a_im.shape), _const_spec(d2.shape)],
        out_specs=[blk, blk],
        out_shape=[jax.ShapeDtypeStruct((n_rows, LANES), F32)] * 2,
        scratch_shapes=[pltpu.VMEM((2 * ns // LANES, SUBLANES, LANES), F32)]
                       + [pltpu.VMEM((2 * ns // LANES, rows // S5_SUBTILES, LANES), F32)] * S5_SUBTILES,
        compiler_params=_params("arbitrary"),
        name="s5_scan",
    )(u0, u1, bst, cst, a_re, a_im, d2)


def _mlp_rows(x, gpre_ref, gpost_ref, w1_ref, w2_ref):
    h = _rms(x, gpre_ref[...]).astype(BF16)
    d_ff = w1_ref.shape[1]
    acc = jnp.zeros(x.shape, F32)
    for f in range(0, d_ff, MLP_FF_CHUNK):
        a = jnp.maximum(_dot(h, w1_ref[:, f:f + MLP_FF_CHUNK]), 0.0)
        acc = acc + _dot((a * a).astype(BF16), w2_ref[f:f + MLP_FF_CHUNK, :])
    return x + _rms(acc, gpost_ref[...])


def _even_post_kernel(x_ref, g0_ref, g1_ref, yb_ref, wglu_ref, wout_ref, g_ref,
                      gpre_ref, gpost_ref, w1_ref, w2_ref, o_ref):
    nb, tm, _ = x_ref.shape
    half_w = wout_ref.shape[0] // 2

    def mixer_tail(b):
        blocks = [ref[pl.ds(2 * b + half, tm, stride=2 * nb), :]
                  for half in range(2) for ref in (g0_ref, g1_ref)]
        ga = jnp.concatenate(blocks, axis=1)
        ya = ga * jax.nn.sigmoid(_dot(ga.astype(BF16), wglu_ref[...]))
        mix = (_dot(ya.astype(BF16), wout_ref[0:half_w, :])
               + _dot(yb_ref[b], wout_ref[half_w:2 * half_w, :]))
        return x_ref[b] + _rms(mix, g_ref[...])

    x1 = jnp.concatenate([mixer_tail(b) for b in range(nb)], axis=0)
    out = _mlp_rows(x1, gpre_ref, gpost_ref, w1_ref, w2_ref)
    for b in range(nb):
        o_ref[b] = out[b * tm:(b + 1) * tm]


def _even_post(x, g0, g1, yb, w_glu, w_out, g, g_pre, g_post, w1s, w2s, layer):
    B, L, D = x.shape
    tm = min(POST_TILE, L)
    tok = lambda r: (0, r, 0)
    gblk = pl.BlockSpec((tm * 2 * B, LANES), lambda r: (r, 0))
    consts = (w_glu, w_out, g, g_pre, g_post)
    return pl.pallas_call(
        _even_post_kernel,
        grid=(L // tm,),
        in_specs=[pl.BlockSpec((B, tm, D), tok), gblk, gblk, pl.BlockSpec((B, tm, 512), tok)]
                 + [_const_spec(c.shape) for c in consts]
                 + [_layer_spec(w1s.shape, layer), _layer_spec(w2s.shape, layer)],
        out_specs=pl.BlockSpec((B, tm, D), tok),
        out_shape=jax.ShapeDtypeStruct((B, L, D), F32),
        compiler_params=_params("parallel"),
        name="even_post_mlp",
    )(x, g0, g1, yb, *consts, w1s, w2s)


def _odd_inproj_rows(x, gpre_ref, win_ref):
    width = win_ref.shape[1] // 3
    h = _rms(x, gpre_ref[...]).astype(BF16)
    return tuple(_dot(h, win_ref[:, n * width:(n + 1) * width]) for n in range(3))


def _odd_mix_rows(x, xc, u, v, t0, pool_ref, pw_ref, ps_ref, lng_ref, lnb_ref, ws, bs_ref,
                  wout_ref, gpost_ref):
    tm = x.shape[0]
    pad = max(POOL_WINDOWS)
    gd = LANES
    n_grp = len(POOL_WINDOWS)
    width = n_grp * gd

    pool_ref[pad:pad + tm, :] = xc
    t = t0 + lax.broadcasted_iota(jnp.int32, (tm, 1), 0)
    yc = []
    for g, w in enumerate(POOL_WINDOWS):
        cols = slice(g * gd, (g + 1) * gd)
        tot = xc[:, cols]
        for lag in range(1, w):
            tot = tot + pool_ref[pl.ds(pad - lag, tm), cols]
        cnt = jnp.minimum(t + 1, w).astype(F32)
        pooled = tot / cnt - xc[:, cols]
        yc.append(_dot(pooled.astype(BF16), pw_ref[g]))
    pool_ref[0:pad, :] = pool_ref[tm:tm + pad, :]
    y_c = jnp.concatenate(yc, axis=-1) * ps_ref[...]

    gv = jax.nn.gelu(v)
    mu = jnp.mean(gv, axis=-1, keepdims=True)
    vc = gv - mu
    vn = vc * lax.rsqrt(jnp.mean(vc * vc, axis=-1, keepdims=True) + EPS)
    vn = (vn * lng_ref[...] + lnb_ref[...]).astype(BF16)
    chunks = []
    for n in range(tm // CHUNK):
        parts = [_dot(ws[g], vn[n * CHUNK:(n + 1) * CHUNK, g * gd:(g + 1) * gd]) for g in range(n_grp)]
        chunks.append(jnp.concatenate(parts, axis=-1) + bs_ref[...])
    y_d = jax.nn.gelu(u) * jnp.concatenate(chunks, axis=0)

    mix = _dot(y_c.astype(BF16), wout_ref[0:width, :]) + _dot(y_d.astype(BF16), wout_ref[width:2 * width, :])
    return x + _rms(mix, gpost_ref[...])


def _odd_layer_kernel(x_ref, gpre_ref, win_ref, pw_ref, ps_ref, lng_ref, lnb_ref, ws_ref,
                      bs_ref, wout_ref, gpost_ref, mgpre_ref, mgpost_ref, w1_ref, w2_ref,
                      o_ref, pool_ref):
    r = pl.program_id(0)
    nb, tm, _ = x_ref.shape
    pad = max(POOL_WINDOWS)

    @pl.when(r == 0)
    def _():
        pool_ref[:, 0:pad, :] = jnp.zeros((nb, pad, pool_ref.shape[2]), F32)

    rows = lax.broadcasted_iota(jnp.int32, (CHUNK, CHUNK), 0)
    cols = lax.broadcasted_iota(jnp.int32, (CHUNK, CHUNK), 1)
    ws = [jnp.where(cols <= rows, ws_ref[g], 0.0).astype(BF16) for g in range(ws_ref.shape[0])]
    proj = {0: _odd_inproj_rows(x_ref[0], gpre_ref, win_ref)}
    x1 = []
    for b in range(nb):
        if b + 1 < nb:
            proj[b + 1] = _odd_inproj_rows(x_ref[b + 1], gpre_ref, win_ref)
        if len(x1) == MLP_GROUP:
            out = _mlp_rows(jnp.concatenate(x1, axis=0), mgpre_ref, mgpost_ref, w1_ref, w2_ref)
            for n in range(MLP_GROUP):
                o_ref[b - MLP_GROUP + n] = out[n * tm:(n + 1) * tm]
            x1 = []
        x1.append(_odd_mix_rows(x_ref[b], *proj.pop(b), r * tm, pool_ref.at[b], pw_ref, ps_ref,
                                lng_ref, lnb_ref, ws, bs_ref, wout_ref, gpost_ref))
    out = _mlp_rows(jnp.concatenate(x1, axis=0), mgpre_ref, mgpost_ref, w1_ref, w2_ref)
    for n in range(len(x1)):
        o_ref[nb - len(x1) + n] = out[n * tm:(n + 1) * tm]


def _odd_layer_call(x, consts, w1s, w2s, layer):
    B, L, D = x.shape
    tm = min(POST_TILE, L)
    width = consts[3].shape[1]
    tok = lambda r: (0, r, 0)
    return pl.pallas_call(
        _odd_layer_kernel,
        grid=(L // tm,),
        in_specs=[pl.BlockSpec((B, tm, D), tok)] + [_const_spec(c.shape) for c in consts]
                 + [_layer_spec(w1s.shape, layer), _layer_spec(w2s.shape, layer)],
        out_specs=pl.BlockSpec((B, tm, D), tok),
        out_shape=jax.ShapeDtypeStruct((B, L, D), F32),
        scratch_shapes=[pltpu.VMEM((B, tm + max(POOL_WINDOWS), width), F32)],
        compiler_params=_params("arbitrary"),
        name="odd_layer",
    )(x, *consts, w1s, w2s)


def _even_layer(x, g_pre, g_post, w_in, lam_re, lam_im, log_dt, b_re, b_im, c_re, c_im, d,
                b_f, mlp_g_pre, mlp_g_post, late, i, layer):
    B, L, D = x.shape
    n_heads = b_f.shape[0]
    s5w = d.shape[0]
    fw = n_heads * HEAD_DIM
    assert s5w == 4 * LANES and w_in.shape[1] == s5w + 3 * fw + n_heads
    wf = jnp.pad(w_in[:, s5w + 3 * fw:], ((0, 0), (0, LANES - n_heads))).astype(BF16)
    assert B <= SUBLANES
    u0, u1, q, k, v, feat, ones = _even_inproj(x, g_pre[None], w_in.astype(BF16), wf, b_f)

    todo = [n for n, w in late.items() if w.dtype != BF16]
    y_b, cast = _fox_attn(q, k, v, feat, ones,
                          tuple(late[n].reshape(-1, late[n].shape[-1]) for n in todo))
    late = {**late, **{n: c.reshape(late[n].shape) for n, c in zip(todo, cast)}}

    ab_re, ab_im, bbt_re, bbt_im = _s5_discretize(lam_re, lam_im, log_dt, b_re, b_im)
    bst, cst, a_re, a_im, d2 = _s5_matrices(ab_re, ab_im, bbt_re, bbt_im, c_re, c_im, d, B)
    g0, g1 = _s5_scan(u0, u1, bst, cst, a_re, a_im, d2)

    x = _even_post(x, g0, g1, y_b, late["w_glu"][i], late["w_out_even"][i], g_post[None],
                   mlp_g_pre[None], mlp_g_post[None], late["mlp_w1"], late["mlp_w2"], layer)
    return x, late


def _odd_layer(x, g_pre, g_post, pool_scale, ln_g, ln_b, w_s, b_s, mlp_g_pre, mlp_g_post,
               late, i, layer):
    bs_full = jnp.repeat(jnp.transpose(b_s), LANES, axis=1)
    consts = (g_pre[None], late["w_in_odd"][i], late["pool_w"][i], pool_scale[None], ln_g[None],
              ln_b[None], w_s, bs_full, late["w_out_odd"][i], g_post[None], mlp_g_pre[None],
              mlp_g_post[None])
    return _odd_layer_call(x, consts, late["mlp_w1"], late["mlp_w2"], layer)


def kernel(x, mix_pre_g, mix_post_g, mlp_pre_g, mlp_post_g, w_in_even, s5_lam_re, s5_lam_im, s5_log_dt, s5_b_re, s5_b_im, s5_c_re, s5_c_im, s5_d, s5_w_glu, fox_b_f, w_out_even, w_in_odd, pool_w, pool_scale, sgu_ln_g, sgu_ln_b, sgu_w_s, sgu_b_s, w_out_odd, mlp_w1, mlp_w2):
    depth = mix_pre_g.shape[0]
    late = dict(mlp_w1=mlp_w1, mlp_w2=mlp_w2, w_glu=s5_w_glu, w_out_even=w_out_even,
                w_in_odd=w_in_odd, pool_w=pool_w, w_out_odd=w_out_odd)
    for l in range(depth):
        i = l // 2
        if l % 2 == 0:
            x, late = _even_layer(
                x, mix_pre_g[l], mix_post_g[l], w_in_even[i], s5_lam_re[i], s5_lam_im[i],
                s5_log_dt[i], s5_b_re[i], s5_b_im[i], s5_c_re[i], s5_c_im[i], s5_d[i],
                fox_b_f[i], mlp_pre_g[l], mlp_post_g[l], late, i, l)
        else:
            x = _odd_layer(x, mix_pre_g[l], mix_post_g[l], pool_scale[i], sgu_ln_g[i], sgu_ln_b[i],
                           sgu_w_s[i], sgu_b_s[i], mlp_pre_g[l], mlp_post_g[l], late, i, l)
    return x
```

```python
import numpy as np
import jax
import jax.numpy as jnp
from jax import lax
from jax.experimental import pallas as pl
from jax.experimental.pallas import tpu as pltpu

F32 = jnp.float32
BF16 = jnp.bfloat16

EPS = 1e-6
LOG2E = 1.4426950408889634
NEG_BIG = -1e30

LANES = 128
SUBLANES = 8
VMEM_LIMIT = 56 * 1024 * 1024

HEAD_DIM = 64
POOL_WINDOWS = (2, 4, 8, 16)
CHUNK = 128

ROW_TILE = 512
POST_TILE = 256
MLP_FF_CHUNK = 1024
MLP_GROUP = 2
S5_TILE = 256
S5_SUBTILES = 2
ATT_TQ = 1024
ATT_TK = 512
ATT_PAIRS = 2


def _params(*sem):
    return pltpu.CompilerParams(dimension_semantics=sem, vmem_limit_bytes=VMEM_LIMIT)


def _const_spec(shape):
    nd = len(shape)
    return pl.BlockSpec(shape, lambda *_: (0,) * nd, pipeline_mode=pl.Buffered(1))


def _layer_spec(shape, layer):
    nd = len(shape)
    return pl.BlockSpec((None,) + tuple(shape[1:]), lambda *_: (layer,) + (0,) * (nd - 1),
                        pipeline_mode=pl.Buffered(1))


def _rms(x, g):
    return x * lax.rsqrt(jnp.mean(x * x, axis=-1, keepdims=True) + EPS) * g


def _split3(x):
    hi = x.astype(BF16)
    r1 = x - hi.astype(F32)
    mid = r1.astype(BF16)
    lo = (r1 - mid.astype(F32)).astype(BF16)
    return hi, mid, lo


def _dot(a, b):
    return jnp.dot(a, b, preferred_element_type=F32)


def _even_inproj_kernel(x_ref, g_ref, w_ref, wf_ref, bf_ref, tri_ref, place_ref,
                        u0_ref, u1_ref, q_ref, k_ref, v_ref, feat_ref, carry_ref):
    nb, tm, _ = x_ref.shape
    s5w = 4 * LANES
    fw = q_ref.shape[2]
    q_scale = HEAD_DIM ** -0.5 * LOG2E

    @pl.when(pl.program_id(0) == 0)
    def _():
        carry_ref[...] = jnp.zeros_like(carry_ref)

    def running_sum(b, z):
        logf = jnp.minimum(z, 0.0) - jnp.log1p(jnp.exp(-jnp.abs(z)))
        c3 = _dot(tri_ref[...], jnp.concatenate(_split3(logf), axis=1))
        csum = (c3[:, 0:LANES] + c3[:, LANES:2 * LANES] + c3[:, 2 * LANES:3 * LANES]
                + carry_ref[b:b + 1, :])
        carry_ref[b:b + 1, :] = csum[tm - 1:tm, :]
        return csum

    def place(b, csum):
        pieces = jnp.concatenate(_split3(csum * LOG2E), axis=1)
        feat_ref[b] = _dot(pieces, place_ref[...]).astype(BF16)

    z_prev = None
    for b in range(nb):
        h = _rms(x_ref[b], g_ref[...]).astype(BF16)
        z = _dot(h, wf_ref[...]) + bf_ref[...]
        zu = _dot(h, w_ref[:, 0:s5w])
        for half in range(2):
            rows = pl.ds(2 * b + half, tm, stride=2 * nb)
            u0_ref[rows, :] = zu[:, 2 * half * LANES:(2 * half + 1) * LANES]
            u1_ref[rows, :] = zu[:, (2 * half + 1) * LANES:(2 * half + 2) * LANES]
        if z_prev is not None:
            csum = running_sum(b - 1, z_prev)
        q_ref[b] = (_dot(h, w_ref[:, s5w:s5w + fw]) * q_scale).astype(BF16)
        k_ref[b] = _dot(h, w_ref[:, s5w + fw:s5w + 2 * fw]).astype(BF16)
        if z_prev is not None:
            place(b - 1, csum)
        v_ref[b] = _dot(h, w_ref[:, s5w + 2 * fw:s5w + 3 * fw]).astype(BF16)
        z_prev = z
    place(nb - 1, running_sum(nb - 1, z_prev))


def _even_inproj(x, g, w, wf, b_f):
    B, L, D = x.shape
    n_heads = b_f.shape[0]
    tm = min(ROW_TILE, L)
    tok = lambda r: (0, r, 0)
    ublk = pl.BlockSpec((tm * 2 * B, LANES), lambda r: (r, 0))
    aspec = pl.BlockSpec((B, tm, 512), tok)
    place, ones = _fox_placement(n_heads)
    tri = jnp.asarray(np.tril(np.ones((tm, tm), np.float32)), BF16)
    bf = jnp.zeros((1, LANES), F32).at[0, :n_heads].set(b_f)
    consts = (g, w, wf, bf, tri, jnp.asarray(place, BF16))
    out = pl.pallas_call(
        _even_inproj_kernel,
        grid=(L // tm,),
        in_specs=[pl.BlockSpec((B, tm, D), tok)] + [_const_spec(c.shape) for c in consts],
        out_specs=[ublk, ublk, aspec, aspec, aspec, pl.BlockSpec((B, tm, place.shape[1]), tok)],
        out_shape=[
            jax.ShapeDtypeStruct((L * 2 * B, LANES), F32),
            jax.ShapeDtypeStruct((L * 2 * B, LANES), F32),
            jax.ShapeDtypeStruct((B, L, 512), BF16),
            jax.ShapeDtypeStruct((B, L, 512), BF16),
            jax.ShapeDtypeStruct((B, L, 512), BF16),
            jax.ShapeDtypeStruct((B, L, place.shape[1]), BF16),
        ],
        scratch_shapes=[pltpu.VMEM((SUBLANES, LANES), F32)],
        compiler_params=_params("arbitrary"),
        name="even_inproj",
    )(x, *consts)
    return (*out, jnp.asarray(ones))


def _fox_placement(n_heads):
    width = (n_heads // 2) * LANES
    place = np.zeros((3 * LANES, width), np.float32)
    ones = np.zeros((2, LANES), np.float32)
    for h in range(n_heads):
        base = HEAD_DIM if h % 2 == 0 else 0
        for j in range(3):
            place[j * LANES + h, (h // 2) * LANES + base + j] = 1.0
            place[j * LANES + h, (h // 2) * LANES + base + 3 + j] = -1.0
            ones[0, base + 3 + j] = 1.0
            ones[1, base + j] = 1.0
    return place, ones


def _fox_attn_kernel(q_ref, k_ref, v_ref, fq_ref, fk_ref, ones_ref, *refs):
    n_riders = (len(refs) - 6) // 2
    o_ref = refs[n_riders]
    kaug_ref, vaug_ref, qaug_ref, m_ref, acc_ref = refs[2 * n_riders + 1:]
    for src, dst in zip(refs[:n_riders], refs[n_riders + 1:2 * n_riders + 1]):
        dst[...] = src[...].astype(BF16)

    qi = pl.program_id(2)
    tq = q_ref.shape[1]
    L = k_ref.shape[1]
    tk = min(ATT_TK, L)
    n_heads = kaug_ref.shape[0]
    assert tq % tk == 0

    def merge(lanes, first, x_ref, f):
        x = x_ref[0, :, first * LANES:(first + 1) * LANES].astype(F32)
        return (jnp.where(lanes < HEAD_DIM, x, f(0)).astype(BF16),
                jnp.where(lanes >= HEAD_DIM, x, f(1)).astype(BF16))

    def features(f_ref, pr, side):
        one = ones_ref[side:side + 1, :]
        return f_ref[0, :, pr * LANES:(pr + 1) * LANES].astype(F32) * (1.0 - one) + one

    @pl.when(qi == 0)
    def _():
        lanes = lax.broadcasted_iota(jnp.int32, (L, LANES), 1)
        for pr in range(n_heads // 2):
            fk = features(fk_ref, pr, 1)
            kaug_ref[2 * pr], kaug_ref[2 * pr + 1] = merge(lanes, pr, k_ref, lambda e: fk)
            ones = ((lanes == HEAD_DIM).astype(F32), (lanes == 0).astype(F32))
            vaug_ref[2 * pr], vaug_ref[2 * pr + 1] = merge(lanes, pr, v_ref, lambda e: ones[e])

    lanes_q = lax.broadcasted_iota(jnp.int32, (tq, LANES), 1)
    for pr in range(n_heads // 2):
        fq = features(fq_ref, pr, 0)
        qaug_ref[2 * pr], qaug_ref[2 * pr + 1] = merge(lanes_q, pr, q_ref, lambda e: fq)

    def tile(j, r0, r1, masked, first=False):
        start = pl.multiple_of(j * tk, tk)
        for e in range(n_heads):
            kt = kaug_ref[e, pl.ds(start, tk), :]
            s = lax.dot_general(qaug_ref[e, r0:r1, :], kt, (((1,), (1,)), ((), ())),
                                preferred_element_type=F32)
            if masked:
                row = lax.broadcasted_iota(jnp.int32, s.shape, 0)
                col = lax.broadcasted_iota(jnp.int32, s.shape, 1)
                s = jnp.where(col <= row, s, NEG_BIG)
            m_new = jnp.broadcast_to(jnp.max(s, axis=1, keepdims=True), (r1 - r0, LANES))
            if not first:
                m = m_ref[e, r0:r1, :]
                m_new = jnp.maximum(m, m_new)
            p = jnp.exp2((s - jnp.concatenate([m_new] * (tk // LANES), axis=1)).astype(BF16))
            pv = _dot(p, vaug_ref[e, pl.ds(start, tk), :])
            acc_ref[e, r0:r1, :] = pv if first else jnp.exp2(m - m_new) * acc_ref[e, r0:r1, :] + pv
            m_ref[e, r0:r1, :] = m_new

    n_diag = tq // tk

    tile(qi * n_diag, 0, tk, True, first=True)
    if tk < tq:
        tile(qi * n_diag, tk, tq, False, first=True)

    def body(j, c):
        for d in range(n_diag):
            tile(j * n_diag + d, 0, tq, False)
        return c

    lax.fori_loop(0, qi, body, 0)
    for d in range(1, n_diag):
        tile(qi * n_diag + d, d * tk, (d + 1) * tk, True)
        if (d + 1) * tk < tq:
            tile(qi * n_diag + d, (d + 1) * tk, tq, False)

    for pr in range(n_heads // 2):
        acc0, acc1 = acc_ref[2 * pr], acc_ref[2 * pr + 1]
        out0 = acc0 / acc0[:, HEAD_DIM:HEAD_DIM + 1]
        out1 = acc1 / acc1[:, 0:1]
        o_ref[0, :, pr * LANES:(pr + 1) * LANES] = jnp.where(lanes_q < HEAD_DIM, out0, out1).astype(BF16)


def _fox_attn(q, k, v, feat, ones, riders):
    B, L, W = q.shape
    bw = ATT_PAIRS * LANES
    n_heads = 2 * ATT_PAIRS
    tq = min(ATT_TQ, L)
    grid = (B, W // bw, L // tq)
    n_steps = grid[0] * grid[1] * grid[2]
    qspec = pl.BlockSpec((1, tq, bw), lambda b, p, i: (b, i, p))
    kspec = pl.BlockSpec((1, L, bw), lambda b, p, i: (b, 0, p))
    step = lambda b, p, i: ((b * grid[1] + p) * grid[2] + i, 0)
    rspecs = [pl.BlockSpec((r.shape[0] // n_steps, r.shape[1]), step) for r in riders]
    out = pl.pallas_call(
        _fox_attn_kernel,
        grid=grid,
        in_specs=[qspec, kspec, kspec, qspec, kspec, _const_spec(ones.shape)] + rspecs,
        out_specs=[qspec] + rspecs,
        out_shape=[jax.ShapeDtypeStruct((B, L, W), BF16)]
                  + [jax.ShapeDtypeStruct(r.shape, BF16) for r in riders],
        scratch_shapes=[pltpu.VMEM((n_heads, L, LANES), BF16), pltpu.VMEM((n_heads, L, LANES), BF16),
                        pltpu.VMEM((n_heads, tq, LANES), BF16), pltpu.VMEM((n_heads, tq, LANES), F32),
                        pltpu.VMEM((n_heads, tq, LANES), F32)],
        compiler_params=_params("parallel", "parallel", "arbitrary"),
        name="fox_attn",
    )(q, k, v, feat, feat, ones, *riders)
    return out[0], out[1:]


def _s5_kernel(u0_ref, u1_ref, bst_ref, cst_ref, are_ref, aim_ref, d_ref, o0_ref, o1_ref,
               st_ref, *x_refs):
    rows = u0_ref.shape[0]
    nb = st_ref.shape[0] // 2
    sub = rows // len(x_refs)
    hrows = sub // 2

    @pl.when(pl.program_id(0) == 0)
    def _():
        st_ref[...] = jnp.zeros_like(st_ref)

    def half_rows(ref, base, h, *lead):
        return ref[(*lead, pl.ds(base + h, hrows, stride=2), slice(None))]

    def project(k):
        u_h = []
        for h in range(2):
            u = jnp.concatenate([half_rows(u0_ref, k * sub, h), half_rows(u1_ref, k * sub, h)],
                                axis=1)
            u_h.append(u)
            bu = _dot(u.astype(BF16), bst_ref[h])
            for c in range(2 * nb):
                x_refs[k][c, pl.ds(h, hrows, stride=2), :] = bu[:, c * LANES:(c + 1) * LANES]
        return u_h

    tile_rows = lambda a: jnp.concatenate([a] * (SUBLANES // 2), axis=0)
    a_re = [tile_rows(are_ref[:, c * LANES:(c + 1) * LANES]) for c in range(nb)]
    a_im = [tile_rows(aim_ref[:, c * LANES:(c + 1) * LANES]) for c in range(nb)]

    def scan(k, state):
        state = list(state)
        x_ref = x_refs[k]
        for i in range(sub // SUBLANES):
            r = pl.ds(i * SUBLANES, SUBLANES)
            for c in range(nb):
                s_re, s_im = state[2 * c], state[2 * c + 1]
                n_re = a_re[c] * s_re - a_im[c] * s_im + x_ref[c, r, :]
                n_im = a_re[c] * s_im + a_im[c] * s_re + x_ref[nb + c, r, :]
                x_ref[c, r, :] = n_re
                x_ref[nb + c, r, :] = n_im
                state[2 * c], state[2 * c + 1] = n_re, n_im
        return state

    def emit(k, u_h):
        for h in range(2):
            xs = jnp.concatenate([half_rows(x_refs[k], 0, h, c) for c in range(2 * nb)], axis=1)
            y = _dot(xs.astype(BF16), cst_ref[h])
            g = jax.nn.gelu(y + d_ref[h:h + 1, :] * u_h[h])
            o0_ref[pl.ds(k * sub + h, hrows, stride=2), :] = g[:, 0:LANES]
            o1_ref[pl.ds(k * sub + h, hrows, stride=2), :] = g[:, LANES:2 * LANES]

    state = [st_ref[c] for c in range(2 * nb)]
    u_next = project(0)
    for k in range(len(x_refs)):
        u_k = u_next
        if k + 1 < len(x_refs):
            u_next = project(k + 1)
        state = scan(k, state)
        emit(k, u_k)
    for c in range(2 * nb):
        st_ref[c] = state[c]


def _s5_zoh_kernel(lre_ref, lim_ref, ldt_ref, bre_ref, bim_ref, are_ref, aim_ref, bbre_ref, bbim_ref):
    lr, li = lre_ref[...], lim_ref[...]
    dt = jnp.exp(ldt_ref[...])
    mag = jnp.exp(lr * dt)
    ab_re = mag * jnp.cos(li * dt)
    ab_im = mag * jnp.sin(li * dt)
    den = lr * lr + li * li
    nr = ab_re - 1.0
    ni = ab_im
    q_re = (nr * lr + ni * li) / den
    q_im = (ni * lr - nr * li) / den
    b_re, b_im = bre_ref[...], bim_ref[...]
    q_re, q_im = q_re[:, None, :], q_im[:, None, :]
    are_ref[...] = ab_re
    aim_ref[...] = ab_im
    bbre_ref[...] = q_re * b_re - q_im * b_im
    bbim_ref[...] = q_re * b_im + q_im * b_re


def _s5_discretize(lam_re, lam_im, log_dt, b_re, b_im):
    G, P, H = b_re.shape
    ab_re, ab_im, bbt_re, bbt_im = pl.pallas_call(
        _s5_zoh_kernel,
        out_shape=[jax.ShapeDtypeStruct((G, P), F32)] * 2 + [jax.ShapeDtypeStruct((G, H, P), F32)] * 2,
        name="s5_zoh",
    )(lam_re, lam_im, log_dt[:, None], jnp.transpose(b_re, (0, 2, 1)), jnp.transpose(b_im, (0, 2, 1)))
    return ab_re, ab_im, bbt_re.reshape(G * H, P), bbt_im.reshape(G * H, P)


def _s5_matrices(ab_re, ab_im, bbt_re, bbt_im, c_re, c_im, d):
    G, H, P = c_re.shape
    gh = G // 2
    hw, ns = gh * H, gh * P
    bmask = np.kron(np.eye(gh, dtype=np.float32), np.ones((H, P), np.float32))

    def bmat(bbt):
        return jnp.tile(bbt.reshape(2, hw, P), (1, 1, gh)) * bmask

    def cmat(cc):
        t = jnp.transpose(cc.reshape(2, gh, H, P), (0, 1, 3, 2)).reshape(2, ns, H)
        return jnp.tile(t, (1, 1, gh)) * bmask.T

    bst = jnp.concatenate([bmat(bbt_re), bmat(bbt_im)], axis=2).astype(BF16)
    cst = jnp.concatenate([cmat(c_re), -cmat(c_im)], axis=1).astype(BF16)
    a_re, a_im = ab_re.reshape(2, ns), ab_im.reshape(2, ns)
    return bst, cst, a_re, a_im, d.reshape(2, hw)


def _s5_scan(u0, u1, bst, cst, a_re, a_im, d2):
    n_rows = u0.shape[0]
    ns = a_re.shape[1]
    rows = min(S5_TILE * SUBLANES, n_rows)
    blk = pl.BlockSpec((rows, LANES), lambda t: (t, 0))
    return pl.pallas_call(
        _s5_kernel,
        grid=(n_rows // rows,),
        in_specs=[blk, blk, _const_spec(bst.shape), _const_spec(cst.shape),
                  _const_spec(a_re.shape), _const_spec(a_im.shape), _const_spec(d2.shape)],
        out_specs=[blk, blk],
        out_shape=[jax.ShapeDtypeStruct((n_rows, LANES), F32)] * 2,
        scratch_shapes=[pltpu.VMEM((2 * ns // LANES, SUBLANES, LANES), F32)]
                       + [pltpu.VMEM((2 * ns // LANES, rows // S5_SUBTILES, LANES), F32)] * S5_SUBTILES,
        compiler_params=_params("arbitrary"),
        name="s5_scan",
    )(u0, u1, bst, cst, a_re, a_im, d2)


def _mlp_rows(x, gpre_ref, gpost_ref, w1_ref, w2_ref):
    h = _rms(x, gpre_ref[...]).astype(BF16)
    d_ff = w1_ref.shape[1]
    acc = jnp.zeros(x.shape, F32)
    for f in range(0, d_ff, MLP_FF_CHUNK):
        a = jnp.maximum(_dot(h, w1_ref[:, f:f + MLP_FF_CHUNK]), 0.0)
        acc = acc + _dot((a * a).astype(BF16), w2_ref[f:f + MLP_FF_CHUNK, :])
    return x + _rms(acc, gpost_ref[...])


def _even_post_kernel(x_ref, g0_ref, g1_ref, yb_ref, wglu_ref, wout_ref, g_ref,
                      gpre_ref, gpost_ref, w1_ref, w2_ref, o_ref):
    nb, tm, _ = x_ref.shape
    half_w = wout_ref.shape[0] // 2

    def mixer_tail(b):
        blocks = [ref[pl.ds(2 * b + half, tm, stride=2 * nb), :]
                  for half in range(2) for ref in (g0_ref, g1_ref)]
        ga = jnp.concatenate(blocks, axis=1)
        ya = ga * jax.nn.sigmoid(_dot(ga.astype(BF16), wglu_ref[...]))
        mix = (_dot(ya.astype(BF16), wout_ref[0:half_w, :])
               + _dot(yb_ref[b], wout_ref[half_w:2 * half_w, :]))
        return x_ref[b] + _rms(mix, g_ref[...])

    x1 = jnp.concatenate([mixer_tail(b) for b in range(nb)], axis=0)
    out = _mlp_rows(x1, gpre_ref, gpost_ref, w1_ref, w2_ref)
    for b in range(nb):
        o_ref[b] = out[b * tm:(b + 1) * tm]


def _even_post(x, g0, g1, yb, w_glu, w_out, g, g_pre, g_post, w1s, w2s, layer):
    B, L, D = x.shape
    tm = min(POST_TILE, L)
    tok = lambda r: (0, r, 0)
    gblk = pl.BlockSpec((tm * 2 * B, LANES), lambda r: (r, 0))
    consts = (w_glu, w_out, g, g_pre, g_post)
    return pl.pallas_call(
        _even_post_kernel,
        grid=(L // tm,),
        in_specs=[pl.BlockSpec((B, tm, D), tok), gblk, gblk, pl.BlockSpec((B, tm, 512), tok)]
                 + [_const_spec(c.shape) for c in consts]
                 + [_layer_spec(w1s.shape, layer), _layer_spec(w2s.shape, layer)],
        out_specs=pl.BlockSpec((B, tm, D), tok),
        out_shape=jax.ShapeDtypeStruct((B, L, D), F32),
        compiler_params=_params("parallel"),
        name="even_post_mlp",
    )(x, g0, g1, yb, *consts, w1s, w2s)


def _odd_inproj_rows(x, gpre_ref, win_ref):
    width = win_ref.shape[1] // 3
    h = _rms(x, gpre_ref[...]).astype(BF16)
    return tuple(_dot(h, win_ref[:, n * width:(n + 1) * width]) for n in range(3))


def _odd_mix_rows(x, xc, u, v, t0, pool_ref, pw_ref, ps_ref, lng_ref, lnb_ref, ws, bs,
                  wout_ref, gpost_ref):
    tm = x.shape[0]
    pad = max(POOL_WINDOWS)
    gd = LANES
    n_grp = len(POOL_WINDOWS)
    width = n_grp * gd

    pool_ref[pad:pad + tm, :] = xc
    t = t0 + lax.broadcasted_iota(jnp.int32, (tm, 1), 0)
    yc = []
    for g, w in enumerate(POOL_WINDOWS):
        cols = slice(g * gd, (g + 1) * gd)
        tot = xc[:, cols]
        for lag in range(1, w):
            tot = tot + pool_ref[pl.ds(pad - lag, tm), cols]
        cnt = jnp.minimum(t + 1, w).astype(F32)
        pooled = tot / cnt - xc[:, cols]
        yc.append(_dot(pooled.astype(BF16), pw_ref[g]))
    pool_ref[0:pad, :] = pool_ref[tm:tm + pad, :]
    y_c = jnp.concatenate(yc, axis=-1) * ps_ref[...]

    gv = jax.nn.gelu(v)
    mu = jnp.mean(gv, axis=-1, keepdims=True)
    vc = gv - mu
    vn = vc * lax.rsqrt(jnp.mean(vc * vc, axis=-1, keepdims=True) + EPS)
    vn = (vn * lng_ref[...] + lnb_ref[...]).astype(BF16)
    chunks = []
    for n in range(tm // CHUNK):
        parts = [_dot(ws[g], vn[n * CHUNK:(n + 1) * CHUNK, g * gd:(g + 1) * gd]) for g in range(n_grp)]
        chunks.append(jnp.concatenate(parts, axis=-1) + bs)
    y_d = jax.nn.gelu(u) * jnp.concatenate(chunks, axis=0)

    mix = _dot(y_c.astype(BF16), wout_ref[0:width, :]) + _dot(y_d.astype(BF16), wout_ref[width:2 * width, :])
    return x + _rms(mix, gpost_ref[...])


def _odd_layer_kernel(x_ref, gpre_ref, win_ref, pw_ref, ps_ref, lng_ref, lnb_ref, ws_ref,
                      bs_ref, wout_ref, gpost_ref, mgpre_ref, mgpost_ref, w1_ref, w2_ref,
                      o_ref, pool_ref):
    r = pl.program_id(0)
    nb, tm, _ = x_ref.shape
    pad = max(POOL_WINDOWS)

    @pl.when(r == 0)
    def _():
        pool_ref[:, 0:pad, :] = jnp.zeros((nb, pad, pool_ref.shape[2]), F32)

    rows = lax.broadcasted_iota(jnp.int32, (CHUNK, CHUNK), 0)
    cols = lax.broadcasted_iota(jnp.int32, (CHUNK, CHUNK), 1)
    ws = [jnp.where(cols <= rows, ws_ref[g], 0.0).astype(BF16) for g in range(ws_ref.shape[0])]
    bs = jnp.concatenate([jnp.broadcast_to(bs_ref[:, g:g + 1], (CHUNK, LANES))
                          for g in range(bs_ref.shape[1])], axis=1)
    proj = {0: _odd_inproj_rows(x_ref[0], gpre_ref, win_ref)}
    x1 = []
    for b in range(nb):
        if b + 1 < nb:
            proj[b + 1] = _odd_inproj_rows(x_ref[b + 1], gpre_ref, win_ref)
        if len(x1) == MLP_GROUP:
            out = _mlp_rows(jnp.concatenate(x1, axis=0), mgpre_ref, mgpost_ref, w1_ref, w2_ref)
            for n in range(MLP_GROUP):
                o_ref[b - MLP_GROUP + n] = out[n * tm:(n + 1) * tm]
            x1 = []
        x1.append(_odd_mix_rows(x_ref[b], *proj.pop(b), r * tm, pool_ref.at[b], pw_ref, ps_ref,
                                lng_ref, lnb_ref, ws, bs, wout_ref, gpost_ref))
    out = _mlp_rows(jnp.concatenate(x1, axis=0), mgpre_ref, mgpost_ref, w1_ref, w2_ref)
    for n in range(len(x1)):
        o_ref[nb - len(x1) + n] = out[n * tm:(n + 1) * tm]


def _odd_layer_call(x, consts, w1s, w2s, layer):
    B, L, D = x.shape
    tm = min(POST_TILE, L)
    width = consts[3].shape[1]
    tok = lambda r: (0, r, 0)
    return pl.pallas_call(
        _odd_layer_kernel,
        grid=(L // tm,),
        in_specs=[pl.BlockSpec((B, tm, D), tok)] + [_const_spec(c.shape) for c in consts]
                 + [_layer_spec(w1s.shape, layer), _layer_spec(w2s.shape, layer)],
        out_specs=pl.BlockSpec((B, tm, D), tok),
        out_shape=jax.ShapeDtypeStruct((B, L, D), F32),
        scratch_shapes=[pltpu.VMEM((B, tm + max(POOL_WINDOWS), width), F32)],
        compiler_params=_params("arbitrary"),
        name="odd_layer",
    )(x, *consts, w1s, w2s)


def _even_layer(x, g_pre, g_post, w_in, lam_re, lam_im, log_dt, b_re, b_im, c_re, c_im, d,
                b_f, mlp_g_pre, mlp_g_post, late, i, layer):
    B, L, D = x.shape
    n_heads = b_f.shape[0]
    s5w = d.shape[0]
    fw = n_heads * HEAD_DIM
    assert s5w == 4 * LANES and w_in.shape[1] == s5w + 3 * fw + n_heads
    wf = jnp.pad(w_in[:, s5w + 3 * fw:], ((0, 0), (0, LANES - n_heads))).astype(BF16)
    assert 2 * B == SUBLANES
    u0, u1, q, k, v, feat, ones = _even_inproj(x, g_pre[None], w_in.astype(BF16), wf, b_f)

    todo = [n for n, w in late.items() if w.dtype != BF16]
    y_b, cast = _fox_attn(q, k, v, feat, ones,
                          tuple(late[n].reshape(-1, late[n].shape[-1]) for n in todo))
    late = {**late, **{n: c.reshape(late[n].shape) for n, c in zip(todo, cast)}}

    ab_re, ab_im, bbt_re, bbt_im = _s5_discretize(lam_re, lam_im, log_dt, b_re, b_im)
    bst, cst, a_re, a_im, d2 = _s5_matrices(ab_re, ab_im, bbt_re, bbt_im, c_re, c_im, d)
    g0, g1 = _s5_scan(u0, u1, bst, cst, a_re, a_im, d2)

    x = _even_post(x, g0, g1, y_b, late["w_glu"][i], late["w_out_even"][i], g_post[None],
                   mlp_g_pre[None], mlp_g_post[None], late["mlp_w1"], late["mlp_w2"], layer)
    return x, late


def _odd_layer(x, g_pre, g_post, pool_scale, ln_g, ln_b, w_s, b_s, mlp_g_pre, mlp_g_post,
               late, i, layer):
    consts = (g_pre[None], late["w_in_odd"][i], late["pool_w"][i], pool_scale[None], ln_g[None],
              ln_b[None], w_s, jnp.transpose(b_s), late["w_out_odd"][i], g_post[None], mlp_g_pre[None],
              mlp_g_post[None])
    return _odd_layer_call(x, consts, late["mlp_w1"], late["mlp_w2"], layer)


def kernel(x, mix_pre_g, mix_post_g, mlp_pre_g, mlp_post_g, w_in_even, s5_lam_re, s5_lam_im, s5_log_dt, s5_b_re, s5_b_im, s5_c_re, s5_c_im, s5_d, s5_w_glu, fox_b_f, w_out_even, w_in_odd, pool_w, pool_scale, sgu_ln_g, sgu_ln_b, sgu_w_s, sgu_b_s, w_out_odd, mlp_w1, mlp_w2):
    depth = mix_pre_g.shape[0]
    late = dict(mlp_w1=mlp_w1, mlp_w2=mlp_w2, w_glu=s5_w_glu, w_out_even=w_out_even,
                w_in_odd=w_in_odd, pool_w=pool_w, w_out_odd=w_out_odd)
    for l in range(depth):
        i = l // 2
        if l % 2 == 0:
            x, late = _even_layer(
                x, mix_pre_g[l], mix_post_g[l], w_in_even[i], s5_lam_re[i], s5_lam_im[i],
                s5_log_dt[i], s5_b_re[i], s5_b_im[i], s5_c_re[i], s5_c_im[i], s5_d[i],
                fox_b_f[i], mlp_pre_g[l], mlp_post_g[l], late, i, l)
        else:
            x = _odd_layer(x, mix_pre_g[l], mix_post_g[l], pool_scale[i], sgu_ln_g[i], sgu_ln_b[i],
                           sgu_w_s[i], sgu_b_s[i], mlp_pre_g[l], mlp_post_g[l], late, i, l)
    return x
```

```python
import numpy as np
import jax
import jax.numpy as jnp
from jax import lax
from jax.experimental import pallas as pl
from jax.experimental.pallas import tpu as pltpu

F32 = jnp.float32
BF16 = jnp.bfloat16

EPS = 1e-6
LOG2E = 1.4426950408889634
NEG_BIG = -1e30

LANES = 128
SUBLANES = 8
VMEM_LIMIT = 56 * 1024 * 1024

HEAD_DIM = 64
POOL_WINDOWS = (2, 4, 8, 16)
CHUNK = 128

ROW_TILE = 512
POST_TILE = 256
MLP_FF_CHUNK = 1024
MLP_GROUP = 2
S5_TILE = 256
S5_SUBTILES = 2
ATT_TQ = 1024
ATT_TK = 512
ATT_PAIRS = 2


def _params(*sem):
    return pltpu.CompilerParams(dimension_semantics=sem, vmem_limit_bytes=VMEM_LIMIT)


def _const_spec(shape):
    nd = len(shape)
    return pl.BlockSpec(shape, lambda *_: (0,) * nd, pipeline_mode=pl.Buffered(1))


def _layer_spec(shape, layer):
    nd = len(shape)
    return pl.BlockSpec((None,) + tuple(shape[1:]), lambda *_: (layer,) + (0,) * (nd - 1),
                        pipeline_mode=pl.Buffered(1))


def _rms(x, g):
    return x * lax.rsqrt(jnp.mean(x * x, axis=-1, keepdims=True) + EPS) * g


def _split3(x):
    hi = x.astype(BF16)
    r1 = x - hi.astype(F32)
    mid = r1.astype(BF16)
    lo = (r1 - mid.astype(F32)).astype(BF16)
    return hi, mid, lo


def _dot(a, b):
    return jnp.dot(a, b, preferred_element_type=F32)


def _even_inproj_kernel(x_ref, g_ref, w_ref, wf_ref, bf_ref, tri_ref, place_ref,
                        u0_ref, u1_ref, q_ref, k_ref, v_ref, feat_ref, carry_ref):
    nb, tm, _ = x_ref.shape
    s5w = 4 * LANES
    fw = q_ref.shape[2]
    q_scale = HEAD_DIM ** -0.5 * LOG2E

    @pl.when(pl.program_id(0) == 0)
    def _():
        carry_ref[...] = jnp.zeros_like(carry_ref)

    def running_sum(b, z):
        logf = jnp.minimum(z, 0.0) - jnp.log1p(jnp.exp(-jnp.abs(z)))
        c3 = _dot(tri_ref[...], jnp.concatenate(_split3(logf), axis=1))
        csum = (c3[:, 0:LANES] + c3[:, LANES:2 * LANES] + c3[:, 2 * LANES:3 * LANES]
                + carry_ref[b:b + 1, :])
        carry_ref[b:b + 1, :] = csum[tm - 1:tm, :]
        return csum

    def place(b, csum):
        pieces = jnp.concatenate(_split3(csum * LOG2E), axis=1)
        feat_ref[b] = _dot(pieces, place_ref[...]).astype(BF16)

    z_prev = None
    for b in range(nb):
        h = _rms(x_ref[b], g_ref[...]).astype(BF16)
        z = _dot(h, wf_ref[...]) + bf_ref[...]
        zu = _dot(h, w_ref[:, 0:s5w])
        for half in range(2):
            rows = pl.ds(2 * b + half, tm, stride=2 * nb)
            u0_ref[rows, :] = zu[:, 2 * half * LANES:(2 * half + 1) * LANES]
            u1_ref[rows, :] = zu[:, (2 * half + 1) * LANES:(2 * half + 2) * LANES]
        if z_prev is not None:
            csum = running_sum(b - 1, z_prev)
        q_ref[b] = (_dot(h, w_ref[:, s5w:s5w + fw]) * q_scale).astype(BF16)
        k_ref[b] = _dot(h, w_ref[:, s5w + fw:s5w + 2 * fw]).astype(BF16)
        if z_prev is not None:
            place(b - 1, csum)
        v_ref[b] = _dot(h, w_ref[:, s5w + 2 * fw:s5w + 3 * fw]).astype(BF16)
        z_prev = z
    place(nb - 1, running_sum(nb - 1, z_prev))


def _even_inproj(x, g, w, wf, b_f):
    B, L, D = x.shape
    n_heads = b_f.shape[0]
    tm = min(ROW_TILE, L)
    tok = lambda r: (0, r, 0)
    ublk = pl.BlockSpec((tm * 2 * B, LANES), lambda r: (r, 0))
    aspec = pl.BlockSpec((B, tm, 512), tok)
    place, ones = _fox_placement(n_heads)
    tri = jnp.asarray(np.tril(np.ones((tm, tm), np.float32)), BF16)
    bf = jnp.zeros((1, LANES), F32).at[0, :n_heads].set(b_f)
    consts = (g, w, wf, bf, tri, jnp.asarray(place, BF16))
    out = pl.pallas_call(
        _even_inproj_kernel,
        grid=(L // tm,),
        in_specs=[pl.BlockSpec((B, tm, D), tok)] + [_const_spec(c.shape) for c in consts],
        out_specs=[ublk, ublk, aspec, aspec, aspec, pl.BlockSpec((B, tm, place.shape[1]), tok)],
        out_shape=[
            jax.ShapeDtypeStruct((L * 2 * B, LANES), F32),
            jax.ShapeDtypeStruct((L * 2 * B, LANES), F32),
            jax.ShapeDtypeStruct((B, L, 512), BF16),
            jax.ShapeDtypeStruct((B, L, 512), BF16),
            jax.ShapeDtypeStruct((B, L, 512), BF16),
            jax.ShapeDtypeStruct((B, L, place.shape[1]), BF16),
        ],
        scratch_shapes=[pltpu.VMEM((SUBLANES, LANES), F32)],
        compiler_params=_params("arbitrary"),
        name="even_inproj",
    )(x, *consts)
    return (*out, jnp.asarray(ones))


def _fox_placement(n_heads):
    width = (n_heads // 2) * LANES
    place = np.zeros((3 * LANES, width), np.float32)
    ones = np.zeros((2, LANES), np.float32)
    for h in range(n_heads):
        base = HEAD_DIM if h % 2 == 0 else 0
        for j in range(3):
            place[j * LANES + h, (h // 2) * LANES + base + j] = 1.0
            place[j * LANES + h, (h // 2) * LANES + base + 3 + j] = -1.0
            ones[0, base + 3 + j] = 1.0
            ones[1, base + j] = 1.0
    return place, ones


def _fox_attn_kernel(q_ref, k_ref, v_ref, fq_ref, fk_ref, ones_ref, *refs):
    n_riders = (len(refs) - 6) // 2
    o_ref = refs[n_riders]
    kaug_ref, vaug_ref, qaug_ref, m_ref, acc_ref = refs[2 * n_riders + 1:]
    for src, dst in zip(refs[:n_riders], refs[n_riders + 1:2 * n_riders + 1]):
        dst[...] = src[...].astype(BF16)

    qi = pl.program_id(2)
    tq = q_ref.shape[1]
    L = k_ref.shape[1]
    tk = min(ATT_TK, L)
    n_heads = kaug_ref.shape[0]
    assert tq % tk == 0

    def merge(lanes, first, x_ref, f):
        x = x_ref[0, :, first * LANES:(first + 1) * LANES].astype(F32)
        return (jnp.where(lanes < HEAD_DIM, x, f(0)).astype(BF16),
                jnp.where(lanes >= HEAD_DIM, x, f(1)).astype(BF16))

    def features(f_ref, pr, side):
        one = ones_ref[side:side + 1, :]
        return f_ref[0, :, pr * LANES:(pr + 1) * LANES].astype(F32) * (1.0 - one) + one

    @pl.when(qi == 0)
    def _():
        lanes = lax.broadcasted_iota(jnp.int32, (L, LANES), 1)
        for pr in range(n_heads // 2):
            fk = features(fk_ref, pr, 1)
            kaug_ref[2 * pr], kaug_ref[2 * pr + 1] = merge(lanes, pr, k_ref, lambda e: fk)
            ones = ((lanes == HEAD_DIM).astype(F32), (lanes == 0).astype(F32))
            vaug_ref[2 * pr], vaug_ref[2 * pr + 1] = merge(lanes, pr, v_ref, lambda e: ones[e])

    lanes_q = lax.broadcasted_iota(jnp.int32, (tq, LANES), 1)
    for pr in range(n_heads // 2):
        fq = features(fq_ref, pr, 0)
        qaug_ref[2 * pr], qaug_ref[2 * pr + 1] = merge(lanes_q, pr, q_ref, lambda e: fq)

    def tile(j, r0, r1, masked, first=False):
        start = pl.multiple_of(j * tk, tk)
        for e in range(n_heads):
            kt = kaug_ref[e, pl.ds(start, tk), :]
            s = lax.dot_general(qaug_ref[e, r0:r1, :], kt, (((1,), (1,)), ((), ())),
                                preferred_element_type=F32)
            if masked:
                row = lax.broadcasted_iota(jnp.int32, s.shape, 0)
                col = lax.broadcasted_iota(jnp.int32, s.shape, 1)
                s = jnp.where(col <= row, s, NEG_BIG)
            m_new = jnp.broadcast_to(jnp.max(s, axis=1, keepdims=True), (r1 - r0, LANES))
            if not first:
                m = m_ref[e, r0:r1, :]
                m_new = jnp.maximum(m, m_new)
            p = jnp.exp2((s - jnp.concatenate([m_new] * (tk // LANES), axis=1)).astype(BF16))
            pv = _dot(p, vaug_ref[e, pl.ds(start, tk), :])
            acc_ref[e, r0:r1, :] = pv if first else jnp.exp2(m - m_new) * acc_ref[e, r0:r1, :] + pv
            m_ref[e, r0:r1, :] = m_new

    n_diag = tq // tk

    tile(qi * n_diag, 0, tk, True, first=True)
    if tk < tq:
        tile(qi * n_diag, tk, tq, False, first=True)
    for d in range(1, n_diag):
        tile(qi * n_diag + d, d * tk, (d + 1) * tk, True)
        if (d + 1) * tk < tq:
            tile(qi * n_diag + d, (d + 1) * tk, tq, False)

    def body(j, c):
        for d in range(n_diag):
            tile(j * n_diag + d, 0, tq, False)
        return c

    lax.fori_loop(0, qi, body, 0)

    for pr in range(n_heads // 2):
        acc0, acc1 = acc_ref[2 * pr], acc_ref[2 * pr + 1]
        out0 = acc0 / acc0[:, HEAD_DIM:HEAD_DIM + 1]
        out1 = acc1 / acc1[:, 0:1]
        o_ref[0, :, pr * LANES:(pr + 1) * LANES] = jnp.where(lanes_q < HEAD_DIM, out0, out1).astype(BF16)


def _fox_attn(q, k, v, feat, ones, riders):
    B, L, W = q.shape
    bw = ATT_PAIRS * LANES
    n_heads = 2 * ATT_PAIRS
    tq = min(ATT_TQ, L)
    grid = (B, W // bw, L // tq)
    n_steps = grid[0] * grid[1] * grid[2]
    qspec = pl.BlockSpec((1, tq, bw), lambda b, p, i: (b, i, p))
    kspec = pl.BlockSpec((1, L, bw), lambda b, p, i: (b, 0, p))
    step = lambda b, p, i: ((b * grid[1] + p) * grid[2] + i, 0)
    rspecs = [pl.BlockSpec((r.shape[0] // n_steps, r.shape[1]), step) for r in riders]
    out = pl.pallas_call(
        _fox_attn_kernel,
        grid=grid,
        in_specs=[qspec, kspec, kspec, qspec, kspec, _const_spec(ones.shape)] + rspecs,
        out_specs=[qspec] + rspecs,
        out_shape=[jax.ShapeDtypeStruct((B, L, W), BF16)]
                  + [jax.ShapeDtypeStruct(r.shape, BF16) for r in riders],
        scratch_shapes=[pltpu.VMEM((n_heads, L, LANES), BF16), pltpu.VMEM((n_heads, L, LANES), BF16),
                        pltpu.VMEM((n_heads, tq, LANES), BF16), pltpu.VMEM((n_heads, tq, LANES), F32),
                        pltpu.VMEM((n_heads, tq, LANES), F32)],
        compiler_params=_params("parallel", "parallel", "arbitrary"),
        name="fox_attn",
    )(q, k, v, feat, feat, ones, *riders)
    return out[0], out[1:]


def _s5_kernel(u0_ref, u1_ref, bst_ref, cst_ref, are_ref, aim_ref, d_ref, o0_ref, o1_ref,
               st_ref, *x_refs):
    rows = u0_ref.shape[0]
    nb = st_ref.shape[0] // 2
    sub = rows // len(x_refs)
    hrows = sub // 2

    @pl.when(pl.program_id(0) == 0)
    def _():
        st_ref[...] = jnp.zeros_like(st_ref)

    def half_rows(ref, base, h, *lead):
        return ref[(*lead, pl.ds(base + h, hrows, stride=2), slice(None))]

    def project(k):
        u_h = []
        for h in range(2):
            u = jnp.concatenate([half_rows(u0_ref, k * sub, h), half_rows(u1_ref, k * sub, h)],
                                axis=1)
            u_h.append(u)
            bu = _dot(u.astype(BF16), bst_ref[h])
            for c in range(2 * nb):
                x_refs[k][c, pl.ds(h, hrows, stride=2), :] = bu[:, c * LANES:(c + 1) * LANES]
        return u_h

    a_re = [are_ref[:, c * LANES:(c + 1) * LANES] for c in range(nb)]
    a_im = [aim_ref[:, c * LANES:(c + 1) * LANES] for c in range(nb)]

    def scan(k, state):
        state = list(state)
        x_ref = x_refs[k]
        for i in range(sub // SUBLANES):
            r = pl.ds(i * SUBLANES, SUBLANES)
            for c in range(nb):
                s_re, s_im = state[2 * c], state[2 * c + 1]
                n_re = a_re[c] * s_re - a_im[c] * s_im + x_ref[c, r, :]
                n_im = a_re[c] * s_im + a_im[c] * s_re + x_ref[nb + c, r, :]
                x_ref[c, r, :] = n_re
                x_ref[nb + c, r, :] = n_im
                state[2 * c], state[2 * c + 1] = n_re, n_im
        return state

    def emit(k, u_h):
        for h in range(2):
            xs = jnp.concatenate([half_rows(x_refs[k], 0, h, c) for c in range(2 * nb)], axis=1)
            y = _dot(xs.astype(BF16), cst_ref[h])
            g = jax.nn.gelu(y + d_ref[h:h + 1, :] * u_h[h])
            o0_ref[pl.ds(k * sub + h, hrows, stride=2), :] = g[:, 0:LANES]
            o1_ref[pl.ds(k * sub + h, hrows, stride=2), :] = g[:, LANES:2 * LANES]

    state = [st_ref[c] for c in range(2 * nb)]
    u_next = project(0)
    for k in range(len(x_refs)):
        u_k = u_next
        if k + 1 < len(x_refs):
            u_next = project(k + 1)
        state = scan(k, state)
        emit(k, u_k)
    for c in range(2 * nb):
        st_ref[c] = state[c]


def _s5_zoh_kernel(lre_ref, lim_ref, ldt_ref, bre_ref, bim_ref, are_ref, aim_ref, bbre_ref, bbim_ref):
    lr, li = lre_ref[...], lim_ref[...]
    dt = jnp.exp(ldt_ref[...])
    mag = jnp.exp(lr * dt)
    ab_re = mag * jnp.cos(li * dt)
    ab_im = mag * jnp.sin(li * dt)
    den = lr * lr + li * li
    nr = ab_re - 1.0
    ni = ab_im
    q_re = (nr * lr + ni * li) / den
    q_im = (ni * lr - nr * li) / den
    b_re, b_im = bre_ref[...], bim_ref[...]
    q_re, q_im = q_re[:, None, :], q_im[:, None, :]
    are_ref[...] = ab_re
    aim_ref[...] = ab_im
    bbre_ref[...] = q_re * b_re - q_im * b_im
    bbim_ref[...] = q_re * b_im + q_im * b_re


def _s5_discretize(lam_re, lam_im, log_dt, b_re, b_im):
    G, P, H = b_re.shape
    ab_re, ab_im, bbt_re, bbt_im = pl.pallas_call(
        _s5_zoh_kernel,
        out_shape=[jax.ShapeDtypeStruct((G, P), F32)] * 2 + [jax.ShapeDtypeStruct((G, H, P), F32)] * 2,
        name="s5_zoh",
    )(lam_re, lam_im, log_dt[:, None], jnp.transpose(b_re, (0, 2, 1)), jnp.transpose(b_im, (0, 2, 1)))
    return ab_re, ab_im, bbt_re.reshape(G * H, P), bbt_im.reshape(G * H, P)


def _s5_matrices(ab_re, ab_im, bbt_re, bbt_im, c_re, c_im, d, batch):
    G, H, P = c_re.shape
    gh = G // 2
    hw, ns = gh * H, gh * P
    bmask = np.kron(np.eye(gh, dtype=np.float32), np.ones((H, P), np.float32))

    def bmat(bbt):
        return jnp.tile(bbt.reshape(2, hw, P), (1, 1, gh)) * bmask

    def cmat(cc):
        t = jnp.transpose(cc.reshape(2, gh, H, P), (0, 1, 3, 2)).reshape(2, ns, H)
        return jnp.tile(t, (1, 1, gh)) * bmask.T

    bst = jnp.concatenate([bmat(bbt_re), bmat(bbt_im)], axis=2).astype(BF16)
    cst = jnp.concatenate([cmat(c_re), -cmat(c_im)], axis=1).astype(BF16)
    a_re = jnp.tile(ab_re.reshape(2, ns), (batch, 1))
    a_im = jnp.tile(ab_im.reshape(2, ns), (batch, 1))
    return bst, cst, a_re, a_im, d.reshape(2, hw)


def _s5_scan(u0, u1, bst, cst, a_re, a_im, d2):
    n_rows = u0.shape[0]
    ns = a_re.shape[1]
    rows = min(S5_TILE * SUBLANES, n_rows)
    blk = pl.BlockSpec((rows, LANES), lambda t: (t, 0))
    return pl.pallas_call(
        _s5_kernel,
        grid=(n_rows // rows,),
        in_specs=[blk, blk, _const_spec(bst.shape), _const_spec(cst.shape),
                  _const_spec(a_re.shape), _const_spec(a_im.shape), _const_spec(d2.shape)],
        out_specs=[blk, blk],
        out_shape=[jax.ShapeDtypeStruct((n_rows, LANES), F32)] * 2,
        scratch_shapes=[pltpu.VMEM((2 * ns // LANES, SUBLANES, LANES), F32)]
                       + [pltpu.VMEM((2 * ns // LANES, rows // S5_SUBTILES, LANES), F32)] * S5_SUBTILES,
        compiler_params=_params("arbitrary"),
        name="s5_scan",
    )(u0, u1, bst, cst, a_re, a_im, d2)


def _mlp_rows(x, gpre_ref, gpost_ref, w1_ref, w2_ref):
    h = _rms(x, gpre_ref[...]).astype(BF16)
    d_ff = w1_ref.shape[1]
    acc = jnp.zeros(x.shape, F32)
    for f in range(0, d_ff, MLP_FF_CHUNK):
        a = jnp.maximum(_dot(h, w1_ref[:, f:f + MLP_FF_CHUNK]), 0.0)
        acc = acc + _dot((a * a).astype(BF16), w2_ref[f:f + MLP_FF_CHUNK, :])
    return x + _rms(acc, gpost_ref[...])


def _even_post_kernel(x_ref, g0_ref, g1_ref, yb_ref, wglu_ref, wout_ref, g_ref,
                      gpre_ref, gpost_ref, w1_ref, w2_ref, o_ref):
    nb, tm, _ = x_ref.shape
    half_w = wout_ref.shape[0] // 2

    def mixer_tail(b):
        blocks = [ref[pl.ds(2 * b + half, tm, stride=2 * nb), :]
                  for half in range(2) for ref in (g0_ref, g1_ref)]
        ga = jnp.concatenate(blocks, axis=1)
        ya = ga * jax.nn.sigmoid(_dot(ga.astype(BF16), wglu_ref[...]))
        mix = (_dot(ya.astype(BF16), wout_ref[0:half_w, :])
               + _dot(yb_ref[b], wout_ref[half_w:2 * half_w, :]))
        return x_ref[b] + _rms(mix, g_ref[...])

    x1 = jnp.concatenate([mixer_tail(b) for b in range(nb)], axis=0)
    out = _mlp_rows(x1, gpre_ref, gpost_ref, w1_ref, w2_ref)
    for b in range(nb):
        o_ref[b] = out[b * tm:(b + 1) * tm]


def _even_post(x, g0, g1, yb, w_glu, w_out, g, g_pre, g_post, w1s, w2s, layer):
    B, L, D = x.shape
    tm = min(POST_TILE, L)
    tok = lambda r: (0, r, 0)
    gblk = pl.BlockSpec((tm * 2 * B, LANES), lambda r: (r, 0))
    consts = (w_glu, w_out, g, g_pre, g_post)
    return pl.pallas_call(
        _even_post_kernel,
        grid=(L // tm,),
        in_specs=[pl.BlockSpec((B, tm, D), tok), gblk, gblk, pl.BlockSpec((B, tm, 512), tok)]
                 + [_const_spec(c.shape) for c in consts]
                 + [_layer_spec(w1s.shape, layer), _layer_spec(w2s.shape, layer)],
        out_specs=pl.BlockSpec((B, tm, D), tok),
        out_shape=jax.ShapeDtypeStruct((B, L, D), F32),
        compiler_params=_params("parallel"),
        name="even_post_mlp",
    )(x, g0, g1, yb, *consts, w1s, w2s)


def _odd_inproj_rows(x, gpre_ref, win_ref):
    width = win_ref.shape[1] // 3
    h = _rms(x, gpre_ref[...]).astype(BF16)
    return tuple(_dot(h, win_ref[:, n * width:(n + 1) * width]) for n in range(3))


def _odd_mix_rows(x, xc, u, v, t0, pool_ref, pw_ref, ps_ref, lng_ref, lnb_ref, ws, bs_ref,
                  wout_ref, gpost_ref):
    tm = x.shape[0]
    pad = max(POOL_WINDOWS)
    gd = LANES
    n_grp = len(POOL_WINDOWS)
    width = n_grp * gd

    pool_ref[pad:pad + tm, :] = xc
    t = t0 + lax.broadcasted_iota(jnp.int32, (tm, 1), 0)
    yc = []
    for g, w in enumerate(POOL_WINDOWS):
        cols = slice(g * gd, (g + 1) * gd)
        tot = xc[:, cols]
        for lag in range(1, w):
            tot = tot + pool_ref[pl.ds(pad - lag, tm), cols]
        cnt = jnp.minimum(t + 1, w).astype(F32)
        pooled = tot / cnt - xc[:, cols]
        yc.append(_dot(pooled.astype(BF16), pw_ref[g]))
    pool_ref[0:pad, :] = pool_ref[tm:tm + pad, :]
    y_c = jnp.concatenate(yc, axis=-1) * ps_ref[...]

    gv = jax.nn.gelu(v)
    mu = jnp.mean(gv, axis=-1, keepdims=True)
    vc = gv - mu
    vn = vc * lax.rsqrt(jnp.mean(vc * vc, axis=-1, keepdims=True) + EPS)
    vn = (vn * lng_ref[...] + lnb_ref[...]).astype(BF16)
    chunks = []
    for n in range(tm // CHUNK):
        parts = [_dot(ws[g], vn[n * CHUNK:(n + 1) * CHUNK, g * gd:(g + 1) * gd]) for g in range(n_grp)]
        chunks.append(jnp.concatenate(parts, axis=-1) + bs_ref[...])
    y_d = jax.nn.gelu(u) * jnp.concatenate(chunks, axis=0)

    mix = _dot(y_c.astype(BF16), wout_ref[0:width, :]) + _dot(y_d.astype(BF16), wout_ref[width:2 * width, :])
    return x + _rms(mix, gpost_ref[...])


def _odd_layer_kernel(x_ref, gpre_ref, win_ref, pw_ref, ps_ref, lng_ref, lnb_ref, ws_ref,
                      bs_ref, wout_ref, gpost_ref, mgpre_ref, mgpost_ref, w1_ref, w2_ref,
                      o_ref, pool_ref):
    r = pl.program_id(0)
    nb, tm, _ = x_ref.shape
    pad = max(POOL_WINDOWS)

    @pl.when(r == 0)
    def _():
        pool_ref[:, 0:pad, :] = jnp.zeros((nb, pad, pool_ref.shape[2]), F32)

    rows = lax.broadcasted_iota(jnp.int32, (CHUNK, CHUNK), 0)
    cols = lax.broadcasted_iota(jnp.int32, (CHUNK, CHUNK), 1)
    ws = [jnp.where(cols <= rows, ws_ref[g], 0.0).astype(BF16) for g in range(ws_ref.shape[0])]
    proj = {0: _odd_inproj_rows(x_ref[0], gpre_ref, win_ref)}
    x1 = []
    for b in range(nb):
        if b + 1 < nb:
            proj[b + 1] = _odd_inproj_rows(x_ref[b + 1], gpre_ref, win_ref)
        if len(x1) == MLP_GROUP:
            out = _mlp_rows(jnp.concatenate(x1, axis=0), mgpre_ref, mgpost_ref, w1_ref, w2_ref)
            for n in range(MLP_GROUP):
                o_ref[b - MLP_GROUP + n] = out[n * tm:(n + 1) * tm]
            x1 = []
        x1.append(_odd_mix_rows(x_ref[b], *proj.pop(b), r * tm, pool_ref.at[b], pw_ref, ps_ref,
                                lng_ref, lnb_ref, ws, bs_ref, wout_ref, gpost_ref))
    out = _mlp_rows(jnp.concatenate(x1, axis=0), mgpre_ref, mgpost_ref, w1_ref, w2_ref)
    for n in range(len(x1)):
        o_ref[nb - len(x1) + n] = out[n * tm:(n + 1) * tm]


def _odd_layer_call(x, consts, w1s, w2s, layer):
    B, L, D = x.shape
    tm = min(POST_TILE, L)
    width = consts[3].shape[1]
    tok = lambda r: (0, r, 0)
    return pl.pallas_call(
        _odd_layer_kernel,
        grid=(L // tm,),
        in_specs=[pl.BlockSpec((B, tm, D), tok)] + [_const_spec(c.shape) for c in consts]
                 + [_layer_spec(w1s.shape, layer), _layer_spec(w2s.shape, layer)],
        out_specs=pl.BlockSpec((B, tm, D), tok),
        out_shape=jax.ShapeDtypeStruct((B, L, D), F32),
        scratch_shapes=[pltpu.VMEM((B, tm + max(POOL_WINDOWS), width), F32)],
        compiler_params=_params("arbitrary"),
        name="odd_layer",
    )(x, *consts, w1s, w2s)


def _even_layer(x, g_pre, g_post, w_in, lam_re, lam_im, log_dt, b_re, b_im, c_re, c_im, d,
                b_f, mlp_g_pre, mlp_g_post, late, i, layer):
    B, L, D = x.shape
    n_heads = b_f.shape[0]
    s5w = d.shape[0]
    fw = n_heads * HEAD_DIM
    assert s5w == 4 * LANES and w_in.shape[1] == s5w + 3 * fw + n_heads
    wf = jnp.pad(w_in[:, s5w + 3 * fw:], ((0, 0), (0, LANES - n_heads))).astype(BF16)
    assert B <= SUBLANES
    u0, u1, q, k, v, feat, ones = _even_inproj(x, g_pre[None], w_in.astype(BF16), wf, b_f)

    todo = [n for n, w in late.items() if w.dtype != BF16]
    y_b, cast = _fox_attn(q, k, v, feat, ones,
                          tuple(late[n].reshape(-1, late[n].shape[-1]) for n in todo))
    late = {**late, **{n: c.reshape(late[n].shape) for n, c in zip(todo, cast)}}

    ab_re, ab_im, bbt_re, bbt_im = _s5_discretize(lam_re, lam_im, log_dt, b_re, b_im)
    bst, cst, a_re, a_im, d2 = _s5_matrices(ab_re, ab_im, bbt_re, bbt_im, c_re, c_im, d, B)
    g0, g1 = _s5_scan(u0, u1, bst, cst, a_re, a_im, d2)

    x = _even_post(x, g0, g1, y_b, late["w_glu"][i], late["w_out_even"][i], g_post[None],
                   mlp_g_pre[None], mlp_g_post[None], late["mlp_w1"], late["mlp_w2"], layer)
    return x, late


def _odd_layer(x, g_pre, g_post, pool_scale, ln_g, ln_b, w_s, b_s, mlp_g_pre, mlp_g_post,
               late, i, layer):
    bs_full = jnp.repeat(jnp.transpose(b_s), LANES, axis=1)
    consts = (g_pre[None], late["w_in_odd"][i], late["pool_w"][i], pool_scale[None], ln_g[None],
              ln_b[None], w_s, bs_full, late["w_out_odd"][i], g_post[None], mlp_g_pre[None],
              mlp_g_post[None])
    return _odd_layer_call(x, consts, late["mlp_w1"], late["mlp_w2"], layer)


def kernel(x, mix_pre_g, mix_post_g, mlp_pre_g, mlp_post_g, w_in_even, s5_lam_re, s5_lam_im, s5_log_dt, s5_b_re, s5_b_im, s5_c_re, s5_c_im, s5_d, s5_w_glu, fox_b_f, w_out_even, w_in_odd, pool_w, pool_scale, sgu_ln_g, sgu_ln_b, sgu_w_s, sgu_b_s, w_out_odd, mlp_w1, mlp_w2):
    depth = mix_pre_g.shape[0]
    late = dict(mlp_w1=mlp_w1, mlp_w2=mlp_w2, w_glu=s5_w_glu, w_out_even=w_out_even,
                w_in_odd=w_in_odd, pool_w=pool_w, w_out_odd=w_out_odd)
    for l in range(depth):
        i = l // 2
        if l % 2 == 0:
            x, late = _even_layer(
                x, mix_pre_g[l], mix_post_g[l], w_in_even[i], s5_lam_re[i], s5_lam_im[i],
                s5_log_dt[i], s5_b_re[i], s5_b_im[i], s5_c_re[i], s5_c_im[i], s5_d[i],
                fox_b_f[i], mlp_pre_g[l], mlp_post_g[l], late, i, l)
        else:
            x = _odd_layer(x, mix_pre_g[l], mix_post_g[l], pool_scale[i], sgu_ln_g[i], sgu_ln_b[i],
                           sgu_w_s[i], sgu_b_s[i], mlp_pre_g[l], mlp_post_g[l], late, i, l)
    return x
```
